```python
import math
import jax, jax.numpy as jnp
from jax import lax
import numpy as np

D_MODEL = 1024
BATCH = 16
SEQ = 2048
DEPTH = 2

EPS = 1e-6
A_HEADS = 4
A_DK = 48
A_DV = 96
A_LOWRANK = 16
A_GATE_TAU = 16.0
A_CHUNK = 64
B_HEADS = 6
B_HEAD_DIM = 64
IDX_HEADS = 4
IDX_DIM = 32
B_TOPK_MAX = 256
B_QBLOCK = 32
C_HEADS = 6
C_HEAD_DIM = 64
C_PATTERNS = ((128, 1), (512, 4), (2048, 16))
C_QBLOCK = 128
D_WIDTH = 384
D_BLOCKS = 8
D_CONV = 4
D_C = 8.0
REL_BUCKETS = 32
REL_MAX_DIST = 2048
N_SOFTMAX_HEADS = B_HEADS + C_HEADS
N_BRANCH = 4
BRANCH_WIDTH = 384
MOE_GROUPS = 4
MOE_EXPERTS_PER_GROUP = 4
N_EXPERTS = MOE_GROUPS * MOE_EXPERTS_PER_GROUP
MOE_TOPK = 2
MOE_HIDDEN = 512

IN_SPLITS = (A_HEADS * A_DK, A_HEADS * A_DK, A_HEADS * A_DV, A_HEADS * A_DV, A_LOWRANK,
             B_HEADS * B_HEAD_DIM, B_HEADS * B_HEAD_DIM, B_HEADS * B_HEAD_DIM,
             IDX_HEADS * IDX_DIM, IDX_DIM, IDX_HEADS,
             C_HEADS * C_HEAD_DIM, C_HEADS * C_HEAD_DIM, C_HEADS * C_HEAD_DIM,
             D_WIDTH, D_WIDTH)
IN_TOTAL = sum(IN_SPLITS)

kernel_name = "hybrid_gla_dsa_dilated_rglru_hmoe"


def rms_norm(x, gain):
    xf = x.astype(jnp.float32)
    y = xf * lax.rsqrt(jnp.mean(xf * xf, axis=-1, keepdims=True) + EPS)
    return (y * gain.astype(jnp.float32)).astype(x.dtype)


def rel_bucket(dist):
    max_exact = REL_BUCKETS // 2
    n = jnp.maximum(dist, 0)
    nf = jnp.maximum(n, 1).astype(jnp.float32)
    large = max_exact + (jnp.log(nf / max_exact) / math.log(REL_MAX_DIST / max_exact)
                         * (REL_BUCKETS - max_exact)).astype(jnp.int32)
    large = jnp.minimum(large, REL_BUCKETS - 1)
    return jnp.where(n < max_exact, n, large)


def gla_mixer(q, k, v, g, a_lr, w_a2, b_a2, gain):
    f32 = jnp.float32
    bsz, T, _ = q.shape
    n = T // A_CHUNK
    log_a = jax.nn.log_sigmoid((a_lr @ w_a2 + b_a2).astype(f32)) / A_GATE_TAU
    shp_k = (bsz, n, A_CHUNK, A_HEADS, A_DK)
    qf = q.astype(f32).reshape(shp_k) * A_DK ** -0.5
    kf = k.astype(f32).reshape(shp_k)
    vf = v.astype(f32).reshape(bsz, n, A_CHUNK, A_HEADS, A_DV)
    b = jnp.cumsum(log_a.reshape(shp_k), axis=2)
    b_last = b[:, :, -1:]
    q_dec = qf * jnp.exp(b)
    k_dec = kf * jnp.exp(-b)
    k_end = kf * jnp.exp(b_last - b)
    causal = jnp.tril(jnp.ones((A_CHUNK, A_CHUNK), dtype=bool))
    att = jnp.where(causal, jnp.einsum('bnchk,bnshk->bnhcs', q_dec, k_dec), 0.0)
    o_intra = jnp.einsum('bnhcs,bnshv->bnchv', att, vf)
    u = jnp.einsum('bnshk,bnshv->bnhkv', k_end, vf)
    decay = jnp.exp(b_last[:, :, 0])

    def step(S, inp):
        d, uu = inp
        return d[..., None] * S + uu, S

    S0 = jnp.zeros((bsz, A_HEADS, A_DK, A_DV), f32)
    _, S_prev = lax.scan(step, S0, (jnp.moveaxis(decay, 1, 0), jnp.moveaxis(u, 1, 0)))
    S_prev = jnp.moveaxis(S_prev, 0, 1)
    o = o_intra + jnp.einsum('bnchk,bnhkv->bnchv', q_dec, S_prev)
    o = o.reshape(bsz, T, A_HEADS, A_DV)
    o = o * lax.rsqrt(jnp.mean(o * o, axis=-1, keepdims=True) + EPS)
    o = o * gain.astype(f32).reshape(A_HEADS, A_DV)
    o = o.reshape(bsz, T, A_HEADS * A_DV) * jax.nn.silu(g.astype(f32))
    return o.astype(q.dtype)


def dsa_mixer(q, k, v, iq, ik, iw, bias_tab):
    f32 = jnp.float32
    bsz, T, _ = q.shape
    k_sel = min(B_TOPK_MAX, T // 4)
    nb = T // B_QBLOCK
    iqf = iq.astype(f32).reshape(bsz, T, IDX_HEADS, IDX_DIM) * IDX_DIM ** -0.5
    ikf = ik.astype(f32)
    iwf = iw.astype(f32) * IDX_HEADS ** -0.5
    qh = q.astype(f32).reshape(bsz, T, B_HEADS, B_HEAD_DIM) * B_HEAD_DIM ** -0.5
    kh = k.reshape(bsz, T, B_HEADS, B_HEAD_DIM)
    vh = v.reshape(bsz, T, B_HEADS, B_HEAD_DIM)
    key_pos = jnp.arange(T)

    def block(i):
        start = i * B_QBLOCK
        q_pos = start + jnp.arange(B_QBLOCK)
        iq_b = lax.dynamic_slice_in_dim(iqf, start, B_QBLOCK, axis=1)
        iw_b = lax.dynamic_slice_in_dim(iwf, start, B_QBLOCK, axis=1)
        q_b = lax.dynamic_slice_in_dim(qh, start, B_QBLOCK, axis=1)
        s = jax.nn.relu(jnp.einsum('bqhd,bsd->bqhs', iq_b, ikf))
        score = jnp.einsum('bqh,bqhs->bqs', iw_b, s)
        score = jnp.where(key_pos[None, :] <= q_pos[:, None], score, -jnp.inf)
        _, idx = lax.top_k(score, k_sel)
        valid = idx <= q_pos[None, :, None]
        k_g = jax.vmap(lambda kk, ii: kk[ii])(kh, idx).astype(f32)
        v_g = jax.vmap(lambda vv, ii: vv[ii])(vh, idx).astype(f32)
        logits = jnp.einsum('bqhd,bqkhd->bqhk', q_b, k_g)
        bias = bias_tab[rel_bucket(q_pos[None, :, None] - idx)].astype(f32)
        logits = logits + jnp.moveaxis(bias, -1, 2)
        logits = jnp.where(valid[:, :, None, :], logits, -jnp.inf)
        p = jax.nn.softmax(logits, axis=-1)
        return jnp.einsum('bqhk,bqkhd->bqhd', p, v_g).astype(q.dtype)

    out = lax.map(block, jnp.arange(nb))
    return jnp.moveaxis(out, 0, 1).reshape(bsz, T, B_HEADS * B_HEAD_DIM)


def _dilated_window(qh, kh, vh, window, dil, bias_tab):
    bsz, T, H, Dh = qh.shape
    n = T // dil
    span = window // dil
    c = math.gcd(C_QBLOCK, n)
    nb = n // c

    def to_sub(t):
        return t.reshape(bsz, n, dil, H, Dh).transpose(0, 2, 1, 3, 4).reshape(bsz * dil, n, H, Dh)

    pad = ((0, 0), (span, 0), (0, 0), (0, 0))
    qs = to_sub(qh).reshape(bsz * dil, nb, c, H, Dh)
    ks = jnp.pad(to_sub(kh), pad)
    vs = jnp.pad(to_sub(vh), pad)
    kidx = (jnp.arange(nb) * c)[:, None] + jnp.arange(c + span)[None, :]
    kb = ks[:, kidx]
    vb = vs[:, kidx]
    rel = jnp.arange(c)[:, None] - jnp.arange(c + span)[None, :] + span
    kpos = kidx - span
    valid = ((rel >= 0) & (rel <= span))[None] & (kpos >= 0)[:, None, :]
    bias = jnp.transpose(bias_tab[rel_bucket(rel * dil)].astype(jnp.float32), (2, 0, 1))
    logits = jnp.einsum('bnqhd,bnkhd->bnhqk', qs, kb) + bias[None, None]
    logits = jnp.where(valid[None, :, None], logits, -jnp.inf)
    m = jnp.max(logits, axis=-1, keepdims=True)
    p = jnp.exp(logits - m)
    l = jnp.sum(p, axis=-1, keepdims=True)
    o = jnp.einsum('bnhqk,bnkhd->bnqhd', p / l, vb)
    lse = (m + jnp.log(l))[..., 0]
    o = o.reshape(bsz, dil, n, H, Dh).transpose(0, 2, 1, 3, 4).reshape(bsz, T, H, Dh)
    lse = lse.transpose(0, 1, 3, 2).reshape(bsz, dil, n, H).transpose(0, 2, 1, 3).reshape(bsz, T, H)
    return o, lse


def dilated_mixer(q, k, v, bias_tab):
    f32 = jnp.float32
    bsz, T, _ = q.shape
    qh = q.astype(f32).reshape(bsz, T, C_HEADS, C_HEAD_DIM) * C_HEAD_DIM ** -0.5
    kh = k.astype(f32).reshape(bsz, T, C_HEADS, C_HEAD_DIM)
    vh = v.astype(f32).reshape(bsz, T, C_HEADS, C_HEAD_DIM)
    outs, lses = [], []
    for window, dil in C_PATTERNS:
        o, lse = _dilated_window(qh, kh, vh, window, dil, bias_tab)
        outs.append(o)
        lses.append(lse)
    w = jax.nn.softmax(jnp.stack(lses, axis=0), axis=0)
    o = jnp.einsum('pbth,pbthd->bthd', w, jnp.stack(outs, axis=0))
    return o.reshape(bsz, T, C_HEADS * C_HEAD_DIM).astype(q.dtype)


def rglru_mixer(y_in, x_in, conv_w, conv_b, w_r, b_r, w_i, b_i, lam):
    f32 = jnp.float32
    bsz, T, W = x_in.shape
    xc = lax.conv_general_dilated(x_in.astype(f32), conv_w.astype(f32)[:, None, :],
                                  window_strides=(1,), padding=[(D_CONV - 1, 0)],
                                  dimension_numbers=('NWC', 'WIO', 'NWC'),
                                  feature_group_count=W) + conv_b.astype(f32)
    xb = xc.reshape(bsz, T, D_BLOCKS, W // D_BLOCKS)
    r = jax.nn.sigmoid(jnp.einsum('btnc,ncd->btnd', xb, w_r.astype(f32)).reshape(bsz, T, W) + b_r.astype(f32))
    i = jax.nn.sigmoid(jnp.einsum('btnc,ncd->btnd', xb, w_i.astype(f32)).reshape(bsz, T, W) + b_i.astype(f32))
    log_a = -D_C * r * jax.nn.softplus(-lam.astype(f32))
    a = jnp.exp(log_a)
    u = jnp.sqrt(-jnp.expm1(2.0 * log_a)) * (i * xc)

    def combine(left, right):
        a1, b1 = left
        a2, b2 = right
        return a1 * a2, a2 * b1 + b2

    _, h = lax.associative_scan(combine, (a, u), axis=1)
    return (jax.nn.gelu(y_in.astype(f32)) * h).astype(x_in.dtype)


def mixer_block(xn, w_in, a_w2, a_b2, a_gain, conv_w, conv_b, lru_wr, lru_br, lru_wi, lru_bi,
                lru_lambda, w_gate, b_gate, w_branch, w_out, rel_bias):
    proj = xn @ w_in
    offsets = np.cumsum(IN_SPLITS)[:-1].tolist()
    (aq, ak, av, ag, alr, bq, bk, bv, iq, ik, iw, cq, ck, cv, dy, dx) = jnp.split(proj, offsets, axis=-1)
    o_a = gla_mixer(aq, ak, av, ag, alr, a_w2, a_b2, a_gain)
    o_b = dsa_mixer(bq, bk, bv, iq, ik, iw, rel_bias[:, :B_HEADS])
    o_c = dilated_mixer(cq, ck, cv, rel_bias[:, B_HEADS:])
    o_d = rglru_mixer(dy, dx, conv_w, conv_b, lru_wr, lru_br, lru_wi, lru_bi, lru_lambda)
    merged = None
    for gi, ob in enumerate((o_a, o_b, o_c, o_d)):
        gate = jax.nn.sigmoid(xn @ w_gate[gi] + b_gate[gi])
        term = gate * (ob @ w_branch[gi])
        merged = term if merged is None else merged + term
    return merged @ w_out


def hier_moe(x, w_rg, b_rg, w_re, b_re, w_g, w_u, w_d):
    f32 = jnp.float32
    bsz, T, Dm = x.shape
    xt = x.reshape(-1, Dm)
    g_logits = (xt @ w_rg).astype(f32) + b_rg.astype(f32)
    g_prob = jax.nn.softmax(g_logits, axis=-1)
    g_sel = jnp.argmax(g_logits, axis=-1)
    g_w = jnp.take_along_axis(g_prob, g_sel[:, None], axis=-1)
    e_logits = ((xt @ w_re).astype(f32) + b_re.astype(f32)).reshape(-1, MOE_GROUPS, MOE_EXPERTS_PER_GROUP)
    e_in = jnp.take_along_axis(e_logits, g_sel[:, None, None], axis=1)[:, 0]
    top_v, top_i = lax.top_k(e_in, MOE_TOPK)
    top_w = jax.nn.softmax(top_v, axis=-1) * g_w
    expert_id = g_sel[:, None] * MOE_EXPERTS_PER_GROUP + top_i
    gates = jnp.sum(jax.nn.one_hot(expert_id, N_EXPERTS, dtype=f32) * top_w[..., None], axis=1)
    out = jnp.zeros(xt.shape, f32)
    for e in range(N_EXPERTS):
        h = jax.nn.silu(xt @ w_g[e]) * (xt @ w_u[e])
        out = out + gates[:, e:e + 1] * (h @ w_d[e]).astype(f32)
    return out.astype(x.dtype).reshape(bsz, T, Dm)


def setup_inputs(seed: int = 0) -> dict:
    key = jax.random.key(seed)
    ks = jax.random.split(key, 32)
    f32 = jnp.float32

    def nrm(k, shape, fan_in, scale=1.0):
        return jax.random.normal(k, shape, f32) * (scale * fan_in ** -0.5)

    def small(k, shape, s):
        return jax.random.normal(k, shape, f32) * s

    bw = D_WIDTH // D_BLOCKS
    u = jax.random.uniform(ks[11], (DEPTH, D_WIDTH), f32, 0.9, 0.999)
    a0 = u ** (1.0 / D_C)
    return {
        "x": jax.random.normal(ks[0], (BATCH, SEQ, D_MODEL), f32),
        "w_in": nrm(ks[1], (DEPTH, D_MODEL, IN_TOTAL), D_MODEL),
        "a_w2": nrm(ks[2], (DEPTH, A_LOWRANK, A_HEADS * A_DK), A_LOWRANK),
        "a_b2": small(ks[3], (DEPTH, A_HEADS * A_DK), 0.1),
        "a_gain": 1.0 + small(ks[4], (DEPTH, A_HEADS * A_DV), 0.02),
        "conv_w": nrm(ks[5], (DEPTH, D_CONV, D_WIDTH), D_CONV),
        "conv_b": small(ks[6], (DEPTH, D_WIDTH), 0.01),
        "lru_wr": nrm(ks[7], (DEPTH, D_BLOCKS, bw, bw), bw),
        "lru_br": small(ks[8], (DEPTH, D_WIDTH), 0.1),
        "lru_wi": nrm(ks[9], (DEPTH, D_BLOCKS, bw, bw), bw),
        "lru_bi": small(ks[10], (DEPTH, D_WIDTH), 0.1),
        "lru_lambda": jnp.log(a0) - jnp.log1p(-a0),
        "w_gate": nrm(ks[12], (DEPTH, N_BRANCH, D_MODEL, D_MODEL), D_MODEL),
        "b_gate": small(ks[13], (DEPTH, N_BRANCH, D_MODEL), 0.02),
        "w_branch": nrm(ks[14], (DEPTH, N_BRANCH, BRANCH_WIDTH, D_MODEL), BRANCH_WIDTH),
        "w_out": nrm(ks[15], (DEPTH, D_MODEL, D_MODEL), D_MODEL, 0.5),
        "rel_bias": small(ks[16], (REL_BUCKETS, N_SOFTMAX_HEADS), 0.1),
        "norm1": 1.0 + small(ks[17], (DEPTH, D_MODEL), 0.02),
        "norm2": 1.0 + small(ks[18], (DEPTH, D_MODEL), 0.02),
        "norm_f": 1.0 + small(ks[19], (D_MODEL,), 0.02),
        "moe_wrg": nrm(ks[20], (DEPTH, D_MODEL, MOE_GROUPS), D_MODEL),
        "moe_brg": small(ks[21], (DEPTH, MOE_GROUPS), 0.01),
        "moe_wre": nrm(ks[22], (DEPTH, D_MODEL, N_EXPERTS), D_MODEL),
        "moe_bre": small(ks[23], (DEPTH, N_EXPERTS), 0.01),
        "moe_wg": nrm(ks[24], (DEPTH, N_EXPERTS, D_MODEL, MOE_HIDDEN), D_MODEL),
        "moe_wu": nrm(ks[25], (DEPTH, N_EXPERTS, D_MODEL, MOE_HIDDEN), D_MODEL),
        "moe_wd": nrm(ks[26], (DEPTH, N_EXPERTS, MOE_HIDDEN, D_MODEL), MOE_HIDDEN),
    }


def reference(x, w_in, a_w2, a_b2, a_gain, conv_w, conv_b, lru_wr, lru_br, lru_wi, lru_bi,
              lru_lambda, w_gate, b_gate, w_branch, w_out, rel_bias, norm1, norm2, norm_f,
              moe_wrg, moe_brg, moe_wre, moe_bre, moe_wg, moe_wu, moe_wd):
    h = x
    for l in range(DEPTH):
        xn = rms_norm(h, norm1[l])
        h = h + mixer_block(xn, w_in[l], a_w2[l], a_b2[l], a_gain[l], conv_w[l], conv_b[l],
                            lru_wr[l], lru_br[l], lru_wi[l], lru_bi[l], lru_lambda[l],
                            w_gate[l], b_gate[l], w_branch[l], w_out[l], rel_bias)
        h = h + hier_moe(rms_norm(h, norm2[l]), moe_wrg[l], moe_brg[l], moe_wre[l], moe_bre[l],
                         moe_wg[l], moe_wu[l], moe_wd[l])
    return rms_norm(h, norm_f)
```

```python
import functools
import math

import numpy as np
import jax
import jax.numpy as jnp
from jax import lax
from jax.experimental import pallas as pl
from jax.experimental.pallas import tpu as pltpu

F32 = jnp.float32
BF16 = jnp.bfloat16
HIGHEST = lax.Precision.HIGHEST

D_MODEL = 1024
EPS = 1e-6
A_HEADS, A_DK, A_DV, A_LOWRANK, A_GATE_TAU, A_CHUNK = 4, 48, 96, 16, 16.0, 64
B_HEADS, B_HEAD_DIM, IDX_HEADS, IDX_DIM, B_TOPK_MAX = 6, 64, 4, 32, 256
C_HEADS, C_HEAD_DIM = 6, 64
C_PATTERNS = ((128, 1), (512, 4), (2048, 16))
D_WIDTH, D_BLOCKS, D_CONV, D_C = 384, 8, 4, 8.0
REL_BUCKETS, REL_MAX_DIST = 32, 2048
N_SOFTMAX_HEADS = B_HEADS + C_HEADS
MOE_GROUPS, MOE_EXPERTS_PER_GROUP, MOE_HIDDEN = 4, 4, 512
N_EXPERTS = MOE_GROUPS * MOE_EXPERTS_PER_GROUP
IN_SPLITS = (A_HEADS * A_DK, A_HEADS * A_DK, A_HEADS * A_DV, A_HEADS * A_DV, A_LOWRANK,
             B_HEADS * B_HEAD_DIM, B_HEADS * B_HEAD_DIM, B_HEADS * B_HEAD_DIM,
             IDX_HEADS * IDX_DIM, IDX_DIM, IDX_HEADS,
             C_HEADS * C_HEAD_DIM, C_HEADS * C_HEAD_DIM, C_HEADS * C_HEAD_DIM,
             D_WIDTH, D_WIDTH)

LANES = 128
VMEM_LIMIT = 56 * 1024 * 1024
NEG = -1e30
INT_MIN = -2 ** 31

ATT_BLOCK = 256
SEQ_BLOCK = 256
PROJ_ROWS = 512
MOE_ROWS = 1024

GLA_W = 4 * A_HEADS * LANES + LANES
ATT_W = 3 * B_HEADS * B_HEAD_DIM
IDX_W = 3 * LANES
LRU_W = 2 * D_WIDTH
GROUP_WIDTHS = (GLA_W, ATT_W, IDX_W, ATT_W, LRU_W)
NT_DIMS = (((1,), (1,)), ((), ()))
TN_DIMS = (((0,), (0,)), ((), ()))


def _params(*sem):
    return pltpu.CompilerParams(dimension_semantics=sem, vmem_limit_bytes=VMEM_LIMIT)


def _resident(shape):
    zeros = (0,) * len(shape)
    return pl.BlockSpec(shape, lambda *_: zeros, pipeline_mode=pl.Buffered(1))


def _rms(x, gain):
    return x * lax.rsqrt(jnp.mean(x * x, axis=-1, keepdims=True) + EPS) * gain


def _sigmoid(x):
    return 1.0 / (1.0 + jnp.exp(-x))


def _softplus(x):
    return jnp.maximum(x, 0.0) + jnp.log1p(jnp.exp(-jnp.abs(x)))


def _inproj_kernel(x_ref, g_ref, w_ref, *out_refs):
    xn = _rms(x_ref[...], g_ref[...]).astype(BF16)
    off = 0
    for o_ref in out_refs:
        width = o_ref.shape[-1]
        o_ref[...] = jnp.dot(xn, w_ref[:, off:off + width],
                             preferred_element_type=F32).astype(o_ref.dtype)
        off += width


def _inproj(h, gain, w_all):
    n = h.shape[0]
    tm = min(PROJ_ROWS, n)
    total = sum(GROUP_WIDTHS)
    return pl.pallas_call(
        _inproj_kernel,
        grid=(n // tm,),
        in_specs=[pl.BlockSpec((tm, D_MODEL), lambda i: (i, 0)),
                  _resident((1, D_MODEL)),
                  _resident((D_MODEL, total))],
        out_specs=[pl.BlockSpec((tm, w), lambda i: (i, 0)) for w in GROUP_WIDTHS],
        out_shape=[jax.ShapeDtypeStruct((n, w), BF16) for w in GROUP_WIDTHS],
        compiler_params=_params("parallel"),
        name="inproj",
    )(h, gain, w_all)


def _gla_kernel(x_ref, wa2_ref, ba2_ref, gain_ref, o_ref, st_ref, *, tb):
    @pl.when(pl.program_id(1) == 0)
    def _():
        st_ref[...] = jnp.zeros_like(st_ref)

    hw = A_HEADS * LANES
    x = x_ref[...]
    z = jnp.dot(x[:, 4 * hw:].astype(F32), wa2_ref[...], precision=HIGHEST,
                preferred_element_type=F32) + ba2_ref[...]
    log_a = -_softplus(-z) * (1.0 / A_GATE_TAU)

    row = lax.broadcasted_iota(jnp.int32, (tb, tb), 0)
    col = lax.broadcasted_iota(jnp.int32, (tb, tb), 1)
    same = (row // A_CHUNK) == (col // A_CHUNK)
    chunk_ones = jnp.where(same, 1.0, 0.0)
    chunk_tri = jnp.where(col <= row, chunk_ones, 0.0)
    b = jnp.dot(chunk_tri, log_a, precision=HIGHEST, preferred_element_type=F32)
    b_last = jnp.dot(chunk_ones, log_a, precision=HIGHEST, preferred_element_type=F32)

    q = x[:, 0:hw].astype(F32) * (A_DK ** -0.5)
    k = x[:, hw:2 * hw].astype(F32)
    v = x[:, 2 * hw:3 * hw]
    q_dec = (q * jnp.exp(b)).astype(BF16)
    k_dec = (k * jnp.exp(-b)).astype(BF16)
    k_end = (k * jnp.exp(b_last - b)).astype(BF16)
    decay = jnp.exp(b_last)

    r64 = lax.broadcasted_iota(jnp.int32, (A_CHUNK, A_CHUNK), 0)
    c64 = lax.broadcasted_iota(jnp.int32, (A_CHUNK, A_CHUNK), 1)
    causal = c64 <= r64
    heads = []
    for h in range(A_HEADS):
        cs = slice(h * LANES, (h + 1) * LANES)
        state_t = st_ref[h]
        chunks = []
        for c in range(tb // A_CHUNK):
            rs = slice(c * A_CHUNK, (c + 1) * A_CHUNK)
            qc, kc, kec, vc = q_dec[rs, cs], k_dec[rs, cs], k_end[rs, cs], v[rs, cs]
            att = lax.dot_general(qc, kc, NT_DIMS, preferred_element_type=F32)
            att = jnp.where(causal, att, 0.0).astype(BF16)
            oc = jnp.dot(att, vc, preferred_element_type=F32)
            oc += lax.dot_general(qc, state_t.astype(BF16), NT_DIMS, preferred_element_type=F32)
            inc_t = lax.dot_general(vc, kec, TN_DIMS, preferred_element_type=F32)
            state_t = state_t * decay[c * A_CHUNK:c * A_CHUNK + 1, cs] + inc_t
            chunks.append(oc)
        st_ref[h] = state_t
        o_h = jnp.concatenate(chunks, axis=0)
        ms = jnp.sum(o_h * o_h, axis=-1, keepdims=True) * (1.0 / A_DV)
        heads.append(o_h * lax.rsqrt(ms + EPS))
    o = jnp.concatenate(heads, axis=1) * gain_ref[...]
    g = x[:, 3 * hw:4 * hw].astype(F32)
    o_ref[...] = (o * (g * _sigmoid(g))).astype(o_ref.dtype)


def _gla(x3, wa2, ba2, gain):
    bsz, t, _ = x3.shape
    tb = min(SEQ_BLOCK, t)
    hw = A_HEADS * LANES
    return pl.pallas_call(
        functools.partial(_gla_kernel, tb=tb),
        grid=(bsz, t // tb),
        in_specs=[pl.BlockSpec((None, tb, GLA_W), lambda b, i: (b, i, 0)),
                  _resident((LANES, hw)), _resident((1, hw)), _resident((1, hw))],
        out_specs=pl.BlockSpec((None, tb, hw), lambda b, i: (b, i, 0)),
        out_shape=jax.ShapeDtypeStruct((bsz, t, hw), BF16),
        scratch_shapes=[pltpu.VMEM((A_HEADS, LANES, LANES), F32)],
        compiler_params=_params("parallel", "arbitrary"),
        name="gla",
    )(x3, wa2, ba2, gain)


def _rglru_kernel(d_ref, cw_ref, cb_ref, wri_ref, bri_ref, lam_ref, o_ref, xbuf, hc_ref, *, tb):
    halo = 8

    @pl.when(pl.program_id(1) == 0)
    def _():
        xbuf[0:halo, :] = jnp.zeros((halo, D_WIDTH), F32)
        hc_ref[...] = jnp.zeros_like(hc_ref)

    d = d_ref[...]
    y = d[:, 0:D_WIDTH].astype(F32)
    x = d[:, D_WIDTH:].astype(F32)
    xbuf[halo:halo + tb, :] = x
    xc = cb_ref[...]
    for j in range(D_CONV):
        xc = xc + cw_ref[j:j + 1, :] * xbuf[pl.ds(halo - (D_CONV - 1) + j, tb), :]
    xbuf[0:halo, :] = x[tb - halo:tb, :]

    ri = jnp.dot(xc.astype(BF16), wri_ref[...], preferred_element_type=F32) + bri_ref[...]
    r = _sigmoid(ri[:, 0:D_WIDTH])
    ig = _sigmoid(ri[:, D_WIDTH:])
    log_a = (-D_C) * r * _softplus(-lam_ref[...])
    a = jnp.exp(log_a)
    u = jnp.sqrt(-jnp.tanh(log_a) * (a * a + 1.0)) * (ig * xc)

    row = lax.broadcasted_iota(jnp.int32, (tb, D_WIDTH), 0)
    s = 1
    while s < tb:
        valid = row >= s
        u = jnp.where(valid, a * pltpu.roll(u, s, axis=0), 0.0) + u
        a = jnp.where(valid, a * pltpu.roll(a, s, axis=0), a)
        s *= 2
    hs = u + a * hc_ref[0:1, :]
    hc_ref[...] = jnp.broadcast_to(hs[tb - 1:tb, :], hc_ref.shape)
    gelu = 0.5 * y * (1.0 + jnp.tanh(math.sqrt(2.0 / math.pi) * (y + 0.044715 * (y * y * y))))
    o_ref[...] = (gelu * hs).astype(o_ref.dtype)


def _rglru(d3, conv_w, conv_b, w_ri, b_ri, lam):
    bsz, t, _ = d3.shape
    tb = min(SEQ_BLOCK, t)
    return pl.pallas_call(
        functools.partial(_rglru_kernel, tb=tb),
        grid=(bsz, t // tb),
        in_specs=[pl.BlockSpec((None, tb, LRU_W), lambda b, i: (b, i, 0)),
                  _resident((D_CONV, D_WIDTH)), _resident((1, D_WIDTH)),
                  _resident((D_WIDTH, 2 * D_WIDTH)), _resident((1, 2 * D_WIDTH)),
                  _resident((1, D_WIDTH))],
        out_specs=pl.BlockSpec((None, tb, D_WIDTH), lambda b, i: (b, i, 0)),
        out_shape=jax.ShapeDtypeStruct((bsz, t, D_WIDTH), BF16),
        scratch_shapes=[pltpu.VMEM((tb + 8, D_WIDTH), F32), pltpu.VMEM((8, D_WIDTH), F32)],
        compiler_params=_params("parallel", "arbitrary"),
        name="rglru",
    )(d3, conv_w, conv_b, w_ri, b_ri, lam)


def _bucket_edges():
    n = np.arange(REL_MAX_DIST + 1)
    max_exact = REL_BUCKETS // 2
    nf = np.maximum(n, 1).astype(np.float64)
    large = max_exact + (np.log(nf / max_exact) / math.log(REL_MAX_DIST / max_exact)
                         * (REL_BUCKETS - max_exact)).astype(np.int64)
    bucket = np.where(n < max_exact, n, np.minimum(large, REL_BUCKETS - 1))
    assert np.all(np.diff(bucket) >= 0)
    return [int(np.argmax(bucket >= b)) for b in range(REL_BUCKETS)]


def _tbias_kernel(tab_ref, o_ref, *, bs, edges):
    h = pl.program_id(0)
    d = pl.program_id(1)
    row = lax.broadcasted_iota(jnp.int32, (bs, bs), 0)
    col = lax.broadcasted_iota(jnp.int32, (bs, bs), 1)
    dist = d * bs + row - col
    val = jnp.full((bs, bs), tab_ref[0, h], F32)
    for b in range(1, REL_BUCKETS):
        val = jnp.where(dist >= edges[b], tab_ref[b, h], val)
    mult = jnp.zeros((bs, bs), jnp.int32)
    for window, dil in C_PATTERNS:
        hit = jnp.where((dist & (dil - 1)) == 0, 1, 0)
        mult = mult + jnp.where(dist <= window, hit, 0)
    logm = jnp.full((bs, bs), NEG, F32)
    for m in range(1, len(C_PATTERNS) + 1):
        logm = jnp.where(mult == m, math.log(m), logm)
    val = jnp.where(h >= B_HEADS, val + logm, val)
    o_ref[...] = jnp.where(dist >= 0, val, NEG)


def _tbias(rel_bias, t):
    bs = min(ATT_BLOCK, t)
    nd = t // bs
    for _, dil in C_PATTERNS:
        assert dil & (dil - 1) == 0
    return pl.pallas_call(
        functools.partial(_tbias_kernel, bs=bs, edges=_bucket_edges()),
        grid=(N_SOFTMAX_HEADS, nd),
        in_specs=[pl.BlockSpec(memory_space=pltpu.SMEM)],
        out_specs=pl.BlockSpec((None, None, bs, bs), lambda h, d: (h, d, 0, 0)),
        out_shape=jax.ShapeDtypeStruct((N_SOFTMAX_HEADS, nd, bs, bs), F32),
        compiler_params=_params("parallel", "parallel"),
        name="tbias",
    )(rel_bias)


def _attend(q_ref, k_ref, v_ref, tb_ref, madd_ref, o_ref, i, bs):
    n_pairs = q_ref.shape[-1] // LANES
    lane = lax.broadcasted_iota(jnp.int32, (bs, LANES), 1)
    low = lane < B_HEAD_DIM
    for hp in range(n_pairs):
        cs = slice(hp * LANES, (hp + 1) * LANES)
        qp = q_ref[:, cs] * (B_HEAD_DIM ** -0.5)
        outs = []
        for j in range(2):
            qm = jnp.where(low if j == 0 else jnp.logical_not(low), qp, 0.0).astype(BF16)
            h = 2 * hp + j

            def body(c, carry, qm=qm, h=h, cs=cs):
                m, l, acc = carry
                start = pl.multiple_of(c * bs, bs)
                ks = k_ref[pl.ds(start, bs), cs]
                vs = v_ref[pl.ds(start, bs), cs]
                s = lax.dot_general(qm, ks, NT_DIMS, preferred_element_type=F32)
                s = s + tb_ref[h, i - c]
                if madd_ref is not None:
                    s = s + madd_ref[c]
                m_new = jnp.maximum(m, jnp.max(s, axis=1, keepdims=True))
                alpha = jnp.exp(m - m_new)
                p = jnp.exp(s - m_new)
                l = alpha * l + jnp.sum(p, axis=1, keepdims=True)
                acc = alpha * acc + jnp.dot(p.astype(BF16), vs, preferred_element_type=F32)
                return m_new, l, acc

            init = (jnp.full((bs, 1), NEG, F32), jnp.zeros((bs, 1), F32),
                    jnp.zeros((bs, LANES), F32))
            _, l, acc = lax.fori_loop(0, i + 1, body, init)
            outs.append(acc / l)
        o_ref[:, cs] = jnp.where(low, outs[0], outs[1]).astype(o_ref.dtype)


def _dilated_kernel(q_ref, k_ref, v_ref, tb_ref, o_ref, *, bs):
    _attend(q_ref, k_ref, v_ref, tb_ref, None, o_ref, pl.program_id(1), bs)


def _dsa_kernel(q_ref, k_ref, v_ref, iq_ref, ik_ref, iw_ref, tb_ref, o_ref,
                key_ref, madd_ref, *, bs, k_sel):
    i = pl.program_id(1)
    lane = lax.broadcasted_iota(jnp.int32, (bs, LANES), 1)
    row = lax.broadcasted_iota(jnp.int32, (bs, bs), 0)
    col = lax.broadcasted_iota(jnp.int32, (bs, bs), 1)

    iq = iq_ref[...]
    iw = iw_ref[...].astype(F32) * (IDX_HEADS ** -0.5 * IDX_DIM ** -0.5)
    iq_heads = [jnp.where((lane // IDX_DIM) == hh, iq, 0.0).astype(BF16) for hh in range(IDX_HEADS)]
    iw_heads = [iw[:, hh:hh + 1] for hh in range(IDX_HEADS)]

    def fill(c, _):
        ikc = ik_ref[pl.ds(pl.multiple_of(c * bs, bs), bs), :]
        sc = jnp.zeros((bs, bs), F32)
        for hh in range(IDX_HEADS):
            raw = lax.dot_general(iq_heads[hh], ikc, NT_DIMS, preferred_element_type=F32)
            sc = sc + iw_heads[hh] * jnp.maximum(raw, 0.0)
        sc = jnp.where(sc == 0.0, 0.0, sc)
        bits = pltpu.bitcast(sc, jnp.int32)
        key = jnp.where(bits < 0, bits ^ 0x7FFFFFFF, bits)
        visible = jnp.logical_or(c < i, col <= row)
        key_ref[c] = jnp.where(visible, key, INT_MIN)
        return 0

    lax.fori_loop(0, i + 1, fill, 0)

    def count(pred, thr):
        def body(c, acc):
            kk = key_ref[c]
            for j in range(bs // LANES):
                acc = acc + jnp.where(pred(kk[:, j * LANES:(j + 1) * LANES], thr), 1.0, 0.0)
            return acc
        acc = lax.fori_loop(0, i + 1, body, jnp.zeros((bs, LANES), F32))
        return jnp.sum(acc, axis=1, keepdims=True)

    ge = lambda a, b: a >= b
    gt = lambda a, b: a > b
    kf = float(k_sel)
    zero = jnp.zeros((bs, LANES), jnp.int32)
    thr = jnp.where(count(ge, zero) >= kf, zero, INT_MIN)

    def bisect(it, thr):
        cand = thr + lax.shift_left(jnp.int32(1), 30 - it)
        return jnp.where(count(ge, cand) >= kf, cand, thr)

    thr = lax.fori_loop(0, 31, bisect, thr)

    need = kf - count(gt, thr)
    thr2 = jnp.concatenate([thr] * (bs // LANES), axis=1)
    prefix = jnp.where(row <= col, 1.0, 0.0).astype(BF16)

    def select(c, seen):
        kk = key_ref[c]
        tie = kk == thr2
        rank = seen + jnp.dot(jnp.where(tie, 1.0, 0.0).astype(BF16), prefix,
                              preferred_element_type=F32)
        tie_add = jnp.where(tie, jnp.where(rank <= need, 0.0, NEG), NEG)
        madd_ref[c] = jnp.where(kk > thr2, 0.0, tie_add)
        return rank[:, bs - 1:bs]

    lax.fori_loop(0, i + 1, select, jnp.zeros((bs, 1), F32))

    _attend(q_ref, k_ref, v_ref, tb_ref, madd_ref, o_ref, i, bs)


def _attention_specs(t, bs, width):
    q_spec = pl.BlockSpec((None, bs, width), lambda b, i: (b, i, 0))
    k_spec = pl.BlockSpec((None, t, width), lambda b, i: (b, 0, 1))
    v_spec = pl.BlockSpec((None, t, width), lambda b, i: (b, 0, 2))
    return q_spec, k_spec, v_spec


def _dilated(qkv3, tbias):
    bsz, t, _ = qkv3.shape
    bs = min(ATT_BLOCK, t)
    width = C_HEADS * C_HEAD_DIM
    nd = t // bs
    return pl.pallas_call(
        functools.partial(_dilated_kernel, bs=bs),
        grid=(bsz, nd),
        in_specs=[*_attention_specs(t, bs, width),
                  pl.BlockSpec((C_HEADS, nd, bs, bs), lambda b, i: (1, 0, 0, 0))],
        out_specs=pl.BlockSpec((None, bs, width), lambda b, i: (b, i, 0)),
        out_shape=jax.ShapeDtypeStruct((bsz, t, width), BF16),
        compiler_params=_params("parallel", "arbitrary"),
        name="dilated",
    )(qkv3, qkv3, qkv3, tbias)


def _dsa(qkv3, idx3, tbias):
    bsz, t, _ = qkv3.shape
    bs = min(ATT_BLOCK, t)
    width = B_HEADS * B_HEAD_DIM
    nd = t // bs
    k_sel = min(B_TOPK_MAX, t // 4)
    return pl.pallas_call(
        functools.partial(_dsa_kernel, bs=bs, k_sel=k_sel),
        grid=(bsz, nd),
        in_specs=[*_attention_specs(t, bs, width),
                  pl.BlockSpec((None, bs, LANES), lambda b, i: (b, i, 0)),
                  pl.BlockSpec((None, t, LANES), lambda b, i: (b, 0, 1)),
                  pl.BlockSpec((None, bs, LANES), lambda b, i: (b, i, 2)),
                  pl.BlockSpec((B_HEADS, nd, bs, bs), lambda b, i: (0, 0, 0, 0))],
        out_specs=pl.BlockSpec((None, bs, width), lambda b, i: (b, i, 0)),
        out_shape=jax.ShapeDtypeStruct((bsz, t, width), BF16),
        scratch_shapes=[pltpu.VMEM((nd, bs, bs), jnp.int32), pltpu.VMEM((nd, bs, bs), F32)],
        compiler_params=_params("parallel", "arbitrary"),
        name="dsa",
    )(qkv3, qkv3, qkv3, idx3, idx3, idx3, tbias)


def _route(logits):
    lane = lax.broadcasted_iota(jnp.int32, logits.shape, 1).astype(F32)
    far = float(LANES)

    def first_argmax(vals):
        top = jnp.max(vals, axis=1, keepdims=True)
        return top, jnp.min(jnp.where(vals == top, lane, far), axis=1, keepdims=True)

    is_group = lane < MOE_GROUPS
    g_top, g_sel = first_argmax(jnp.where(is_group, logits, NEG))
    g_w = 1.0 / jnp.sum(jnp.where(is_group, jnp.exp(logits - g_top), 0.0), axis=1, keepdims=True)
    lo = MOE_GROUPS + MOE_EXPERTS_PER_GROUP * g_sel
    e_logits = jnp.where(lane >= lo, jnp.where(lane < lo + MOE_EXPERTS_PER_GROUP, logits, NEG), NEG)
    v1, i1 = first_argmax(e_logits)
    v2, i2 = first_argmax(jnp.where(lane == i1, NEG, e_logits))
    e2 = jnp.exp(v2 - v1)
    w1 = 1.0 / (1.0 + e2)
    return jnp.where(lane == i1, w1 * g_w, jnp.where(lane == i2, e2 * w1 * g_w, 0.0))


def _merge_kernel(h_ref, g1_ref, oa_ref, ob_ref, oc_ref, od_ref, wg_ref, bg_ref,
                  wba_ref, wbb_ref, wbc_ref, wbd_ref, wo_ref, g2_ref, wr_ref, br_ref,
                  h1_ref, xn2_ref, gates_ref):
    h = h_ref[...]
    xn = _rms(h, g1_ref[...]).astype(BF16)
    merged = None
    for gi, (o_ref, wb_ref) in enumerate(((oa_ref, wba_ref), (ob_ref, wbb_ref),
                                          (oc_ref, wbc_ref), (od_ref, wbd_ref))):
        gate = _sigmoid(jnp.dot(xn, wg_ref[gi], preferred_element_type=F32) + bg_ref[gi:gi + 1, :])
        term = gate * jnp.dot(o_ref[...], wb_ref[...], preferred_element_type=F32)
        merged = term if merged is None else merged + term
    h1 = h + jnp.dot(merged.astype(BF16), wo_ref[...], preferred_element_type=F32)
    h1_ref[...] = h1
    xn2 = _rms(h1, g2_ref[...])
    xn2_ref[...] = xn2.astype(BF16)
    logits = jnp.dot(xn2, wr_ref[...], precision=HIGHEST, preferred_element_type=F32) + br_ref[...]
    gates_ref[...] = _route(logits)


def _merge(h, g1, oa, ob, oc, od, wg, bg, wba, wbb, wbc, wbd, wo, g2, wr, br):
    n = h.shape[0]
    tm = min(PROJ_ROWS, n)
    rows = lambda w: pl.BlockSpec((tm, w), lambda i: (i, 0))
    return pl.pallas_call(
        _merge_kernel,
        grid=(n // tm,),
        in_specs=[rows(D_MODEL), _resident(g1.shape),
                  rows(oa.shape[1]), rows(ob.shape[1]), rows(oc.shape[1]), rows(od.shape[1]),
                  _resident(wg.shape), _resident(bg.shape),
                  _resident(wba.shape), _resident(wbb.shape), _resident(wbc.shape),
                  _resident(wbd.shape), _resident(wo.shape), _resident(g2.shape),
                  _resident(wr.shape), _resident(br.shape)],
        out_specs=[rows(D_MODEL), rows(D_MODEL), rows(LANES)],
        out_shape=[jax.ShapeDtypeStruct((n, D_MODEL), F32),
                   jax.ShapeDtypeStruct((n, D_MODEL), BF16),
                   jax.ShapeDtypeStruct((n, LANES), F32)],
        compiler_params=_params("parallel"),
        name="merge",
    )(h, g1, oa, ob, oc, od, wg, bg, wba, wbb, wbc, wbd, wo, g2, wr, br)


def _moe_kernel(x_ref, gates_ref, h1_ref, wg_ref, wu_ref, wd_ref, gf_ref, o_ref, acc_ref,
                *, final_norm):
    e = pl.program_id(1)

    @pl.when(e == 0)
    def _():
        acc_ref[...] = h1_ref[...]

    x = x_ref[...]
    hg = jnp.dot(x, wg_ref[...], preferred_element_type=F32)
    hu = jnp.dot(x, wu_ref[...], preferred_element_type=F32)
    hid = (hg * _sigmoid(hg) * hu).astype(BF16)
    y = jnp.dot(hid, wd_ref[...], preferred_element_type=F32)
    lane = lax.broadcasted_iota(jnp.int32, gates_ref.shape, 1)
    gate = jnp.sum(jnp.where(lane == e + MOE_GROUPS, gates_ref[...], 0.0), axis=1, keepdims=True)
    acc_ref[...] += gate * y

    @pl.when(e == N_EXPERTS - 1)
    def _():
        out = acc_ref[...]
        o_ref[...] = _rms(out, gf_ref[...]) if final_norm else out


def _moe(xn2, gates, h1, wg, wu, wd, gf, final_norm):
    n = xn2.shape[0]
    tm = min(MOE_ROWS, n)
    rows = lambda w: pl.BlockSpec((tm, w), lambda i, e: (i, 0))
    return pl.pallas_call(
        functools.partial(_moe_kernel, final_norm=final_norm),
        grid=(n // tm, N_EXPERTS),
        in_specs=[rows(D_MODEL), rows(LANES), rows(D_MODEL),
                  pl.BlockSpec((None, D_MODEL, MOE_HIDDEN), lambda i, e: (e, 0, 0)),
                  pl.BlockSpec((None, D_MODEL, MOE_HIDDEN), lambda i, e: (e, 0, 0)),
                  pl.BlockSpec((None, MOE_HIDDEN, D_MODEL), lambda i, e: (e, 0, 0)),
                  pl.BlockSpec((1, D_MODEL), lambda i, e: (0, 0))],
        out_specs=rows(D_MODEL),
        out_shape=jax.ShapeDtypeStruct((n, D_MODEL), F32),
        scratch_shapes=[pltpu.VMEM((tm, D_MODEL), F32)],
        compiler_params=_params("parallel", "arbitrary"),
        name="moe",
    )(xn2, gates, h1, wg, wu, wd, gf)


def _inproj_columns():
    offs = np.concatenate([[0], np.cumsum(IN_SPLITS)])
    (aq, ak, av, ag, alr, bq, bk, bv, iq, ik, iw, cq, ck, cv, dy, dx) = offs[:-1]
    cols = []

    def per_head(base, width):
        for h in range(A_HEADS):
            cols.extend(list(range(base + h * width, base + (h + 1) * width)) + [-1] * (LANES - width))

    per_head(aq, A_DK)
    per_head(ak, A_DK)
    per_head(av, A_DV)
    per_head(ag, A_DV)
    cols.extend(list(range(alr, alr + A_LOWRANK)) + [-1] * (LANES - A_LOWRANK))
    cols.extend(range(bq, bq + ATT_W))
    cols.extend(range(iq, iq + IDX_HEADS * IDX_DIM))
    cols.extend(list(range(ik, ik + IDX_DIM)) * IDX_HEADS)
    cols.extend(list(range(iw, iw + IDX_HEADS)) + [-1] * (LANES - IDX_HEADS))
    cols.extend(range(cq, cq + ATT_W))
    cols.extend(range(dy, dy + LRU_W))
    cols = np.asarray(cols, np.int32)
    assert cols.shape[0] == sum(GROUP_WIDTHS)
    return cols


def _pad_heads(a, width, axis):
    a = jnp.moveaxis(a, axis, -1)
    a = a.reshape(a.shape[:-1] + (A_HEADS, width))
    a = jnp.pad(a, [(0, 0)] * (a.ndim - 1) + [(0, LANES - width)])
    return jnp.moveaxis(a.reshape(a.shape[:-2] + (A_HEADS * LANES,)), -1, axis)


def _block_diag(w):
    nb, bw, _ = w.shape
    eye = jnp.eye(nb, dtype=w.dtype)
    return jnp.einsum('ncd,nm->ncmd', w, eye).reshape(nb * bw, nb * bw)


def kernel(x, w_in, a_w2, a_b2, a_gain, conv_w, conv_b, lru_wr, lru_br, lru_wi, lru_bi, lru_lambda, w_gate, b_gate, w_branch, w_out, rel_bias, norm1, norm2, norm_f, moe_wrg, moe_brg, moe_wre, moe_bre, moe_wg, moe_wu, moe_wd):
    bsz, t, _ = x.shape
    n = bsz * t
    depth = w_in.shape[0]
    cols = _inproj_columns()
    tbias = _tbias(rel_bias, t)
    h = x.reshape(n, D_MODEL)
    for l in range(depth):
        w_all = jnp.where(cols[None, :] >= 0, w_in[l][:, np.maximum(cols, 0)], 0.0).astype(BF16)
        gla_in, dsa_in, idx_in, dil_in, lru_in = _inproj(h, norm1[l][None, :], w_all)

        wa2 = jnp.pad(_pad_heads(a_w2[l], A_DK, 1), ((0, LANES - A_LOWRANK), (0, 0)))
        o_a = _gla(gla_in.reshape(bsz, t, GLA_W), wa2, _pad_heads(a_b2[l], A_DK, 0)[None, :],
                   _pad_heads(a_gain[l], A_DV, 0)[None, :])
        o_b = _dsa(dsa_in.reshape(bsz, t, ATT_W), idx_in.reshape(bsz, t, IDX_W), tbias)
        o_c = _dilated(dil_in.reshape(bsz, t, ATT_W), tbias)
        w_ri = jnp.concatenate([_block_diag(lru_wr[l]), _block_diag(lru_wi[l])], axis=1).astype(BF16)
        b_ri = jnp.concatenate([lru_br[l], lru_bi[l]])[None, :]
        o_d = _rglru(lru_in.reshape(bsz, t, LRU_W), conv_w[l], conv_b[l][None, :], w_ri, b_ri,
                     lru_lambda[l][None, :])

        w_router = jnp.pad(jnp.concatenate([moe_wrg[l], moe_wre[l]], axis=1),
                           ((0, 0), (0, LANES - MOE_GROUPS - N_EXPERTS)))
        b_router = jnp.pad(jnp.concatenate([moe_brg[l], moe_bre[l]]),
                           (0, LANES - MOE_GROUPS - N_EXPERTS))[None, :]
        h1, xn2, gates = _merge(
            h, norm1[l][None, :], o_a.reshape(n, -1), o_b.reshape(n, -1), o_c.reshape(n, -1),
            o_d.reshape(n, -1), w_gate[l].astype(BF16), b_gate[l],
            _pad_heads(w_branch[l, 0], A_DV, 0).astype(BF16), w_branch[l, 1].astype(BF16),
            w_branch[l, 2].astype(BF16), w_branch[l, 3].astype(BF16), w_out[l].astype(BF16),
            norm2[l][None, :], w_router, b_router)
        h = _moe(xn2, gates, h1, moe_wg[l].astype(BF16), moe_wu[l].astype(BF16),
                 moe_wd[l].astype(BF16), norm_f[None, :], final_norm=(l == depth - 1))
    return h.reshape(bsz, t, D_MODEL)
```

```python
import functools
import math

import numpy as np
import jax
import jax.numpy as jnp
from jax import lax
from jax.experimental import pallas as pl
from jax.experimental.pallas import tpu as pltpu

F32 = jnp.float32
BF16 = jnp.bfloat16
HIGHEST = lax.Precision.HIGHEST

D_MODEL = 1024
EPS = 1e-6
A_HEADS, A_DK, A_DV, A_LOWRANK, A_GATE_TAU, A_CHUNK = 4, 48, 96, 16, 16.0, 64
B_HEADS, B_HEAD_DIM, IDX_HEADS, IDX_DIM, B_TOPK_MAX = 6, 64, 4, 32, 256
C_HEADS, C_HEAD_DIM = 6, 64
C_PATTERNS = ((128, 1), (512, 4), (2048, 16))
D_WIDTH, D_BLOCKS, D_CONV, D_C = 384, 8, 4, 8.0
REL_BUCKETS, REL_MAX_DIST = 32, 2048
N_SOFTMAX_HEADS = B_HEADS + C_HEADS
MOE_GROUPS, MOE_EXPERTS_PER_GROUP, MOE_HIDDEN = 4, 4, 512
N_EXPERTS = MOE_GROUPS * MOE_EXPERTS_PER_GROUP
IN_SPLITS = (A_HEADS * A_DK, A_HEADS * A_DK, A_HEADS * A_DV, A_HEADS * A_DV, A_LOWRANK,
             B_HEADS * B_HEAD_DIM, B_HEADS * B_HEAD_DIM, B_HEADS * B_HEAD_DIM,
             IDX_HEADS * IDX_DIM, IDX_DIM, IDX_HEADS,
             C_HEADS * C_HEAD_DIM, C_HEADS * C_HEAD_DIM, C_HEADS * C_HEAD_DIM,
             D_WIDTH, D_WIDTH)

LANES = 128
VMEM_LIMIT = 56 * 1024 * 1024
NEG = -1e30
INT_MIN = -2 ** 31

ATT_BLOCK = 256
SEQ_BLOCK = 256
PROJ_ROWS = 512
MOE_ROWS = 1024

GLA_W = 4 * A_HEADS * LANES + LANES
ATT_W = 3 * B_HEADS * B_HEAD_DIM
IDX_W = 3 * LANES
LRU_W = 2 * D_WIDTH
GROUP_WIDTHS = (GLA_W, ATT_W, IDX_W, ATT_W, LRU_W)
NT_DIMS = (((1,), (1,)), ((), ()))
TN_DIMS = (((0,), (0,)), ((), ()))


def _params(*sem):
    return pltpu.CompilerParams(dimension_semantics=sem, vmem_limit_bytes=VMEM_LIMIT)


def _resident(shape):
    zeros = (0,) * len(shape)
    return pl.BlockSpec(shape, lambda *_: zeros, pipeline_mode=pl.Buffered(1))


def _rms(x, gain):
    return x * lax.rsqrt(jnp.mean(x * x, axis=-1, keepdims=True) + EPS) * gain


def _sigmoid(x):
    return 1.0 / (1.0 + jnp.exp(-x))


def _softplus(x):
    return jnp.maximum(x, 0.0) + jnp.log1p(jnp.exp(-jnp.abs(x)))


def _inproj_kernel(x_ref, g_ref, w_ref, *out_refs):
    xn = _rms(x_ref[...], g_ref[...]).astype(BF16)
    off = 0
    for o_ref in out_refs:
        width = o_ref.shape[-1]
        o_ref[...] = jnp.dot(xn, w_ref[:, off:off + width],
                             preferred_element_type=F32).astype(o_ref.dtype)
        off += width


def _inproj(h, gain, w_all):
    n = h.shape[0]
    tm = min(PROJ_ROWS, n)
    total = sum(GROUP_WIDTHS)
    return pl.pallas_call(
        _inproj_kernel,
        grid=(n // tm,),
        in_specs=[pl.BlockSpec((tm, D_MODEL), lambda i: (i, 0)),
                  _resident((1, D_MODEL)),
                  _resident((D_MODEL, total))],
        out_specs=[pl.BlockSpec((tm, w), lambda i: (i, 0)) for w in GROUP_WIDTHS],
        out_shape=[jax.ShapeDtypeStruct((n, w), BF16) for w in GROUP_WIDTHS],
        compiler_params=_params("parallel"),
        name="inproj",
    )(h, gain, w_all)


def _gla_kernel(x_ref, wa2_ref, ba2_ref, gain_ref, o_ref, st_ref, *, tb):
    @pl.when(pl.program_id(1) == 0)
    def _():
        st_ref[...] = jnp.zeros_like(st_ref)

    hw = A_HEADS * LANES
    x = x_ref[...]
    z = jnp.dot(x[:, 4 * hw:].astype(F32), wa2_ref[...], precision=HIGHEST,
                preferred_element_type=F32) + ba2_ref[...]
    log_a = -_softplus(-z) * (1.0 / A_GATE_TAU)

    row = lax.broadcasted_iota(jnp.int32, (tb, tb), 0)
    col = lax.broadcasted_iota(jnp.int32, (tb, tb), 1)
    same = (row // A_CHUNK) == (col // A_CHUNK)
    chunk_ones = jnp.where(same, 1.0, 0.0)
    chunk_tri = jnp.where(col <= row, chunk_ones, 0.0)
    b = jnp.dot(chunk_tri, log_a, precision=HIGHEST, preferred_element_type=F32)
    b_last = jnp.dot(chunk_ones, log_a, precision=HIGHEST, preferred_element_type=F32)

    q = x[:, 0:hw].astype(F32) * (A_DK ** -0.5)
    k = x[:, hw:2 * hw].astype(F32)
    v = x[:, 2 * hw:3 * hw]
    q_dec = (q * jnp.exp(b)).astype(BF16)
    k_dec = (k * jnp.exp(-b)).astype(BF16)
    k_end = (k * jnp.exp(b_last - b)).astype(BF16)
    decay = jnp.exp(b_last)

    r64 = lax.broadcasted_iota(jnp.int32, (A_CHUNK, A_CHUNK), 0)
    c64 = lax.broadcasted_iota(jnp.int32, (A_CHUNK, A_CHUNK), 1)
    causal = c64 <= r64
    heads = []
    for h in range(A_HEADS):
        cs = slice(h * LANES, (h + 1) * LANES)
        state_t = st_ref[h]
        chunks = []
        for c in range(tb // A_CHUNK):
            rs = slice(c * A_CHUNK, (c + 1) * A_CHUNK)
            qc, kc, kec, vc = q_dec[rs, cs], k_dec[rs, cs], k_end[rs, cs], v[rs, cs]
            att = lax.dot_general(qc, kc, NT_DIMS, preferred_element_type=F32)
            att = jnp.where(causal, att, 0.0).astype(BF16)
            oc = jnp.dot(att, vc, preferred_element_type=F32)
            oc += lax.dot_general(qc, state_t.astype(BF16), NT_DIMS, preferred_element_type=F32)
            inc_t = lax.dot_general(vc, kec, TN_DIMS, preferred_element_type=F32)
            state_t = state_t * decay[c * A_CHUNK:c * A_CHUNK + 1, cs] + inc_t
            chunks.append(oc)
        st_ref[h] = state_t
        o_h = jnp.concatenate(chunks, axis=0)
        ms = jnp.sum(o_h * o_h, axis=-1, keepdims=True) * (1.0 / A_DV)
        heads.append(o_h * lax.rsqrt(ms + EPS))
    o = jnp.concatenate(heads, axis=1) * gain_ref[...]
    g = x[:, 3 * hw:4 * hw].astype(F32)
    o_ref[...] = (o * (g * _sigmoid(g))).astype(o_ref.dtype)


def _gla(x3, wa2, ba2, gain):
    bsz, t, _ = x3.shape
    tb = min(SEQ_BLOCK, t)
    hw = A_HEADS * LANES
    return pl.pallas_call(
        functools.partial(_gla_kernel, tb=tb),
        grid=(bsz, t // tb),
        in_specs=[pl.BlockSpec((None, tb, GLA_W), lambda b, i: (b, i, 0)),
                  _resident((LANES, hw)), _resident((1, hw)), _resident((1, hw))],
        out_specs=pl.BlockSpec((None, tb, hw), lambda b, i: (b, i, 0)),
        out_shape=jax.ShapeDtypeStruct((bsz, t, hw), BF16),
        scratch_shapes=[pltpu.VMEM((A_HEADS, LANES, LANES), F32)],
        compiler_params=_params("parallel", "arbitrary"),
        name="gla",
    )(x3, wa2, ba2, gain)


def _rglru_kernel(d_ref, cw_ref, cb_ref, wri_ref, bri_ref, lam_ref, o_ref, xbuf, hc_ref, *, tb):
    halo = 8

    @pl.when(pl.program_id(1) == 0)
    def _():
        xbuf[0:halo, :] = jnp.zeros((halo, D_WIDTH), F32)
        hc_ref[...] = jnp.zeros_like(hc_ref)

    d = d_ref[...]
    y = d[:, 0:D_WIDTH].astype(F32)
    x = d[:, D_WIDTH:].astype(F32)
    xbuf[halo:halo + tb, :] = x
    xc = cb_ref[...]
    for j in range(D_CONV):
        xc = xc + cw_ref[j:j + 1, :] * xbuf[pl.ds(halo - (D_CONV - 1) + j, tb), :]
    xbuf[0:halo, :] = x[tb - halo:tb, :]

    ri = jnp.dot(xc.astype(BF16), wri_ref[...], preferred_element_type=F32) + bri_ref[...]
    r = _sigmoid(ri[:, 0:D_WIDTH])
    ig = _sigmoid(ri[:, D_WIDTH:])
    log_a = (-D_C) * r * _softplus(-lam_ref[...])
    a = jnp.exp(log_a)
    u = jnp.sqrt(-jnp.tanh(log_a) * (a * a + 1.0)) * (ig * xc)

    row = lax.broadcasted_iota(jnp.int32, (tb, D_WIDTH), 0)
    s = 1
    while s < tb:
        valid = row >= s
        u = jnp.where(valid, a * pltpu.roll(u, s, axis=0), 0.0) + u
        a = jnp.where(valid, a * pltpu.roll(a, s, axis=0), a)
        s *= 2
    hs = u + a * hc_ref[0:1, :]
    hc_ref[...] = jnp.broadcast_to(hs[tb - 1:tb, :], hc_ref.shape)
    gelu = 0.5 * y * (1.0 + jnp.tanh(math.sqrt(2.0 / math.pi) * (y + 0.044715 * (y * y * y))))
    o_ref[...] = (gelu * hs).astype(o_ref.dtype)


def _rglru(d3, conv_w, conv_b, w_ri, b_ri, lam):
    bsz, t, _ = d3.shape
    tb = min(SEQ_BLOCK, t)
    return pl.pallas_call(
        functools.partial(_rglru_kernel, tb=tb),
        grid=(bsz, t // tb),
        in_specs=[pl.BlockSpec((None, tb, LRU_W), lambda b, i: (b, i, 0)),
                  _resident((D_CONV, D_WIDTH)), _resident((1, D_WIDTH)),
                  _resident((D_WIDTH, 2 * D_WIDTH)), _resident((1, 2 * D_WIDTH)),
                  _resident((1, D_WIDTH))],
        out_specs=pl.BlockSpec((None, tb, D_WIDTH), lambda b, i: (b, i, 0)),
        out_shape=jax.ShapeDtypeStruct((bsz, t, D_WIDTH), BF16),
        scratch_shapes=[pltpu.VMEM((tb + 8, D_WIDTH), F32), pltpu.VMEM((8, D_WIDTH), F32)],
        compiler_params=_params("parallel", "arbitrary"),
        name="rglru",
    )(d3, conv_w, conv_b, w_ri, b_ri, lam)


def _bucket_edges():
    n = np.arange(REL_MAX_DIST + 1)
    max_exact = REL_BUCKETS // 2
    nf = np.maximum(n, 1).astype(np.float64)
    large = max_exact + (np.log(nf / max_exact) / math.log(REL_MAX_DIST / max_exact)
                         * (REL_BUCKETS - max_exact)).astype(np.int64)
    bucket = np.where(n < max_exact, n, np.minimum(large, REL_BUCKETS - 1))
    assert np.all(np.diff(bucket) >= 0)
    return [int(np.argmax(bucket >= b)) for b in range(REL_BUCKETS)]


def _tbias_kernel(tab_ref, o_ref, *, bs, edges):
    h = pl.program_id(0)
    d = pl.program_id(1)
    row = lax.broadcasted_iota(jnp.int32, (bs, bs), 0)
    col = lax.broadcasted_iota(jnp.int32, (bs, bs), 1)
    dist = d * bs + row - col
    val = jnp.full((bs, bs), tab_ref[0, h], F32)
    for b in range(1, REL_BUCKETS):
        val = jnp.where(dist >= edges[b], tab_ref[b, h], val)
    mult = jnp.zeros((bs, bs), jnp.int32)
    for window, dil in C_PATTERNS:
        hit = jnp.where((dist & (dil - 1)) == 0, 1, 0)
        mult = mult + jnp.where(dist <= window, hit, 0)
    logm = jnp.full((bs, bs), NEG, F32)
    for m in range(1, len(C_PATTERNS) + 1):
        logm = jnp.where(mult == m, math.log(m), logm)
    val = jnp.where(h >= B_HEADS, val + logm, val)
    o_ref[...] = jnp.where(dist >= 0, val, NEG)


def _tbias(rel_bias, t):
    bs = min(ATT_BLOCK, t)
    nd = t // bs
    for _, dil in C_PATTERNS:
        assert dil & (dil - 1) == 0
    return pl.pallas_call(
        functools.partial(_tbias_kernel, bs=bs, edges=_bucket_edges()),
        grid=(N_SOFTMAX_HEADS, nd),
        in_specs=[pl.BlockSpec(memory_space=pltpu.SMEM)],
        out_specs=pl.BlockSpec((None, None, bs, bs), lambda h, d: (h, d, 0, 0)),
        out_shape=jax.ShapeDtypeStruct((N_SOFTMAX_HEADS, nd, bs, bs), F32),
        compiler_params=_params("parallel", "parallel"),
        name="tbias",
    )(rel_bias)


def _attn_kernel(q_ref, k_ref, v_ref, tb_ref, *rest, bs, masked):
    if masked:
        madd_ref, o_ref, vp_ref, m_ref, acc_ref = rest
    else:
        o_ref, vp_ref, m_ref, acc_ref = rest
    i = pl.program_id(1)
    n_heads = vp_ref.shape[0]
    half = LANES // 2
    assert B_HEAD_DIM == half and C_HEAD_DIM == half

    @pl.when(i == 0)
    def _():
        low_t = lax.broadcasted_iota(jnp.int32, (v_ref.shape[0], LANES), 1) < half
        for hp in range(n_heads // 2):
            pair = v_ref[:, hp * LANES:(hp + 1) * LANES]
            vp_ref[2 * hp] = jnp.where(low_t, pair, 1.0).astype(BF16)
            vp_ref[2 * hp + 1] = jnp.where(low_t, 1.0, pair).astype(BF16)

    m_ref[...] = jnp.full(m_ref.shape, NEG, F32)
    acc_ref[...] = jnp.zeros(acc_ref.shape, F32)
    low = lax.broadcasted_iota(jnp.int32, (bs, LANES), 1) < half
    q_heads = []
    for h in range(n_heads):
        qp = q_ref[:, (h // 2) * LANES:(h // 2 + 1) * LANES] * (B_HEAD_DIM ** -0.5)
        q_heads.append(jnp.where(low if h % 2 == 0 else jnp.logical_not(low), qp, 0.0).astype(BF16))

    def body(c, _):
        start = pl.multiple_of(c * bs, bs)
        extra = madd_ref[c].astype(F32) if masked else None
        for h in range(n_heads):
            ks = k_ref[pl.ds(start, bs), (h // 2) * LANES:(h // 2 + 1) * LANES]
            s = lax.dot_general(q_heads[h], ks, NT_DIMS, preferred_element_type=F32)
            s = s + tb_ref[h, i - c]
            if masked:
                s = s + extra
            m_prev = m_ref[h]
            m_new = jnp.maximum(m_prev, jnp.max(s, axis=1, keepdims=True))
            alpha = jnp.exp(m_prev - m_new)
            p = jnp.exp(s - jnp.concatenate([m_new] * (bs // LANES), axis=1)).astype(BF16)
            acc_ref[h] = alpha * acc_ref[h] + jnp.dot(p, vp_ref[h, pl.ds(start, bs), :],
                                                      preferred_element_type=F32)
            m_ref[h] = m_new
        return 0

    lax.fori_loop(0, i + 1, body, 0)

    for hp in range(n_heads // 2):
        a0, a1 = acc_ref[2 * hp], acc_ref[2 * hp + 1]
        o0 = a0 / pltpu.roll(a0, half, axis=1)
        o1 = a1 / pltpu.roll(a1, half, axis=1)
        o_ref[:, hp * LANES:(hp + 1) * LANES] = jnp.where(low, o0, o1).astype(o_ref.dtype)


def _attention(qkv3, tbias, head_block, madd=None):
    bsz, t, width3 = qkv3.shape
    width = width3 // 3
    n_heads = width // B_HEAD_DIM
    bs = min(ATT_BLOCK, t)
    nd = t // bs
    in_specs = [pl.BlockSpec((None, bs, width), lambda b, i: (b, i, 0)),
                pl.BlockSpec((None, t, width), lambda b, i: (b, 0, 1)),
                pl.BlockSpec((None, t, width), lambda b, i: (b, 0, 2)),
                pl.BlockSpec((n_heads, nd, bs, bs), lambda b, i: (head_block, 0, 0, 0))]
    args = [qkv3, qkv3, qkv3, tbias]
    if madd is not None:
        in_specs.append(pl.BlockSpec((None, None, nd, bs, bs), lambda b, i: (b, i, 0, 0, 0)))
        args.append(madd)
    return pl.pallas_call(
        functools.partial(_attn_kernel, bs=bs, masked=madd is not None),
        grid=(bsz, nd),
        in_specs=in_specs,
        out_specs=pl.BlockSpec((None, bs, width), lambda b, i: (b, i, 0)),
        out_shape=jax.ShapeDtypeStruct((bsz, t, width), BF16),
        scratch_shapes=[pltpu.VMEM((n_heads, t, LANES), BF16),
                        pltpu.VMEM((n_heads, bs, LANES), F32),
                        pltpu.VMEM((n_heads, bs, LANES), F32)],
        compiler_params=_params("arbitrary", "arbitrary"),
        name="attn_masked" if madd is not None else "attn",
    )(*args)


def _dsa_select_kernel(iq_ref, ik_ref, iw_ref, madd_ref, key_ref, thr_ref, need_ref,
                       *, bs, nd, k_sel):
    pair = lambda i, c: i * (i + 1) // 2 + c
    lane = lax.broadcasted_iota(jnp.int32, (bs, LANES), 1)
    row = lax.broadcasted_iota(jnp.int32, (bs, bs), 0)
    col = lax.broadcasted_iota(jnp.int32, (bs, bs), 1)
    tiles = bs // LANES

    for i in range(nd):
        iq = iq_ref[i * bs:(i + 1) * bs, :]
        iw = iw_ref[i * bs:(i + 1) * bs, :].astype(F32) * (IDX_HEADS ** -0.5 * IDX_DIM ** -0.5)
        iq_heads = [jnp.where((lane // IDX_DIM) == hh, iq, 0.0).astype(BF16)
                    for hh in range(IDX_HEADS)]
        for c in range(i + 1):
            ikc = ik_ref[c * bs:(c + 1) * bs, :]
            sc = jnp.zeros((bs, bs), F32)
            for hh in range(IDX_HEADS):
                raw = lax.dot_general(iq_heads[hh], ikc, NT_DIMS, preferred_element_type=F32)
                sc = sc + iw[:, hh:hh + 1] * jnp.maximum(raw, 0.0)
            sc = jnp.where(sc == 0.0, 0.0, sc)
            bits = pltpu.bitcast(sc, jnp.int32)
            key = jnp.where(bits < 0, bits ^ 0x7FFFFFFF, bits)
            if c == i:
                key = jnp.where(col <= row, key, INT_MIN)
            key_ref[pair(i, c)] = key

    thr_ref[...] = jnp.full(thr_ref.shape, INT_MIN, jnp.int32)
    kf = float(k_sel)
    n_bits = 32

    def bisect(it, _):
        searching = it < n_bits
        step = jnp.where(searching, lax.shift_left(jnp.int32(1), jnp.maximum(n_bits - 1 - it, 0)), 1)
        for i in range(nd):
            thr = thr_ref[i]
            cand = thr + step
            acc = jnp.zeros((bs, LANES), F32)
            for c in range(i + 1):
                kk = key_ref[pair(i, c)]
                for j in range(tiles):
                    acc = acc + jnp.where(kk[:, j * LANES:(j + 1) * LANES] >= cand, 1.0, 0.0)
            cnt = jnp.sum(acc, axis=1, keepdims=True)
            thr_ref[i] = jnp.where(jnp.logical_and(searching, cnt >= kf), cand, thr)
            need_ref[i] = jnp.broadcast_to(kf - cnt, (bs, LANES))
        return 0

    lax.fori_loop(0, n_bits + 1, bisect, 0)

    prefix = jnp.where(row <= col, 1.0, 0.0).astype(BF16)
    for i in range(nd):
        thr2 = jnp.concatenate([thr_ref[i]] * tiles, axis=1)
        need = need_ref[i][:, 0:1]
        seen = jnp.zeros((bs, 1), F32)
        for c in range(i + 1):
            kk = key_ref[pair(i, c)]
            tie = kk == thr2
            rank = seen + jnp.dot(jnp.where(tie, 1.0, 0.0).astype(BF16), prefix,
                                  preferred_element_type=F32)
            tie_add = jnp.where(tie, jnp.where(rank <= need, 0.0, NEG), NEG)
            madd_ref[i, c] = jnp.where(kk > thr2, 0.0, tie_add).astype(madd_ref.dtype)
            seen = rank[:, bs - 1:bs]
        for c in range(i + 1, nd):
            madd_ref[i, c] = jnp.full((bs, bs), NEG, madd_ref.dtype)


def _dsa_select(idx3):
    bsz, t, _ = idx3.shape
    bs = min(ATT_BLOCK, t)
    nd = t // bs
    k_sel = min(B_TOPK_MAX, t // 4)
    col_block = lambda j: pl.BlockSpec((None, t, LANES), lambda b: (b, 0, j))
    return pl.pallas_call(
        functools.partial(_dsa_select_kernel, bs=bs, nd=nd, k_sel=k_sel),
        grid=(bsz,),
        in_specs=[col_block(0), col_block(1), col_block(2)],
        out_specs=pl.BlockSpec((None, nd, nd, bs, bs), lambda b: (b, 0, 0, 0, 0)),
        out_shape=jax.ShapeDtypeStruct((bsz, nd, nd, bs, bs), BF16),
        scratch_shapes=[pltpu.VMEM((nd * (nd + 1) // 2, bs, bs), jnp.int32),
                        pltpu.VMEM((nd, bs, LANES), jnp.int32),
                        pltpu.VMEM((nd, bs, LANES), F32)],
        compiler_params=_params("parallel"),
        name="dsa_select",
    )(idx3, idx3, idx3)


def _dilated(qkv3, tbias):
    return _attention(qkv3, tbias, 1)


def _dsa(qkv3, idx3, tbias):
    return _attention(qkv3, tbias, 0, _dsa_select(idx3))


def _route(logits):
    lane = lax.broadcasted_iota(jnp.int32, logits.shape, 1).astype(F32)
    far = float(LANES)

    def first_argmax(vals):
        top = jnp.max(vals, axis=1, keepdims=True)
        return top, jnp.min(jnp.where(vals == top, lane, far), axis=1, keepdims=True)

    is_group = lane < MOE_GROUPS
    g_top, g_sel = first_argmax(jnp.where(is_group, logits, NEG))
    g_w = 1.0 / jnp.sum(jnp.where(is_group, jnp.exp(logits - g_top), 0.0), axis=1, keepdims=True)
    lo = MOE_GROUPS + MOE_EXPERTS_PER_GROUP * g_sel
    e_logits = jnp.where(lane >= lo, jnp.where(lane < lo + MOE_EXPERTS_PER_GROUP, logits, NEG), NEG)
    v1, i1 = first_argmax(e_logits)
    v2, i2 = first_argmax(jnp.where(lane == i1, NEG, e_logits))
    e2 = jnp.exp(v2 - v1)
    w1 = 1.0 / (1.0 + e2)
    return jnp.where(lane == i1, w1 * g_w, jnp.where(lane == i2, e2 * w1 * g_w, 0.0))


def _merge_kernel(h_ref, g1_ref, oa_ref, ob_ref, oc_ref, od_ref, wg_ref, bg_ref,
                  wba_ref, wbb_ref, wbc_ref, wbd_ref, wo_ref, g2_ref, wr_ref, br_ref,
                  h1_ref, xn2_ref, gates_ref):
    h = h_ref[...]
    xn = _rms(h, g1_ref[...]).astype(BF16)
    merged = None
    for gi, (o_ref, wb_ref) in enumerate(((oa_ref, wba_ref), (ob_ref, wbb_ref),
                                          (oc_ref, wbc_ref), (od_ref, wbd_ref))):
        gate = _sigmoid(jnp.dot(xn, wg_ref[gi], preferred_element_type=F32) + bg_ref[gi:gi + 1, :])
        term = gate * jnp.dot(o_ref[...], wb_ref[...], preferred_element_type=F32)
        merged = term if merged is None else merged + term
    h1 = h + jnp.dot(merged.astype(BF16), wo_ref[...], preferred_element_type=F32)
    h1_ref[...] = h1
    xn2 = _rms(h1, g2_ref[...])
    xn2_ref[...] = xn2.astype(BF16)
    logits = jnp.dot(xn2, wr_ref[...], precision=HIGHEST, preferred_element_type=F32) + br_ref[...]
    gates_ref[...] = _route(logits)


def _merge(h, g1, oa, ob, oc, od, wg, bg, wba, wbb, wbc, wbd, wo, g2, wr, br):
    n = h.shape[0]
    tm = min(PROJ_ROWS, n)
    rows = lambda w: pl.BlockSpec((tm, w), lambda i: (i, 0))
    return pl.pallas_call(
        _merge_kernel,
        grid=(n // tm,),
        in_specs=[rows(D_MODEL), _resident(g1.shape),
                  rows(oa.shape[1]), rows(ob.shape[1]), rows(oc.shape[1]), rows(od.shape[1]),
                  _resident(wg.shape), _resident(bg.shape),
                  _resident(wba.shape), _resident(wbb.shape), _resident(wbc.shape),
                  _resident(wbd.shape), _resident(wo.shape), _resident(g2.shape),
                  _resident(wr.shape), _resident(br.shape)],
        out_specs=[rows(D_MODEL), rows(D_MODEL), rows(LANES)],
        out_shape=[jax.ShapeDtypeStruct((n, D_MODEL), F32),
                   jax.ShapeDtypeStruct((n, D_MODEL), BF16),
                   jax.ShapeDtypeStruct((n, LANES), F32)],
        compiler_params=_params("parallel"),
        name="merge",
    )(h, g1, oa, ob, oc, od, wg, bg, wba, wbb, wbc, wbd, wo, g2, wr, br)


def _moe_kernel(x_ref, gates_ref, h1_ref, wg_ref, wu_ref, wd_ref, gf_ref, o_ref, acc_ref,
                *, final_norm):
    e = pl.program_id(1)

    @pl.when(e == 0)
    def _():
        acc_ref[...] = h1_ref[...]

    x = x_ref[...]
    hg = jnp.dot(x, wg_ref[...], preferred_element_type=F32)
    hu = jnp.dot(x, wu_ref[...], preferred_element_type=F32)
    hid = (hg * _sigmoid(hg) * hu).astype(BF16)
    y = jnp.dot(hid, wd_ref[...], preferred_element_type=F32)
    lane = lax.broadcasted_iota(jnp.int32, gates_ref.shape, 1)
    gate = jnp.sum(jnp.where(lane == e + MOE_GROUPS, gates_ref[...], 0.0), axis=1, keepdims=True)
    acc_ref[...] += gate * y

    @pl.when(e == N_EXPERTS - 1)
    def _():
        out = acc_ref[...]
        o_ref[...] = _rms(out, gf_ref[...]) if final_norm else out


def _moe(xn2, gates, h1, wg, wu, wd, gf, final_norm):
    n = xn2.shape[0]
    tm = min(MOE_ROWS, n)
    rows = lambda w: pl.BlockSpec((tm, w), lambda i, e: (i, 0))
    return pl.pallas_call(
        functools.partial(_moe_kernel, final_norm=final_norm),
        grid=(n // tm, N_EXPERTS),
        in_specs=[rows(D_MODEL), rows(LANES), rows(D_MODEL),
                  pl.BlockSpec((None, D_MODEL, MOE_HIDDEN), lambda i, e: (e, 0, 0)),
                  pl.BlockSpec((None, D_MODEL, MOE_HIDDEN), lambda i, e: (e, 0, 0)),
                  pl.BlockSpec((None, MOE_HIDDEN, D_MODEL), lambda i, e: (e, 0, 0)),
                  pl.BlockSpec((1, D_MODEL), lambda i, e: (0, 0))],
        out_specs=rows(D_MODEL),
        out_shape=jax.ShapeDtypeStruct((n, D_MODEL), F32),
        scratch_shapes=[pltpu.VMEM((tm, D_MODEL), F32)],
        compiler_params=_params("parallel", "arbitrary"),
        name="moe",
    )(xn2, gates, h1, wg, wu, wd, gf)


def _inproj_columns():
    offs = np.concatenate([[0], np.cumsum(IN_SPLITS)])
    (aq, ak, av, ag, alr, bq, bk, bv, iq, ik, iw, cq, ck, cv, dy, dx) = offs[:-1]
    cols = []

    def per_head(base, width):
        for h in range(A_HEADS):
            cols.extend(list(range(base + h * width, base + (h + 1) * width)) + [-1] * (LANES - width))

    per_head(aq, A_DK)
    per_head(ak, A_DK)
    per_head(av, A_DV)
    per_head(ag, A_DV)
    cols.extend(list(range(alr, alr + A_LOWRANK)) + [-1] * (LANES - A_LOWRANK))
    cols.extend(range(bq, bq + ATT_W))
    cols.extend(range(iq, iq + IDX_HEADS * IDX_DIM))
    cols.extend(list(range(ik, ik + IDX_DIM)) * IDX_HEADS)
    cols.extend(list(range(iw, iw + IDX_HEADS)) + [-1] * (LANES - IDX_HEADS))
    cols.extend(range(cq, cq + ATT_W))
    cols.extend(range(dy, dy + LRU_W))
    cols = np.asarray(cols, np.int32)
    assert cols.shape[0] == sum(GROUP_WIDTHS)
    return cols


def _pad_heads(a, width, axis):
    a = jnp.moveaxis(a, axis, -1)
    a = a.reshape(a.shape[:-1] + (A_HEADS, width))
    a = jnp.pad(a, [(0, 0)] * (a.ndim - 1) + [(0, LANES - width)])
    return jnp.moveaxis(a.reshape(a.shape[:-2] + (A_HEADS * LANES,)), -1, axis)


def _block_diag(w):
    nb, bw, _ = w.shape
    eye = jnp.eye(nb, dtype=w.dtype)
    return jnp.einsum('ncd,nm->ncmd', w, eye).reshape(nb * bw, nb * bw)


def kernel(x, w_in, a_w2, a_b2, a_gain, conv_w, conv_b, lru_wr, lru_br, lru_wi, lru_bi, lru_lambda, w_gate, b_gate, w_branch, w_out, rel_bias, norm1, norm2, norm_f, moe_wrg, moe_brg, moe_wre, moe_bre, moe_wg, moe_wu, moe_wd):
    bsz, t, _ = x.shape
    n = bsz * t
    depth = w_in.shape[0]
    cols = _inproj_columns()
    tbias = _tbias(rel_bias, t)
    h = x.reshape(n, D_MODEL)
    for l in range(depth):
        w_all = jnp.where(cols[None, :] >= 0, w_in[l][:, np.maximum(cols, 0)], 0.0).astype(BF16)
        gla_in, dsa_in, idx_in, dil_in, lru_in = _inproj(h, norm1[l][None, :], w_all)

        wa2 = jnp.pad(_pad_heads(a_w2[l], A_DK, 1), ((0, LANES - A_LOWRANK), (0, 0)))
        o_a = _gla(gla_in.reshape(bsz, t, GLA_W), wa2, _pad_heads(a_b2[l], A_DK, 0)[None, :],
                   _pad_heads(a_gain[l], A_DV, 0)[None, :])
        o_b = _dsa(dsa_in.reshape(bsz, t, ATT_W), idx_in.reshape(bsz, t, IDX_W), tbias)
        o_c = _dilated(dil_in.reshape(bsz, t, ATT_W), tbias)
        w_ri = jnp.concatenate([_block_diag(lru_wr[l]), _block_diag(lru_wi[l])], axis=1).astype(BF16)
        b_ri = jnp.concatenate([lru_br[l], lru_bi[l]])[None, :]
        o_d = _rglru(lru_in.reshape(bsz, t, LRU_W), conv_w[l], conv_b[l][None, :], w_ri, b_ri,
                     lru_lambda[l][None, :])

        w_router = jnp.pad(jnp.concatenate([moe_wrg[l], moe_wre[l]], axis=1),
                           ((0, 0), (0, LANES - MOE_GROUPS - N_EXPERTS)))
        b_router = jnp.pad(jnp.concatenate([moe_brg[l], moe_bre[l]]),
                           (0, LANES - MOE_GROUPS - N_EXPERTS))[None, :]
        h1, xn2, gates = _merge(
            h, norm1[l][None, :], o_a.reshape(n, -1), o_b.reshape(n, -1), o_c.reshape(n, -1),
            o_d.reshape(n, -1), w_gate[l].astype(BF16), b_gate[l],
            _pad_heads(w_branch[l, 0], A_DV, 0).astype(BF16), w_branch[l, 1].astype(BF16),
            w_branch[l, 2].astype(BF16), w_branch[l, 3].astype(BF16), w_out[l].astype(BF16),
            norm2[l][None, :], w_router, b_router)
        h = _moe(xn2, gates, h1, moe_wg[l].astype(BF16), moe_wu[l].astype(BF16),
                 moe_wd[l].astype(BF16), norm_f[None, :], final_norm=(l == depth - 1))
    return h.reshape(bsz, t, D_MODEL)
```

```python
import functools
import math

import numpy as np
import jax
import jax.numpy as jnp
from jax import lax
from jax.experimental import pallas as pl
from jax.experimental.pallas import tpu as pltpu

F32 = jnp.float32
BF16 = jnp.bfloat16
HIGHEST = lax.Precision.HIGHEST

D_MODEL = 1024
EPS = 1e-6
A_HEADS, A_DK, A_DV, A_LOWRANK, A_GATE_TAU, A_CHUNK = 4, 48, 96, 16, 16.0, 64
B_HEADS, B_HEAD_DIM, IDX_HEADS, IDX_DIM, B_TOPK_MAX = 6, 64, 4, 32, 256
C_HEADS, C_HEAD_DIM = 6, 64
C_PATTERNS = ((128, 1), (512, 4), (2048, 16))
D_WIDTH, D_BLOCKS, D_CONV, D_C = 384, 8, 4, 8.0
REL_BUCKETS, REL_MAX_DIST = 32, 2048
N_SOFTMAX_HEADS = B_HEADS + C_HEADS
MOE_GROUPS, MOE_EXPERTS_PER_GROUP, MOE_HIDDEN = 4, 4, 512
N_EXPERTS = MOE_GROUPS * MOE_EXPERTS_PER_GROUP
IN_SPLITS = (A_HEADS * A_DK, A_HEADS * A_DK, A_HEADS * A_DV, A_HEADS * A_DV, A_LOWRANK,
             B_HEADS * B_HEAD_DIM, B_HEADS * B_HEAD_DIM, B_HEADS * B_HEAD_DIM,
             IDX_HEADS * IDX_DIM, IDX_DIM, IDX_HEADS,
             C_HEADS * C_HEAD_DIM, C_HEADS * C_HEAD_DIM, C_HEADS * C_HEAD_DIM,
             D_WIDTH, D_WIDTH)

LANES = 128
VMEM_LIMIT = 56 * 1024 * 1024
NEG = -1e30
INT_MIN = -2 ** 31
LOG2E = math.log2(math.e)

ATT_BLOCK = 256
SEQ_BLOCK = 256
PROJ_ROWS = 512
MOE_ROWS = 1024

GLA_W = 4 * A_HEADS * LANES + LANES
ATT_W = 3 * B_HEADS * B_HEAD_DIM
IDX_W = 3 * LANES
LRU_W = 2 * D_WIDTH
GROUP_WIDTHS = (GLA_W, ATT_W, IDX_W, ATT_W, LRU_W)
NT_DIMS = (((1,), (1,)), ((), ()))
TN_DIMS = (((0,), (0,)), ((), ()))


def _params(*sem):
    return pltpu.CompilerParams(dimension_semantics=sem, vmem_limit_bytes=VMEM_LIMIT)


def _resident(shape):
    zeros = (0,) * len(shape)
    return pl.BlockSpec(shape, lambda *_: zeros, pipeline_mode=pl.Buffered(1))


def _rms(x, gain):
    return x * lax.rsqrt(jnp.mean(x * x, axis=-1, keepdims=True) + EPS) * gain


def _sigmoid(x):
    return 1.0 / (1.0 + jnp.exp(-x))


def _softplus(x):
    return jnp.maximum(x, 0.0) + jnp.log1p(jnp.exp(-jnp.abs(x)))


def _inproj_kernel(x_ref, g_ref, w_ref, *out_refs):
    xn = _rms(x_ref[...], g_ref[...]).astype(BF16)
    off = 0
    for o_ref in out_refs:
        width = o_ref.shape[-1]
        o_ref[...] = jnp.dot(xn, w_ref[:, off:off + width],
                             preferred_element_type=F32).astype(o_ref.dtype)
        off += width


def _inproj(h, gain, w_all):
    n = h.shape[0]
    tm = min(PROJ_ROWS, n)
    total = sum(GROUP_WIDTHS)
    return pl.pallas_call(
        _inproj_kernel,
        grid=(n // tm,),
        in_specs=[pl.BlockSpec((tm, D_MODEL), lambda i: (i, 0)),
                  _resident((1, D_MODEL)),
                  _resident((D_MODEL, total))],
        out_specs=[pl.BlockSpec((tm, w), lambda i: (i, 0)) for w in GROUP_WIDTHS],
        out_shape=[jax.ShapeDtypeStruct((n, w), BF16) for w in GROUP_WIDTHS],
        compiler_params=_params("parallel"),
        name="inproj",
    )(h, gain, w_all)


def _gla_kernel(x_ref, wa2_ref, ba2_ref, gain_ref, o_ref, st_ref, *, tb):
    @pl.when(pl.program_id(1) == 0)
    def _():
        st_ref[...] = jnp.zeros_like(st_ref)

    hw = A_HEADS * LANES
    x = x_ref[...]
    z = jnp.dot(x[:, 4 * hw:].astype(F32), wa2_ref[...], precision=HIGHEST,
                preferred_element_type=F32) + ba2_ref[...]
    log_a = -_softplus(-z) * (1.0 / A_GATE_TAU)

    r64 = lax.broadcasted_iota(jnp.int32, (A_CHUNK, A_CHUNK), 0)
    c64 = lax.broadcasted_iota(jnp.int32, (A_CHUNK, A_CHUNK), 1)
    causal = c64 <= r64
    tri = jnp.where(causal, 1.0, 0.0).astype(BF16)
    la_hi = log_a.astype(BF16)
    la_lo = (log_a - la_hi.astype(F32)).astype(BF16)
    b_parts, b_last_parts = [], []
    for c in range(tb // A_CHUNK):
        rs = slice(c * A_CHUNK, (c + 1) * A_CHUNK)
        bc = (jnp.dot(tri, la_hi[rs], preferred_element_type=F32)
              + jnp.dot(tri, la_lo[rs], preferred_element_type=F32))
        b_parts.append(bc)
        b_last_parts.append(jnp.broadcast_to(bc[A_CHUNK - 1:A_CHUNK, :], bc.shape))
    b = jnp.concatenate(b_parts, axis=0)
    b_last = jnp.concatenate(b_last_parts, axis=0)

    q = x[:, 0:hw].astype(F32) * (A_DK ** -0.5)
    k = x[:, hw:2 * hw].astype(F32)
    v = x[:, 2 * hw:3 * hw]
    q_dec = (q * jnp.exp(b)).astype(BF16)
    k_dec = (k * jnp.exp(-b)).astype(BF16)
    k_end = (k * jnp.exp(b_last - b)).astype(BF16)
    decay = jnp.exp(b_last)

    heads = []
    for h in range(A_HEADS):
        cs = slice(h * LANES, (h + 1) * LANES)
        state_t = st_ref[h]
        chunks = []
        for c in range(tb // A_CHUNK):
            rs = slice(c * A_CHUNK, (c + 1) * A_CHUNK)
            qc, kc, kec, vc = q_dec[rs, cs], k_dec[rs, cs], k_end[rs, cs], v[rs, cs]
            att = lax.dot_general(qc, kc, NT_DIMS, preferred_element_type=F32)
            att = jnp.where(causal, att, 0.0).astype(BF16)
            oc = jnp.dot(att, vc, preferred_element_type=F32)
            oc += lax.dot_general(qc, state_t.astype(BF16), NT_DIMS, preferred_element_type=F32)
            inc_t = lax.dot_general(vc, kec, TN_DIMS, preferred_element_type=F32)
            state_t = state_t * decay[c * A_CHUNK:c * A_CHUNK + 1, cs] + inc_t
            chunks.append(oc)
        st_ref[h] = state_t
        o_h = jnp.concatenate(chunks, axis=0)
        ms = jnp.sum(o_h * o_h, axis=-1, keepdims=True) * (1.0 / A_DV)
        heads.append(o_h * lax.rsqrt(ms + EPS))
    o = jnp.concatenate(heads, axis=1) * gain_ref[...]
    g = x[:, 3 * hw:4 * hw].astype(F32)
    o_ref[...] = (o * (g * _sigmoid(g))).astype(o_ref.dtype)


def _gla(x3, wa2, ba2, gain):
    bsz, t, _ = x3.shape
    tb = min(SEQ_BLOCK, t)
    hw = A_HEADS * LANES
    return pl.pallas_call(
        functools.partial(_gla_kernel, tb=tb),
        grid=(bsz, t // tb),
        in_specs=[pl.BlockSpec((None, tb, GLA_W), lambda b, i: (b, i, 0)),
                  _resident((LANES, hw)), _resident((1, hw)), _resident((1, hw))],
        out_specs=pl.BlockSpec((None, tb, hw), lambda b, i: (b, i, 0)),
        out_shape=jax.ShapeDtypeStruct((bsz, t, hw), BF16),
        scratch_shapes=[pltpu.VMEM((A_HEADS, LANES, LANES), F32)],
        compiler_params=_params("parallel", "arbitrary"),
        name="gla",
    )(x3, wa2, ba2, gain)


def _rglru_kernel(d_ref, cw_ref, cb_ref, wri_ref, bri_ref, lam_ref, o_ref, xbuf, hc_ref, *, tb):
    halo = 8

    @pl.when(pl.program_id(1) == 0)
    def _():
        xbuf[0:halo, :] = jnp.zeros((halo, D_WIDTH), F32)
        hc_ref[...] = jnp.zeros_like(hc_ref)

    d = d_ref[...]
    y = d[:, 0:D_WIDTH].astype(F32)
    x = d[:, D_WIDTH:].astype(F32)
    xbuf[halo:halo + tb, :] = x
    xc = cb_ref[...]
    for j in range(D_CONV):
        xc = xc + cw_ref[j:j + 1, :] * xbuf[pl.ds(halo - (D_CONV - 1) + j, tb), :]
    xbuf[0:halo, :] = x[tb - halo:tb, :]

    ri = jnp.dot(xc.astype(BF16), wri_ref[...], preferred_element_type=F32) + bri_ref[...]
    r = _sigmoid(ri[:, 0:D_WIDTH])
    ig = _sigmoid(ri[:, D_WIDTH:])
    log_a = (-D_C) * r * _softplus(-lam_ref[...])
    a = jnp.exp(log_a)
    u = jnp.sqrt(-jnp.tanh(log_a) * (a * a + 1.0)) * (ig * xc)

    row = lax.broadcasted_iota(jnp.int32, (tb, D_WIDTH), 0)
    s = 1
    while s < tb:
        valid = row >= s
        u = jnp.where(valid, a * pltpu.roll(u, s, axis=0), 0.0) + u
        a = jnp.where(valid, a * pltpu.roll(a, s, axis=0), a)
        s *= 2
    hs = u + a * hc_ref[0:1, :]
    hc_ref[...] = jnp.broadcast_to(hs[tb - 1:tb, :], hc_ref.shape)
    gelu = 0.5 * y * (1.0 + jnp.tanh(math.sqrt(2.0 / math.pi) * (y + 0.044715 * (y * y * y))))
    o_ref[...] = (gelu * hs).astype(o_ref.dtype)


def _rglru(d3, conv_w, conv_b, w_ri, b_ri, lam):
    bsz, t, _ = d3.shape
    tb = min(SEQ_BLOCK, t)
    return pl.pallas_call(
        functools.partial(_rglru_kernel, tb=tb),
        grid=(bsz, t // tb),
        in_specs=[pl.BlockSpec((None, tb, LRU_W), lambda b, i: (b, i, 0)),
                  _resident((D_CONV, D_WIDTH)), _resident((1, D_WIDTH)),
                  _resident((D_WIDTH, 2 * D_WIDTH)), _resident((1, 2 * D_WIDTH)),
                  _resident((1, D_WIDTH))],
        out_specs=pl.BlockSpec((None, tb, D_WIDTH), lambda b, i: (b, i, 0)),
        out_shape=jax.ShapeDtypeStruct((bsz, t, D_WIDTH), BF16),
        scratch_shapes=[pltpu.VMEM((tb + 8, D_WIDTH), F32), pltpu.VMEM((8, D_WIDTH), F32)],
        compiler_params=_params("parallel", "arbitrary"),
        name="rglru",
    )(d3, conv_w, conv_b, w_ri, b_ri, lam)


def _bucket_edges():
    n = np.arange(REL_MAX_DIST + 1)
    max_exact = REL_BUCKETS // 2
    nf = np.maximum(n, 1).astype(np.float64)
    large = max_exact + (np.log(nf / max_exact) / math.log(REL_MAX_DIST / max_exact)
                         * (REL_BUCKETS - max_exact)).astype(np.int64)
    bucket = np.where(n < max_exact, n, np.minimum(large, REL_BUCKETS - 1))
    assert np.all(np.diff(bucket) >= 0)
    return [int(np.argmax(bucket >= b)) for b in range(REL_BUCKETS)]


def _tbias_kernel(tab_ref, o_ref, *, bs, edges):
    h = pl.program_id(0)
    d = pl.program_id(1)
    row = lax.broadcasted_iota(jnp.int32, (bs, bs), 0)
    col = lax.broadcasted_iota(jnp.int32, (bs, bs), 1)
    dist = d * bs + row - col
    val = jnp.full((bs, bs), tab_ref[0, h], F32)
    for b in range(1, REL_BUCKETS):
        val = jnp.where(dist >= edges[b], tab_ref[b, h], val)
    mult = jnp.zeros((bs, bs), jnp.int32)
    for window, dil in C_PATTERNS:
        hit = jnp.where((dist & (dil - 1)) == 0, 1, 0)
        mult = mult + jnp.where(dist <= window, hit, 0)
    logm = jnp.full((bs, bs), NEG, F32)
    for m in range(1, len(C_PATTERNS) + 1):
        logm = jnp.where(mult == m, math.log(m), logm)
    val = jnp.where(h >= B_HEADS, val + logm, val)
    o_ref[...] = jnp.where(dist >= 0, val * LOG2E, NEG)


def _tbias(rel_bias, t):
    bs = min(ATT_BLOCK, t)
    nd = t // bs
    for _, dil in C_PATTERNS:
        assert dil & (dil - 1) == 0
    return pl.pallas_call(
        functools.partial(_tbias_kernel, bs=bs, edges=_bucket_edges()),
        grid=(N_SOFTMAX_HEADS, nd),
        in_specs=[pl.BlockSpec(memory_space=pltpu.SMEM)],
        out_specs=pl.BlockSpec((None, None, bs, bs), lambda h, d: (h, d, 0, 0)),
        out_shape=jax.ShapeDtypeStruct((N_SOFTMAX_HEADS, nd, bs, bs), F32),
        compiler_params=_params("parallel", "parallel"),
        name="tbias",
    )(rel_bias)


def _attn_kernel(q_ref, k_ref, v_ref, tb_ref, *rest, bs, masked):
    if masked:
        madd_ref, o_ref, vp_ref, m_ref, acc_ref = rest
    else:
        o_ref, vp_ref, m_ref, acc_ref = rest
    i = pl.program_id(1)
    n_heads = vp_ref.shape[0]
    half = LANES // 2
    assert B_HEAD_DIM == half and C_HEAD_DIM == half

    @pl.when(i == 0)
    def _():
        low_t = lax.broadcasted_iota(jnp.int32, (v_ref.shape[0], LANES), 1) < half
        for hp in range(n_heads // 2):
            pair = v_ref[:, hp * LANES:(hp + 1) * LANES]
            vp_ref[2 * hp] = jnp.where(low_t, pair, 1.0).astype(BF16)
            vp_ref[2 * hp + 1] = jnp.where(low_t, 1.0, pair).astype(BF16)

    m_ref[...] = jnp.full(m_ref.shape, NEG, F32)
    acc_ref[...] = jnp.zeros(acc_ref.shape, F32)
    low = lax.broadcasted_iota(jnp.int32, (bs, LANES), 1) < half
    q_heads = []
    for h in range(n_heads):
        qp = q_ref[:, (h // 2) * LANES:(h // 2 + 1) * LANES].astype(F32) * (B_HEAD_DIM ** -0.5 * LOG2E)
        q_heads.append(jnp.where(low if h % 2 == 0 else jnp.logical_not(low), qp, 0.0).astype(BF16))

    def body(c, _):
        start = pl.multiple_of(c * bs, bs)
        extra = madd_ref[c].astype(F32) if masked else None
        for h in range(n_heads):
            ks = k_ref[pl.ds(start, bs), (h // 2) * LANES:(h // 2 + 1) * LANES]
            s = lax.dot_general(q_heads[h], ks, NT_DIMS, preferred_element_type=F32)
            s = s + tb_ref[h, i - c]
            if masked:
                s = s + extra
            m_prev = m_ref[h]
            m_new = jnp.maximum(m_prev, jnp.max(s, axis=1, keepdims=True))
            alpha = jnp.exp2(m_prev - m_new)
            p = jnp.exp2((s - jnp.concatenate([m_new] * (bs // LANES), axis=1)).astype(BF16))
            acc_ref[h] = alpha * acc_ref[h] + jnp.dot(p, vp_ref[h, pl.ds(start, bs), :],
                                                      preferred_element_type=F32)
            m_ref[h] = m_new
        return 0

    lax.fori_loop(0, i + 1, body, 0)

    for hp in range(n_heads // 2):
        a0, a1 = acc_ref[2 * hp], acc_ref[2 * hp + 1]
        o0 = a0 / pltpu.roll(a0, half, axis=1)
        o1 = a1 / pltpu.roll(a1, half, axis=1)
        o_ref[:, hp * LANES:(hp + 1) * LANES] = jnp.where(low, o0, o1).astype(o_ref.dtype)


def _attention(qkv3, tbias, head_block, madd=None):
    bsz, t, width3 = qkv3.shape
    width = width3 // 3
    n_heads = width // B_HEAD_DIM
    bs = min(ATT_BLOCK, t)
    nd = t // bs
    in_specs = [pl.BlockSpec((None, bs, width), lambda b, i: (b, i, 0)),
                pl.BlockSpec((None, t, width), lambda b, i: (b, 0, 1)),
                pl.BlockSpec((None, t, width), lambda b, i: (b, 0, 2)),
                pl.BlockSpec((n_heads, nd, bs, bs), lambda b, i: (head_block, 0, 0, 0))]
    args = [qkv3, qkv3, qkv3, tbias]
    if madd is not None:
        in_specs.append(pl.BlockSpec((None, None, nd, bs, bs), lambda b, i: (b, i, 0, 0, 0)))
        args.append(madd)
    return pl.pallas_call(
        functools.partial(_attn_kernel, bs=bs, masked=madd is not None),
        grid=(bsz, nd),
        in_specs=in_specs,
        out_specs=pl.BlockSpec((None, bs, width), lambda b, i: (b, i, 0)),
        out_shape=jax.ShapeDtypeStruct((bsz, t, width), BF16),
        scratch_shapes=[pltpu.VMEM((n_heads, t, LANES), BF16),
                        pltpu.VMEM((n_heads, bs, LANES), F32),
                        pltpu.VMEM((n_heads, bs, LANES), F32)],
        compiler_params=_params("arbitrary", "arbitrary"),
        name="attn_masked" if madd is not None else "attn",
    )(*args)


def _dsa_select_kernel(iq_ref, ik_ref, iw_ref, madd_ref, key_ref, thr_ref, need_ref,
                       *, bs, nd, k_sel):
    pair = lambda i, c: i * (i + 1) // 2 + c
    lane = lax.broadcasted_iota(jnp.int32, (bs, LANES), 1)
    row = lax.broadcasted_iota(jnp.int32, (bs, bs), 0)
    col = lax.broadcasted_iota(jnp.int32, (bs, bs), 1)
    tiles = bs // LANES

    for i in range(nd):
        iq = iq_ref[i * bs:(i + 1) * bs, :]
        iw = iw_ref[i * bs:(i + 1) * bs, :].astype(F32) * (IDX_HEADS ** -0.5 * IDX_DIM ** -0.5)
        iq_heads = [jnp.where((lane // IDX_DIM) == hh, iq, 0.0).astype(BF16)
                    for hh in range(IDX_HEADS)]
        for c in range(i + 1):
            ikc = ik_ref[c * bs:(c + 1) * bs, :]
            sc = jnp.zeros((bs, bs), F32)
            for hh in range(IDX_HEADS):
                raw = lax.dot_general(iq_heads[hh], ikc, NT_DIMS, preferred_element_type=F32)
                sc = sc + iw[:, hh:hh + 1] * jnp.maximum(raw, 0.0)
            sc = jnp.where(sc == 0.0, 0.0, sc)
            bits = pltpu.bitcast(sc, jnp.int32)
            key = jnp.where(bits < 0, bits ^ 0x7FFFFFFF, bits)
            if c == i:
                key = jnp.where(col <= row, key, INT_MIN)
            key_ref[pair(i, c)] = key

    thr_ref[...] = jnp.full(thr_ref.shape, INT_MIN, jnp.int32)
    kf = float(k_sel)
    n_bits = 32

    def bisect(it, _):
        searching = it < n_bits
        step = jnp.where(searching, lax.shift_left(jnp.int32(1), jnp.maximum(n_bits - 1 - it, 0)), 1)
        for i in range(nd):
            thr = thr_ref[i]
            cand = thr + step
            acc = jnp.zeros((bs, LANES), F32)
            for c in range(i + 1):
                kk = key_ref[pair(i, c)]
                for j in range(tiles):
                    acc = acc + jnp.where(kk[:, j * LANES:(j + 1) * LANES] >= cand, 1.0, 0.0)
            cnt = jnp.sum(acc, axis=1, keepdims=True)
            thr_ref[i] = jnp.where(jnp.logical_and(searching, cnt >= kf), cand, thr)
            need_ref[i] = jnp.broadcast_to(kf - cnt, (bs, LANES))
        return 0

    lax.fori_loop(0, n_bits + 1, bisect, 0)

    prefix = jnp.where(row <= col, 1.0, 0.0).astype(BF16)
    for i in range(nd):
        thr2 = jnp.concatenate([thr_ref[i]] * tiles, axis=1)
        need = need_ref[i][:, 0:1]
        seen = jnp.zeros((bs, 1), F32)
        for c in range(i + 1):
            kk = key_ref[pair(i, c)]
            tie = kk == thr2
            rank = seen + jnp.dot(jnp.where(tie, 1.0, 0.0).astype(BF16), prefix,
                                  preferred_element_type=F32)
            tie_add = jnp.where(tie, jnp.where(rank <= need, 0.0, NEG), NEG)
            madd_ref[i, c] = jnp.where(kk > thr2, 0.0, tie_add).astype(madd_ref.dtype)
            seen = rank[:, bs - 1:bs]
        for c in range(i + 1, nd):
            madd_ref[i, c] = jnp.full((bs, bs), NEG, madd_ref.dtype)


def _dsa_select(idx3):
    bsz, t, _ = idx3.shape
    bs = min(ATT_BLOCK, t)
    nd = t // bs
    k_sel = min(B_TOPK_MAX, t // 4)
    col_block = lambda j: pl.BlockSpec((None, t, LANES), lambda b: (b, 0, j))
    return pl.pallas_call(
        functools.partial(_dsa_select_kernel, bs=bs, nd=nd, k_sel=k_sel),
        grid=(bsz,),
        in_specs=[col_block(0), col_block(1), col_block(2)],
        out_specs=pl.BlockSpec((None, nd, nd, bs, bs), lambda b: (b, 0, 0, 0, 0)),
        out_shape=jax.ShapeDtypeStruct((bsz, nd, nd, bs, bs), BF16),
        scratch_shapes=[pltpu.VMEM((nd * (nd + 1) // 2, bs, bs), jnp.int32),
                        pltpu.VMEM((nd, bs, LANES), jnp.int32),
                        pltpu.VMEM((nd, bs, LANES), F32)],
        compiler_params=_params("parallel"),
        name="dsa_select",
    )(idx3, idx3, idx3)


def _dilated(qkv3, tbias):
    return _attention(qkv3, tbias, 1)


def _dsa(qkv3, idx3, tbias):
    return _attention(qkv3, tbias, 0, _dsa_select(idx3))


def _route(logits):
    lane = lax.broadcasted_iota(jnp.int32, logits.shape, 1).astype(F32)
    far = float(LANES)

    def first_argmax(vals):
        top = jnp.max(vals, axis=1, keepdims=True)
        return top, jnp.min(jnp.where(vals == top, lane, far), axis=1, keepdims=True)

    is_group = lane < MOE_GROUPS
    g_top, g_sel = first_argmax(jnp.where(is_group, logits, NEG))
    g_w = 1.0 / jnp.sum(jnp.where(is_group, jnp.exp(logits - g_top), 0.0), axis=1, keepdims=True)
    lo = MOE_GROUPS + MOE_EXPERTS_PER_GROUP * g_sel
    e_logits = jnp.where(lane >= lo, jnp.where(lane < lo + MOE_EXPERTS_PER_GROUP, logits, NEG), NEG)
    v1, i1 = first_argmax(e_logits)
    v2, i2 = first_argmax(jnp.where(lane == i1, NEG, e_logits))
    e2 = jnp.exp(v2 - v1)
    w1 = 1.0 / (1.0 + e2)
    return jnp.where(lane == i1, w1 * g_w, jnp.where(lane == i2, e2 * w1 * g_w, 0.0))


def _merge_kernel(h_ref, g1_ref, oa_ref, ob_ref, oc_ref, od_ref, wg_ref, bg_ref,
                  wba_ref, wbb_ref, wbc_ref, wbd_ref, wo_ref, g2_ref, wr_ref, br_ref,
                  h1_ref, xn2_ref, gates_ref):
    h = h_ref[...]
    xn = _rms(h, g1_ref[...]).astype(BF16)
    merged = None
    for gi, (o_ref, wb_ref) in enumerate(((oa_ref, wba_ref), (ob_ref, wbb_ref),
                                          (oc_ref, wbc_ref), (od_ref, wbd_ref))):
        gate = _sigmoid(jnp.dot(xn, wg_ref[gi], preferred_element_type=F32) + bg_ref[gi:gi + 1, :])
        term = gate * jnp.dot(o_ref[...], wb_ref[...], preferred_element_type=F32)
        merged = term if merged is None else merged + term
    h1 = h + jnp.dot(merged.astype(BF16), wo_ref[...], preferred_element_type=F32)
    h1_ref[...] = h1
    xn2 = _rms(h1, g2_ref[...]).astype(BF16)
    xn2_ref[...] = xn2
    logits = jnp.dot(xn2, wr_ref[...], preferred_element_type=F32) + br_ref[...]
    gates_ref[...] = _route(logits)


def _merge(h, g1, oa, ob, oc, od, wg, bg, wba, wbb, wbc, wbd, wo, g2, wr, br):
    n = h.shape[0]
    tm = min(PROJ_ROWS, n)
    rows = lambda w: pl.BlockSpec((tm, w), lambda i: (i, 0))
    return pl.pallas_call(
        _merge_kernel,
        grid=(n // tm,),
        in_specs=[rows(D_MODEL), _resident(g1.shape),
                  rows(oa.shape[1]), rows(ob.shape[1]), rows(oc.shape[1]), rows(od.shape[1]),
                  _resident(wg.shape), _resident(bg.shape),
                  _resident(wba.shape), _resident(wbb.shape), _resident(wbc.shape),
                  _resident(wbd.shape), _resident(wo.shape), _resident(g2.shape),
                  _resident(wr.shape), _resident(br.shape)],
        out_specs=[rows(D_MODEL), rows(D_MODEL), rows(LANES)],
        out_shape=[jax.ShapeDtypeStruct((n, D_MODEL), F32),
                   jax.ShapeDtypeStruct((n, D_MODEL), BF16),
                   jax.ShapeDtypeStruct((n, LANES), F32)],
        compiler_params=_params("parallel"),
        name="merge",
    )(h, g1, oa, ob, oc, od, wg, bg, wba, wbb, wbc, wbd, wo, g2, wr, br)


def _moe_kernel(x_ref, gates_ref, h1_ref, wg_ref, wu_ref, wd_ref, gf_ref, o_ref, acc_ref,
                *, final_norm):
    e = pl.program_id(1)

    @pl.when(e == 0)
    def _():
        acc_ref[...] = h1_ref[...]

    x = x_ref[...]
    hg = jnp.dot(x, wg_ref[...], preferred_element_type=F32)
    hu = jnp.dot(x, wu_ref[...], preferred_element_type=F32)
    hid = (hg * _sigmoid(hg) * hu).astype(BF16)
    y = jnp.dot(hid, wd_ref[...], preferred_element_type=F32)
    lane = lax.broadcasted_iota(jnp.int32, gates_ref.shape, 1)
    gate = jnp.sum(jnp.where(lane == e + MOE_GROUPS, gates_ref[...], 0.0), axis=1, keepdims=True)
    acc_ref[...] += gate * y

    @pl.when(e == N_EXPERTS - 1)
    def _():
        out = acc_ref[...]
        o_ref[...] = _rms(out, gf_ref[...]) if final_norm else out


def _moe(xn2, gates, h1, wg, wu, wd, gf, final_norm):
    n = xn2.shape[0]
    tm = min(MOE_ROWS, n)
    rows = lambda w: pl.BlockSpec((tm, w), lambda i, e: (i, 0))
    return pl.pallas_call(
        functools.partial(_moe_kernel, final_norm=final_norm),
        grid=(n // tm, N_EXPERTS),
        in_specs=[rows(D_MODEL), rows(LANES), rows(D_MODEL),
                  pl.BlockSpec((None, D_MODEL, MOE_HIDDEN), lambda i, e: (e, 0, 0)),
                  pl.BlockSpec((None, D_MODEL, MOE_HIDDEN), lambda i, e: (e, 0, 0)),
                  pl.BlockSpec((None, MOE_HIDDEN, D_MODEL), lambda i, e: (e, 0, 0)),
                  pl.BlockSpec((1, D_MODEL), lambda i, e: (0, 0))],
        out_specs=rows(D_MODEL),
        out_shape=jax.ShapeDtypeStruct((n, D_MODEL), F32),
        scratch_shapes=[pltpu.VMEM((tm, D_MODEL), F32)],
        compiler_params=_params("parallel", "arbitrary"),
        name="moe",
    )(xn2, gates, h1, wg, wu, wd, gf)


def _inproj_columns():
    offs = np.concatenate([[0], np.cumsum(IN_SPLITS)])
    (aq, ak, av, ag, alr, bq, bk, bv, iq, ik, iw, cq, ck, cv, dy, dx) = offs[:-1]
    cols = []

    def per_head(base, width):
        for h in range(A_HEADS):
            cols.extend(list(range(base + h * width, base + (h + 1) * width)) + [-1] * (LANES - width))

    per_head(aq, A_DK)
    per_head(ak, A_DK)
    per_head(av, A_DV)
    per_head(ag, A_DV)
    cols.extend(list(range(alr, alr + A_LOWRANK)) + [-1] * (LANES - A_LOWRANK))
    cols.extend(range(bq, bq + ATT_W))
    cols.extend(range(iq, iq + IDX_HEADS * IDX_DIM))
    cols.extend(list(range(ik, ik + IDX_DIM)) * IDX_HEADS)
    cols.extend(list(range(iw, iw + IDX_HEADS)) + [-1] * (LANES - IDX_HEADS))
    cols.extend(range(cq, cq + ATT_W))
    cols.extend(range(dy, dy + LRU_W))
    cols = np.asarray(cols, np.int32)
    assert cols.shape[0] == sum(GROUP_WIDTHS)
    return cols


def _pad_heads(a, width, axis):
    a = jnp.moveaxis(a, axis, -1)
    a = a.reshape(a.shape[:-1] + (A_HEADS, width))
    a = jnp.pad(a, [(0, 0)] * (a.ndim - 1) + [(0, LANES - width)])
    return jnp.moveaxis(a.reshape(a.shape[:-2] + (A_HEADS * LANES,)), -1, axis)


def _block_diag(w):
    nb, bw, _ = w.shape
    eye = jnp.eye(nb, dtype=w.dtype)
    return jnp.einsum('ncd,nm->ncmd', w, eye).reshape(nb * bw, nb * bw)


def kernel(x, w_in, a_w2, a_b2, a_gain, conv_w, conv_b, lru_wr, lru_br, lru_wi, lru_bi, lru_lambda, w_gate, b_gate, w_branch, w_out, rel_bias, norm1, norm2, norm_f, moe_wrg, moe_brg, moe_wre, moe_bre, moe_wg, moe_wu, moe_wd):
    bsz, t, _ = x.shape
    n = bsz * t
    depth = w_in.shape[0]
    cols = _inproj_columns()
    tbias = _tbias(rel_bias, t)
    h = x.reshape(n, D_MODEL)
    for l in range(depth):
        w_all = jnp.where(cols[None, :] >= 0, w_in[l][:, np.maximum(cols, 0)], 0.0).astype(BF16)
        gla_in, dsa_in, idx_in, dil_in, lru_in = _inproj(h, norm1[l][None, :], w_all)

        wa2 = jnp.pad(_pad_heads(a_w2[l], A_DK, 1), ((0, LANES - A_LOWRANK), (0, 0)))
        o_a = _gla(gla_in.reshape(bsz, t, GLA_W), wa2, _pad_heads(a_b2[l], A_DK, 0)[None, :],
                   _pad_heads(a_gain[l], A_DV, 0)[None, :])
        o_b = _dsa(dsa_in.reshape(bsz, t, ATT_W), idx_in.reshape(bsz, t, IDX_W), tbias)
        o_c = _dilated(dil_in.reshape(bsz, t, ATT_W), tbias)
        w_ri = jnp.concatenate([_block_diag(lru_wr[l]), _block_diag(lru_wi[l])], axis=1).astype(BF16)
        b_ri = jnp.concatenate([lru_br[l], lru_bi[l]])[None, :]
        o_d = _rglru(lru_in.reshape(bsz, t, LRU_W), conv_w[l], conv_b[l][None, :], w_ri, b_ri,
                     lru_lambda[l][None, :])

        w_router = jnp.pad(jnp.concatenate([moe_wrg[l], moe_wre[l]], axis=1),
                           ((0, 0), (0, LANES - MOE_GROUPS - N_EXPERTS)))
        b_router = jnp.pad(jnp.concatenate([moe_brg[l], moe_bre[l]]),
                           (0, LANES - MOE_GROUPS - N_EXPERTS))[None, :]
        h1, xn2, gates = _merge(
            h, norm1[l][None, :], o_a.reshape(n, -1), o_b.reshape(n, -1), o_c.reshape(n, -1),
            o_d.reshape(n, -1), w_gate[l].astype(BF16), b_gate[l],
            _pad_heads(w_branch[l, 0], A_DV, 0).astype(BF16), w_branch[l, 1].astype(BF16),
            w_branch[l, 2].astype(BF16), w_branch[l, 3].astype(BF16), w_out[l].astype(BF16),
            norm2[l][None, :], w_router.astype(BF16), b_router)
        h = _moe(xn2, gates, h1, moe_wg[l].astype(BF16), moe_wu[l].astype(BF16),
                 moe_wd[l].astype(BF16), norm_f[None, :], final_norm=(l == depth - 1))
    return h.reshape(bsz, t, D_MODEL)
```

```python
import functools
import math

import numpy as np
import jax
import jax.numpy as jnp
from jax import lax
from jax.experimental import pallas as pl
from jax.experimental.pallas import tpu as pltpu

F32 = jnp.float32
BF16 = jnp.bfloat16
HIGHEST = lax.Precision.HIGHEST

D_MODEL = 1024
EPS = 1e-6
A_HEADS, A_DK, A_DV, A_LOWRANK, A_GATE_TAU, A_CHUNK = 4, 48, 96, 16, 16.0, 64
B_HEADS, B_HEAD_DIM, IDX_HEADS, IDX_DIM, B_TOPK_MAX = 6, 64, 4, 32, 256
C_HEADS, C_HEAD_DIM = 6, 64
C_PATTERNS = ((128, 1), (512, 4), (2048, 16))
D_WIDTH, D_BLOCKS, D_CONV, D_C = 384, 8, 4, 8.0
REL_BUCKETS, REL_MAX_DIST = 32, 2048
N_SOFTMAX_HEADS = B_HEADS + C_HEADS
MOE_GROUPS, MOE_EXPERTS_PER_GROUP, MOE_HIDDEN = 4, 4, 512
N_EXPERTS = MOE_GROUPS * MOE_EXPERTS_PER_GROUP
IN_SPLITS = (A_HEADS * A_DK, A_HEADS * A_DK, A_HEADS * A_DV, A_HEADS * A_DV, A_LOWRANK,
             B_HEADS * B_HEAD_DIM, B_HEADS * B_HEAD_DIM, B_HEADS * B_HEAD_DIM,
             IDX_HEADS * IDX_DIM, IDX_DIM, IDX_HEADS,
             C_HEADS * C_HEAD_DIM, C_HEADS * C_HEAD_DIM, C_HEADS * C_HEAD_DIM,
             D_WIDTH, D_WIDTH)

LANES = 128
VMEM_LIMIT = 56 * 1024 * 1024
NEG = -1e30
INT_MIN = -2 ** 31
HALF16 = 2 ** 15
LOG2E = math.log2(math.e)

ATT_BLOCK = 256
SEQ_BLOCK = 256
PROJ_ROWS = 512
MOE_ROWS = 1024

GLA_W = 4 * A_HEADS * LANES + LANES
ATT_W = 3 * B_HEADS * B_HEAD_DIM
IDX_W = 3 * LANES
LRU_W = 2 * D_WIDTH
GROUP_WIDTHS = (GLA_W, ATT_W, IDX_W, ATT_W, LRU_W)
NT_DIMS = (((1,), (1,)), ((), ()))
TN_DIMS = (((0,), (0,)), ((), ()))


def _params(*sem):
    return pltpu.CompilerParams(dimension_semantics=sem, vmem_limit_bytes=VMEM_LIMIT)


def _resident(shape):
    zeros = (0,) * len(shape)
    return pl.BlockSpec(shape, lambda *_: zeros, pipeline_mode=pl.Buffered(1))


def _rms(x, gain):
    return x * lax.rsqrt(jnp.mean(x * x, axis=-1, keepdims=True) + EPS) * gain


def _sigmoid(x):
    return 1.0 / (1.0 + jnp.exp(-x))


def _softplus(x):
    return jnp.maximum(x, 0.0) + jnp.log1p(jnp.exp(-jnp.abs(x)))


def _inproj_kernel(x_ref, g_ref, w_ref, *out_refs):
    xn = _rms(x_ref[...], g_ref[...]).astype(BF16)
    off = 0
    for o_ref in out_refs:
        width = o_ref.shape[-1]
        o_ref[...] = jnp.dot(xn, w_ref[:, off:off + width],
                             preferred_element_type=F32).astype(o_ref.dtype)
        off += width


def _inproj(h, gain, w_all):
    n = h.shape[0]
    tm = min(PROJ_ROWS, n)
    total = sum(GROUP_WIDTHS)
    return pl.pallas_call(
        _inproj_kernel,
        grid=(n // tm,),
        in_specs=[pl.BlockSpec((tm, D_MODEL), lambda i: (i, 0)),
                  _resident((1, D_MODEL)),
                  _resident((D_MODEL, total))],
        out_specs=[pl.BlockSpec((tm, w), lambda i: (i, 0)) for w in GROUP_WIDTHS],
        out_shape=[jax.ShapeDtypeStruct((n, w), BF16) for w in GROUP_WIDTHS],
        compiler_params=_params("parallel"),
        name="inproj",
    )(h, gain, w_all)


def _gla_kernel(x_ref, wa2_ref, ba2_ref, gain_ref, o_ref, st_ref, *, tb):
    @pl.when(pl.program_id(1) == 0)
    def _():
        st_ref[...] = jnp.zeros_like(st_ref)

    hw = A_HEADS * LANES
    x = x_ref[...]
    z = jnp.dot(x[:, 4 * hw:].astype(F32), wa2_ref[...], precision=HIGHEST,
                preferred_element_type=F32) + ba2_ref[...]
    log_a = -_softplus(-z) * (1.0 / A_GATE_TAU)

    r64 = lax.broadcasted_iota(jnp.int32, (A_CHUNK, A_CHUNK), 0)
    c64 = lax.broadcasted_iota(jnp.int32, (A_CHUNK, A_CHUNK), 1)
    causal = c64 <= r64
    tri = jnp.where(causal, 1.0, 0.0).astype(BF16)
    la_hi = log_a.astype(BF16)
    la_lo = (log_a - la_hi.astype(F32)).astype(BF16)
    b_parts, b_last_parts = [], []
    for c in range(tb // A_CHUNK):
        rs = slice(c * A_CHUNK, (c + 1) * A_CHUNK)
        bc = (jnp.dot(tri, la_hi[rs], preferred_element_type=F32)
              + jnp.dot(tri, la_lo[rs], preferred_element_type=F32))
        b_parts.append(bc)
        b_last_parts.append(jnp.broadcast_to(bc[A_CHUNK - 1:A_CHUNK, :], bc.shape))
    b = jnp.concatenate(b_parts, axis=0)
    b_last = jnp.concatenate(b_last_parts, axis=0)

    q = x[:, 0:hw].astype(F32) * (A_DK ** -0.5)
    k = x[:, hw:2 * hw].astype(F32)
    v = x[:, 2 * hw:3 * hw]
    q_dec = (q * jnp.exp(b)).astype(BF16)
    k_dec = (k * jnp.exp(-b)).astype(BF16)
    k_end = (k * jnp.exp(b_last - b)).astype(BF16)
    decay = jnp.exp(b_last)

    heads = []
    for h in range(A_HEADS):
        cs = slice(h * LANES, (h + 1) * LANES)
        state_t = st_ref[h]
        chunks = []
        for c in range(tb // A_CHUNK):
            rs = slice(c * A_CHUNK, (c + 1) * A_CHUNK)
            qc, kc, kec, vc = q_dec[rs, cs], k_dec[rs, cs], k_end[rs, cs], v[rs, cs]
            att = lax.dot_general(qc, kc, NT_DIMS, preferred_element_type=F32)
            att = jnp.where(causal, att, 0.0).astype(BF16)
            oc = jnp.dot(att, vc, preferred_element_type=F32)
            oc += lax.dot_general(qc, state_t.astype(BF16), NT_DIMS, preferred_element_type=F32)
            inc_t = lax.dot_general(vc, kec, TN_DIMS, preferred_element_type=F32)
            state_t = state_t * decay[c * A_CHUNK:c * A_CHUNK + 1, cs] + inc_t
            chunks.append(oc)
        st_ref[h] = state_t
        o_h = jnp.concatenate(chunks, axis=0)
        ms = jnp.sum(o_h * o_h, axis=-1, keepdims=True) * (1.0 / A_DV)
        heads.append(o_h * lax.rsqrt(ms + EPS))
    o = jnp.concatenate(heads, axis=1) * gain_ref[...]
    g = x[:, 3 * hw:4 * hw].astype(F32)
    o_ref[...] = (o * (g * _sigmoid(g))).astype(o_ref.dtype)


def _gla(x3, wa2, ba2, gain):
    bsz, t, _ = x3.shape
    tb = min(SEQ_BLOCK, t)
    hw = A_HEADS * LANES
    return pl.pallas_call(
        functools.partial(_gla_kernel, tb=tb),
        grid=(bsz, t // tb),
        in_specs=[pl.BlockSpec((None, tb, GLA_W), lambda b, i: (b, i, 0)),
                  _resident((LANES, hw)), _resident((1, hw)), _resident((1, hw))],
        out_specs=pl.BlockSpec((None, tb, hw), lambda b, i: (b, i, 0)),
        out_shape=jax.ShapeDtypeStruct((bsz, t, hw), BF16),
        scratch_shapes=[pltpu.VMEM((A_HEADS, LANES, LANES), F32)],
        compiler_params=_params("parallel", "arbitrary"),
        name="gla",
    )(x3, wa2, ba2, gain)


def _rglru_kernel(d_ref, cw_ref, cb_ref, wri_ref, bri_ref, lam_ref, o_ref, xbuf, hc_ref, *, tb):
    halo = 8

    @pl.when(pl.program_id(1) == 0)
    def _():
        xbuf[0:halo, :] = jnp.zeros((halo, D_WIDTH), F32)
        hc_ref[...] = jnp.zeros_like(hc_ref)

    d = d_ref[...]
    y = d[:, 0:D_WIDTH].astype(F32)
    x = d[:, D_WIDTH:].astype(F32)
    xbuf[halo:halo + tb, :] = x
    xc = cb_ref[...]
    for j in range(D_CONV):
        xc = xc + cw_ref[j:j + 1, :] * xbuf[pl.ds(halo - (D_CONV - 1) + j, tb), :]
    xbuf[0:halo, :] = x[tb - halo:tb, :]

    ri = jnp.dot(xc.astype(BF16), wri_ref[...], preferred_element_type=F32) + bri_ref[...]
    r = _sigmoid(ri[:, 0:D_WIDTH])
    ig = _sigmoid(ri[:, D_WIDTH:])
    log_a = (-D_C) * r * _softplus(-lam_ref[...])
    a = jnp.exp(log_a)
    u = jnp.sqrt(-jnp.tanh(log_a) * (a * a + 1.0)) * (ig * xc)

    row = lax.broadcasted_iota(jnp.int32, (tb, D_WIDTH), 0)
    s = 1
    while s < tb:
        valid = row >= s
        u = jnp.where(valid, a * pltpu.roll(u, s, axis=0), 0.0) + u
        a = jnp.where(valid, a * pltpu.roll(a, s, axis=0), a)
        s *= 2
    hs = u + a * hc_ref[0:1, :]
    hc_ref[...] = jnp.broadcast_to(hs[tb - 1:tb, :], hc_ref.shape)
    gelu = 0.5 * y * (1.0 + jnp.tanh(math.sqrt(2.0 / math.pi) * (y + 0.044715 * (y * y * y))))
    o_ref[...] = (gelu * hs).astype(o_ref.dtype)


def _rglru(d3, conv_w, conv_b, w_ri, b_ri, lam):
    bsz, t, _ = d3.shape
    tb = min(SEQ_BLOCK, t)
    return pl.pallas_call(
        functools.partial(_rglru_kernel, tb=tb),
        grid=(bsz, t // tb),
        in_specs=[pl.BlockSpec((None, tb, LRU_W), lambda b, i: (b, i, 0)),
                  _resident((D_CONV, D_WIDTH)), _resident((1, D_WIDTH)),
                  _resident((D_WIDTH, 2 * D_WIDTH)), _resident((1, 2 * D_WIDTH)),
                  _resident((1, D_WIDTH))],
        out_specs=pl.BlockSpec((None, tb, D_WIDTH), lambda b, i: (b, i, 0)),
        out_shape=jax.ShapeDtypeStruct((bsz, t, D_WIDTH), BF16),
        scratch_shapes=[pltpu.VMEM((tb + 8, D_WIDTH), F32), pltpu.VMEM((8, D_WIDTH), F32)],
        compiler_params=_params("parallel", "arbitrary"),
        name="rglru",
    )(d3, conv_w, conv_b, w_ri, b_ri, lam)


def _bucket_edges():
    n = np.arange(REL_MAX_DIST + 1)
    max_exact = REL_BUCKETS // 2
    nf = np.maximum(n, 1).astype(np.float64)
    large = max_exact + (np.log(nf / max_exact) / math.log(REL_MAX_DIST / max_exact)
                         * (REL_BUCKETS - max_exact)).astype(np.int64)
    bucket = np.where(n < max_exact, n, np.minimum(large, REL_BUCKETS - 1))
    assert np.all(np.diff(bucket) >= 0)
    return [int(np.argmax(bucket >= b)) for b in range(REL_BUCKETS)]


def _tbias_kernel(tab_ref, o_ref, *, bs, edges):
    h = pl.program_id(0)
    d = pl.program_id(1)
    row = lax.broadcasted_iota(jnp.int32, (bs, bs), 0)
    col = lax.broadcasted_iota(jnp.int32, (bs, bs), 1)
    dist = d * bs + row - col
    val = jnp.full((bs, bs), tab_ref[0, h], F32)
    for b in range(1, REL_BUCKETS):
        val = jnp.where(dist >= edges[b], tab_ref[b, h], val)
    mult = jnp.zeros((bs, bs), jnp.int32)
    for window, dil in C_PATTERNS:
        hit = jnp.where((dist & (dil - 1)) == 0, 1, 0)
        mult = mult + jnp.where(dist <= window, hit, 0)
    logm = jnp.full((bs, bs), NEG, F32)
    for m in range(1, len(C_PATTERNS) + 1):
        logm = jnp.where(mult == m, math.log(m), logm)
    val = jnp.where(h >= B_HEADS, val + logm, val)
    o_ref[...] = jnp.where(dist >= 0, val * LOG2E, NEG)


def _tbias(rel_bias, t):
    bs = min(ATT_BLOCK, t)
    nd = t // bs
    for _, dil in C_PATTERNS:
        assert dil & (dil - 1) == 0
    return pl.pallas_call(
        functools.partial(_tbias_kernel, bs=bs, edges=_bucket_edges()),
        grid=(N_SOFTMAX_HEADS, nd),
        in_specs=[pl.BlockSpec(memory_space=pltpu.SMEM)],
        out_specs=pl.BlockSpec((None, None, bs, bs), lambda h, d: (h, d, 0, 0)),
        out_shape=jax.ShapeDtypeStruct((N_SOFTMAX_HEADS, nd, bs, bs), F32),
        compiler_params=_params("parallel", "parallel"),
        name="tbias",
    )(rel_bias)


def _attn_kernel(q_ref, k_ref, v_ref, tb_ref, *rest, bs, masked):
    if masked:
        madd_ref, o_ref, vp_ref, m_ref, acc_ref = rest
    else:
        o_ref, vp_ref, m_ref, acc_ref = rest
    i = pl.program_id(1)
    n_heads = vp_ref.shape[0]
    half = LANES // 2
    assert B_HEAD_DIM == half and C_HEAD_DIM == half

    @pl.when(i == 0)
    def _():
        low_t = lax.broadcasted_iota(jnp.int32, (v_ref.shape[0], LANES), 1) < half
        for hp in range(n_heads // 2):
            pair = v_ref[:, hp * LANES:(hp + 1) * LANES]
            vp_ref[2 * hp] = jnp.where(low_t, pair, 1.0).astype(BF16)
            vp_ref[2 * hp + 1] = jnp.where(low_t, 1.0, pair).astype(BF16)

    m_ref[...] = jnp.full(m_ref.shape, NEG, F32)
    acc_ref[...] = jnp.zeros(acc_ref.shape, F32)
    low = lax.broadcasted_iota(jnp.int32, (bs, LANES), 1) < half
    q_heads = []
    for h in range(n_heads):
        qp = q_ref[:, (h // 2) * LANES:(h // 2 + 1) * LANES].astype(F32) * (B_HEAD_DIM ** -0.5 * LOG2E)
        q_heads.append(jnp.where(low if h % 2 == 0 else jnp.logical_not(low), qp, 0.0).astype(BF16))

    def body(c, _):
        start = pl.multiple_of(c * bs, bs)
        extra = madd_ref[c].astype(F32) if masked else None
        for h in range(n_heads):
            ks = k_ref[pl.ds(start, bs), (h // 2) * LANES:(h // 2 + 1) * LANES]
            s = lax.dot_general(q_heads[h], ks, NT_DIMS, preferred_element_type=F32)
            s = s + tb_ref[h, i - c]
            if masked:
                s = s + extra
            m_prev = m_ref[h]
            m_new = jnp.maximum(m_prev, jnp.max(s, axis=1, keepdims=True))
            alpha = jnp.exp2(m_prev - m_new)
            p = jnp.exp2((s - jnp.concatenate([m_new] * (bs // LANES), axis=1)).astype(BF16))
            acc_ref[h] = alpha * acc_ref[h] + jnp.dot(p, vp_ref[h, pl.ds(start, bs), :],
                                                      preferred_element_type=F32)
            m_ref[h] = m_new
        return 0

    lax.fori_loop(0, i + 1, body, 0)

    for hp in range(n_heads // 2):
        a0, a1 = acc_ref[2 * hp], acc_ref[2 * hp + 1]
        o0 = a0 / pltpu.roll(a0, half, axis=1)
        o1 = a1 / pltpu.roll(a1, half, axis=1)
        o_ref[:, hp * LANES:(hp + 1) * LANES] = jnp.where(low, o0, o1).astype(o_ref.dtype)


def _attention(qkv3, tbias, head_block, madd=None):
    bsz, t, width3 = qkv3.shape
    width = width3 // 3
    n_heads = width // B_HEAD_DIM
    bs = min(ATT_BLOCK, t)
    nd = t // bs
    in_specs = [pl.BlockSpec((None, bs, width), lambda b, i: (b, i, 0)),
                pl.BlockSpec((None, t, width), lambda b, i: (b, 0, 1)),
                pl.BlockSpec((None, t, width), lambda b, i: (b, 0, 2)),
                pl.BlockSpec((n_heads, nd, bs, bs), lambda b, i: (head_block, 0, 0, 0))]
    args = [qkv3, qkv3, qkv3, tbias]
    if madd is not None:
        in_specs.append(pl.BlockSpec((None, None, nd, bs, bs), lambda b, i: (b, i, 0, 0, 0)))
        args.append(madd)
    return pl.pallas_call(
        functools.partial(_attn_kernel, bs=bs, masked=madd is not None),
        grid=(bsz, nd),
        in_specs=in_specs,
        out_specs=pl.BlockSpec((None, bs, width), lambda b, i: (b, i, 0)),
        out_shape=jax.ShapeDtypeStruct((bsz, t, width), BF16),
        scratch_shapes=[pltpu.VMEM((n_heads, t, LANES), BF16),
                        pltpu.VMEM((n_heads, bs, LANES), F32),
                        pltpu.VMEM((n_heads, bs, LANES), F32)],
        compiler_params=_params("arbitrary", "arbitrary"),
        name="attn_masked" if madd is not None else "attn",
    )(*args)


def _dsa_select_kernel(iq_ref, ik_ref, iw_ref, madd_ref, key_ref, hi_ref, lo_ref, thr_ref,
                       hi_thr_ref, need_ref, *, bs, nd, k_sel):
    pair = lambda i, c: i * (i + 1) // 2 + c
    lane = lax.broadcasted_iota(jnp.int32, (bs, LANES), 1)
    row = lax.broadcasted_iota(jnp.int32, (bs, bs), 0)
    col = lax.broadcasted_iota(jnp.int32, (bs, bs), 1)
    tiles = bs // LANES

    for i in range(nd):
        iq = iq_ref[i * bs:(i + 1) * bs, :]
        iw = iw_ref[i * bs:(i + 1) * bs, :].astype(F32) * (IDX_HEADS ** -0.5 * IDX_DIM ** -0.5)
        iq_heads = [jnp.where((lane // IDX_DIM) == hh, iq, 0.0).astype(BF16)
                    for hh in range(IDX_HEADS)]
        for c in range(i + 1):
            ikc = ik_ref[c * bs:(c + 1) * bs, :]
            sc = jnp.zeros((bs, bs), F32)
            for hh in range(IDX_HEADS):
                raw = lax.dot_general(iq_heads[hh], ikc, NT_DIMS, preferred_element_type=F32)
                sc = sc + iw[:, hh:hh + 1] * jnp.maximum(raw, 0.0)
            sc = jnp.where(sc == 0.0, 0.0, sc)
            bits = pltpu.bitcast(sc, jnp.int32)
            key = jnp.where(bits < 0, bits ^ 0x7FFFFFFF, bits)
            if c == i:
                key = jnp.where(col <= row, key, INT_MIN)
            key_ref[pair(i, c)] = key
            hi_ref[pair(i, c)] = lax.shift_right_arithmetic(key, 16).astype(jnp.int16)
            lo_ref[pair(i, c)] = ((key & 0xFFFF) - HALF16).astype(jnp.int16)

    kf = float(k_sel)

    def bisect_digit(src_ref, target):
        thr_ref[...] = jnp.full(thr_ref.shape, -HALF16, jnp.int32)

        def step_fn(it, _):
            step = lax.shift_left(jnp.int32(1), 15 - it)
            for i in range(nd):
                thr = thr_ref[i]
                cand = thr + step
                cand16 = cand.astype(jnp.int16)
                acc = jnp.zeros((bs, LANES), jnp.int16)
                for c in range(i + 1):
                    dd = src_ref[pair(i, c)]
                    for j in range(tiles):
                        acc = acc + jnp.where(dd[:, j * LANES:(j + 1) * LANES] >= cand16,
                                              jnp.int16(1), jnp.int16(0))
                cnt = jnp.sum(acc.astype(F32), axis=1, keepdims=True)
                thr_ref[i] = jnp.where(cnt >= target(i), cand, thr)
            return 0

        lax.fori_loop(0, 16, step_fn, 0)

    bisect_digit(hi_ref, lambda i: kf)
    for i in range(nd):
        thr = thr_ref[i]
        thr16 = thr.astype(jnp.int16)
        acc = jnp.zeros((bs, LANES), jnp.int16)
        for c in range(i + 1):
            hh = hi_ref[pair(i, c)]
            for j in range(tiles):
                sl = slice(j * LANES, (j + 1) * LANES)
                acc = acc + jnp.where(hh[:, sl] > thr16, jnp.int16(1), jnp.int16(0))
            lo_ref[pair(i, c)] = jnp.where(hh == jnp.concatenate([thr16] * tiles, axis=1),
                                           lo_ref[pair(i, c)], jnp.int16(-HALF16))
        above = jnp.sum(acc.astype(F32), axis=1, keepdims=True)
        need_ref[i] = jnp.broadcast_to(kf - above, (bs, LANES))
        hi_thr_ref[i] = thr

    bisect_digit(lo_ref, lambda i: need_ref[i])

    prefix = jnp.where(row <= col, 1.0, 0.0).astype(BF16)
    for i in range(nd):
        thr32 = lax.shift_left(hi_thr_ref[i], 16) + (thr_ref[i] + HALF16)
        thr2 = jnp.concatenate([thr32] * tiles, axis=1)
        acc = jnp.zeros((bs, LANES), F32)
        for c in range(i + 1):
            kk = key_ref[pair(i, c)]
            for j in range(tiles):
                acc = acc + jnp.where(kk[:, j * LANES:(j + 1) * LANES] > thr32, 1.0, 0.0)
        need = kf - jnp.sum(acc, axis=1, keepdims=True)
        seen = jnp.zeros((bs, 1), F32)
        for c in range(i + 1):
            kk = key_ref[pair(i, c)]
            tie = kk == thr2
            rank = seen + jnp.dot(jnp.where(tie, 1.0, 0.0).astype(BF16), prefix,
                                  preferred_element_type=F32)
            tie_add = jnp.where(tie, jnp.where(rank <= need, 0.0, NEG), NEG)
            madd_ref[i, c] = jnp.where(kk > thr2, 0.0, tie_add).astype(madd_ref.dtype)
            seen = rank[:, bs - 1:bs]
        for c in range(i + 1, nd):
            madd_ref[i, c] = jnp.full((bs, bs), NEG, madd_ref.dtype)


def _dsa_select(idx3):
    bsz, t, _ = idx3.shape
    bs = min(ATT_BLOCK, t)
    nd = t // bs
    k_sel = min(B_TOPK_MAX, t // 4)
    col_block = lambda j: pl.BlockSpec((None, t, LANES), lambda b: (b, 0, j))
    return pl.pallas_call(
        functools.partial(_dsa_select_kernel, bs=bs, nd=nd, k_sel=k_sel),
        grid=(bsz,),
        in_specs=[col_block(0), col_block(1), col_block(2)],
        out_specs=pl.BlockSpec((None, nd, nd, bs, bs), lambda b: (b, 0, 0, 0, 0)),
        out_shape=jax.ShapeDtypeStruct((bsz, nd, nd, bs, bs), BF16),
        scratch_shapes=[pltpu.VMEM((nd * (nd + 1) // 2, bs, bs), jnp.int32),
                        pltpu.VMEM((nd * (nd + 1) // 2, bs, bs), jnp.int16),
                        pltpu.VMEM((nd * (nd + 1) // 2, bs, bs), jnp.int16),
                        pltpu.VMEM((nd, bs, LANES), jnp.int32),
                        pltpu.VMEM((nd, bs, LANES), jnp.int32),
                        pltpu.VMEM((nd, bs, LANES), F32)],
        compiler_params=_params("parallel"),
        name="dsa_select",
    )(idx3, idx3, idx3)


def _dilated(qkv3, tbias):
    return _attention(qkv3, tbias, 1)


def _dsa(qkv3, idx3, tbias):
    return _attention(qkv3, tbias, 0, _dsa_select(idx3))


def _route(logits):
    lane = lax.broadcasted_iota(jnp.int32, logits.shape, 1).astype(F32)
    far = float(LANES)

    def first_argmax(vals):
        top = jnp.max(vals, axis=1, keepdims=True)
        return top, jnp.min(jnp.where(vals == top, lane, far), axis=1, keepdims=True)

    is_group = lane < MOE_GROUPS
    g_top, g_sel = first_argmax(jnp.where(is_group, logits, NEG))
    g_w = 1.0 / jnp.sum(jnp.where(is_group, jnp.exp(logits - g_top), 0.0), axis=1, keepdims=True)
    lo = MOE_GROUPS + MOE_EXPERTS_PER_GROUP * g_sel
    e_logits = jnp.where(lane >= lo, jnp.where(lane < lo + MOE_EXPERTS_PER_GROUP, logits, NEG), NEG)
    v1, i1 = first_argmax(e_logits)
    v2, i2 = first_argmax(jnp.where(lane == i1, NEG, e_logits))
    e2 = jnp.exp(v2 - v1)
    w1 = 1.0 / (1.0 + e2)
    return jnp.where(lane == i1, w1 * g_w, jnp.where(lane == i2, e2 * w1 * g_w, 0.0))


def _merge_kernel(h_ref, g1_ref, oa_ref, ob_ref, oc_ref, od_ref, wg_ref, bg_ref,
                  wba_ref, wbb_ref, wbc_ref, wbd_ref, wo_ref, g2_ref, wr_ref, br_ref,
                  h1_ref, xn2_ref, gates_ref):
    h = h_ref[...]
    xn = _rms(h, g1_ref[...]).astype(BF16)
    merged = None
    for gi, (o_ref, wb_ref) in enumerate(((oa_ref, wba_ref), (ob_ref, wbb_ref),
                                          (oc_ref, wbc_ref), (od_ref, wbd_ref))):
        gate = _sigmoid(jnp.dot(xn, wg_ref[gi], preferred_element_type=F32) + bg_ref[gi:gi + 1, :])
        term = gate * jnp.dot(o_ref[...], wb_ref[...], preferred_element_type=F32)
        merged = term if merged is None else merged + term
    h1 = h + jnp.dot(merged.astype(BF16), wo_ref[...], preferred_element_type=F32)
    h1_ref[...] = h1
    xn2 = _rms(h1, g2_ref[...]).astype(BF16)
    xn2_ref[...] = xn2
    logits = jnp.dot(xn2, wr_ref[...], preferred_element_type=F32) + br_ref[...]
    gates_ref[...] = _route(logits)


def _merge(h, g1, oa, ob, oc, od, wg, bg, wba, wbb, wbc, wbd, wo, g2, wr, br):
    n = h.shape[0]
    tm = min(PROJ_ROWS, n)
    rows = lambda w: pl.BlockSpec((tm, w), lambda i: (i, 0))
    return pl.pallas_call(
        _merge_kernel,
        grid=(n // tm,),
        in_specs=[rows(D_MODEL), _resident(g1.shape),
                  rows(oa.shape[1]), rows(ob.shape[1]), rows(oc.shape[1]), rows(od.shape[1]),
                  _resident(wg.shape), _resident(bg.shape),
                  _resident(wba.shape), _resident(wbb.shape), _resident(wbc.shape),
                  _resident(wbd.shape), _resident(wo.shape), _resident(g2.shape),
                  _resident(wr.shape), _resident(br.shape)],
        out_specs=[rows(D_MODEL), rows(D_MODEL), rows(LANES)],
        out_shape=[jax.ShapeDtypeStruct((n, D_MODEL), F32),
                   jax.ShapeDtypeStruct((n, D_MODEL), BF16),
                   jax.ShapeDtypeStruct((n, LANES), F32)],
        compiler_params=_params("parallel"),
        name="merge",
    )(h, g1, oa, ob, oc, od, wg, bg, wba, wbb, wbc, wbd, wo, g2, wr, br)


def _moe_kernel(x_ref, gates_ref, h1_ref, wg_ref, wu_ref, wd_ref, gf_ref, o_ref, acc_ref,
                *, final_norm):
    e = pl.program_id(1)

    @pl.when(e == 0)
    def _():
        acc_ref[...] = h1_ref[...]

    x = x_ref[...]
    hg = jnp.dot(x, wg_ref[...], preferred_element_type=F32)
    hu = jnp.dot(x, wu_ref[...], preferred_element_type=F32)
    hid = (hg * _sigmoid(hg) * hu).astype(BF16)
    y = jnp.dot(hid, wd_ref[...], preferred_element_type=F32)
    lane = lax.broadcasted_iota(jnp.int32, gates_ref.shape, 1)
    gate = jnp.sum(jnp.where(lane == e + MOE_GROUPS, gates_ref[...], 0.0), axis=1, keepdims=True)
    acc_ref[...] += gate * y

    @pl.when(e == N_EXPERTS - 1)
    def _():
        out = acc_ref[...]
        o_ref[...] = _rms(out, gf_ref[...]) if final_norm else out


def _moe(xn2, gates, h1, wg, wu, wd, gf, final_norm):
    n = xn2.shape[0]
    tm = min(MOE_ROWS, n)
    rows = lambda w: pl.BlockSpec((tm, w), lambda i, e: (i, 0))
    return pl.pallas_call(
        functools.partial(_moe_kernel, final_norm=final_norm),
        grid=(n // tm, N_EXPERTS),
        in_specs=[rows(D_MODEL), rows(LANES), rows(D_MODEL),
                  pl.BlockSpec((None, D_MODEL, MOE_HIDDEN), lambda i, e: (e, 0, 0)),
                  pl.BlockSpec((None, D_MODEL, MOE_HIDDEN), lambda i, e: (e, 0, 0)),
                  pl.BlockSpec((None, MOE_HIDDEN, D_MODEL), lambda i, e: (e, 0, 0)),
                  pl.BlockSpec((1, D_MODEL), lambda i, e: (0, 0))],
        out_specs=rows(D_MODEL),
        out_shape=jax.ShapeDtypeStruct((n, D_MODEL), F32),
        scratch_shapes=[pltpu.VMEM((tm, D_MODEL), F32)],
        compiler_params=_params("parallel", "arbitrary"),
        name="moe",
    )(xn2, gates, h1, wg, wu, wd, gf)


def _inproj_columns():
    offs = np.concatenate([[0], np.cumsum(IN_SPLITS)])
    (aq, ak, av, ag, alr, bq, bk, bv, iq, ik, iw, cq, ck, cv, dy, dx) = offs[:-1]
    cols = []

    def per_head(base, width):
        for h in range(A_HEADS):
            cols.extend(list(range(base + h * width, base + (h + 1) * width)) + [-1] * (LANES - width))

    per_head(aq, A_DK)
    per_head(ak, A_DK)
    per_head(av, A_DV)
    per_head(ag, A_DV)
    cols.extend(list(range(alr, alr + A_LOWRANK)) + [-1] * (LANES - A_LOWRANK))
    cols.extend(range(bq, bq + ATT_W))
    cols.extend(range(iq, iq + IDX_HEADS * IDX_DIM))
    cols.extend(list(range(ik, ik + IDX_DIM)) * IDX_HEADS)
    cols.extend(list(range(iw, iw + IDX_HEADS)) + [-1] * (LANES - IDX_HEADS))
    cols.extend(range(cq, cq + ATT_W))
    cols.extend(range(dy, dy + LRU_W))
    cols = np.asarray(cols, np.int32)
    assert cols.shape[0] == sum(GROUP_WIDTHS)
    return cols


def _pad_heads(a, width, axis):
    a = jnp.moveaxis(a, axis, -1)
    a = a.reshape(a.shape[:-1] + (A_HEADS, width))
    a = jnp.pad(a, [(0, 0)] * (a.ndim - 1) + [(0, LANES - width)])
    return jnp.moveaxis(a.reshape(a.shape[:-2] + (A_HEADS * LANES,)), -1, axis)


def _block_diag(w):
    nb, bw, _ = w.shape
    eye = jnp.eye(nb, dtype=w.dtype)
    return jnp.einsum('ncd,nm->ncmd', w, eye).reshape(nb * bw, nb * bw)


def kernel(x, w_in, a_w2, a_b2, a_gain, conv_w, conv_b, lru_wr, lru_br, lru_wi, lru_bi, lru_lambda, w_gate, b_gate, w_branch, w_out, rel_bias, norm1, norm2, norm_f, moe_wrg, moe_brg, moe_wre, moe_bre, moe_wg, moe_wu, moe_wd):
    bsz, t, _ = x.shape
    n = bsz * t
    depth = w_in.shape[0]
    cols = _inproj_columns()
    tbias = _tbias(rel_bias, t)
    h = x.reshape(n, D_MODEL)
    for l in range(depth):
        w_all = jnp.where(cols[None, :] >= 0, w_in[l][:, np.maximum(cols, 0)], 0.0).astype(BF16)
        gla_in, dsa_in, idx_in, dil_in, lru_in = _inproj(h, norm1[l][None, :], w_all)

        wa2 = jnp.pad(_pad_heads(a_w2[l], A_DK, 1), ((0, LANES - A_LOWRANK), (0, 0)))
        o_a = _gla(gla_in.reshape(bsz, t, GLA_W), wa2, _pad_heads(a_b2[l], A_DK, 0)[None, :],
                   _pad_heads(a_gain[l], A_DV, 0)[None, :])
        o_b = _dsa(dsa_in.reshape(bsz, t, ATT_W), idx_in.reshape(bsz, t, IDX_W), tbias)
        o_c = _dilated(dil_in.reshape(bsz, t, ATT_W), tbias)
        w_ri = jnp.concatenate([_block_diag(lru_wr[l]), _block_diag(lru_wi[l])], axis=1).astype(BF16)
        b_ri = jnp.concatenate([lru_br[l], lru_bi[l]])[None, :]
        o_d = _rglru(lru_in.reshape(bsz, t, LRU_W), conv_w[l], conv_b[l][None, :], w_ri, b_ri,
                     lru_lambda[l][None, :])

        w_router = jnp.pad(jnp.concatenate([moe_wrg[l], moe_wre[l]], axis=1),
                           ((0, 0), (0, LANES - MOE_GROUPS - N_EXPERTS)))
        b_router = jnp.pad(jnp.concatenate([moe_brg[l], moe_bre[l]]),
                           (0, LANES - MOE_GROUPS - N_EXPERTS))[None, :]
        h1, xn2, gates = _merge(
            h, norm1[l][None, :], o_a.reshape(n, -1), o_b.reshape(n, -1), o_c.reshape(n, -1),
            o_d.reshape(n, -1), w_gate[l].astype(BF16), b_gate[l],
            _pad_heads(w_branch[l, 0], A_DV, 0).astype(BF16), w_branch[l, 1].astype(BF16),
            w_branch[l, 2].astype(BF16), w_branch[l, 3].astype(BF16), w_out[l].astype(BF16),
            norm2[l][None, :], w_router.astype(BF16), b_router)
        h = _moe(xn2, gates, h1, moe_wg[l].astype(BF16), moe_wu[l].astype(BF16),
                 moe_wd[l].astype(BF16), norm_f[None, :], final_norm=(l == depth - 1))
    return h.reshape(bsz, t, D_MODEL)
```

```python
import functools
import math

import numpy as np
import jax
import jax.numpy as jnp
from jax import lax
from jax.experimental import pallas as pl
from jax.experimental.pallas import tpu as pltpu

F32 = jnp.float32
BF16 = jnp.bfloat16
HIGHEST = lax.Precision.HIGHEST

D_MODEL = 1024
EPS = 1e-6
A_HEADS, A_DK, A_DV, A_LOWRANK, A_GATE_TAU, A_CHUNK = 4, 48, 96, 16, 16.0, 64
B_HEADS, B_HEAD_DIM, IDX_HEADS, IDX_DIM, B_TOPK_MAX = 6, 64, 4, 32, 256
C_HEADS, C_HEAD_DIM = 6, 64
C_PATTERNS = ((128, 1), (512, 4), (2048, 16))
D_WIDTH, D_BLOCKS, D_CONV, D_C = 384, 8, 4, 8.0
REL_BUCKETS, REL_MAX_DIST = 32, 2048
N_SOFTMAX_HEADS = B_HEADS + C_HEADS
MOE_GROUPS, MOE_EXPERTS_PER_GROUP, MOE_HIDDEN = 4, 4, 512
N_EXPERTS = MOE_GROUPS * MOE_EXPERTS_PER_GROUP
IN_SPLITS = (A_HEADS * A_DK, A_HEADS * A_DK, A_HEADS * A_DV, A_HEADS * A_DV, A_LOWRANK,
             B_HEADS * B_HEAD_DIM, B_HEADS * B_HEAD_DIM, B_HEADS * B_HEAD_DIM,
             IDX_HEADS * IDX_DIM, IDX_DIM, IDX_HEADS,
             C_HEADS * C_HEAD_DIM, C_HEADS * C_HEAD_DIM, C_HEADS * C_HEAD_DIM,
             D_WIDTH, D_WIDTH)

LANES = 128
VMEM_LIMIT = 56 * 1024 * 1024
NEG = -1e30
INT_MIN = -2 ** 31
LOG2E = math.log2(math.e)

ATT_BLOCK = 256
SEQ_BLOCK = 256
PROJ_ROWS = 512
MOE_ROWS = 1024

GLA_W = 4 * A_HEADS * LANES + LANES
ATT_W = 3 * B_HEADS * B_HEAD_DIM
IDX_W = 3 * LANES
LRU_W = 2 * D_WIDTH
GROUP_WIDTHS = (GLA_W, ATT_W, IDX_W, ATT_W, LRU_W)
NT_DIMS = (((1,), (1,)), ((), ()))
TN_DIMS = (((0,), (0,)), ((), ()))


def _params(*sem):
    return pltpu.CompilerParams(dimension_semantics=sem, vmem_limit_bytes=VMEM_LIMIT)


def _resident(shape):
    zeros = (0,) * len(shape)
    return pl.BlockSpec(shape, lambda *_: zeros, pipeline_mode=pl.Buffered(1))


def _rms(x, gain):
    return x * lax.rsqrt(jnp.mean(x * x, axis=-1, keepdims=True) + EPS) * gain


def _sigmoid(x):
    return 1.0 / (1.0 + jnp.exp(-x))


def _softplus(x):
    return jnp.maximum(x, 0.0) + jnp.log1p(jnp.exp(-jnp.abs(x)))


def _inproj_kernel(x_ref, g_ref, w_ref, *out_refs):
    xn = _rms(x_ref[...], g_ref[...]).astype(BF16)
    off = 0
    for o_ref in out_refs:
        width = o_ref.shape[-1]
        o_ref[...] = jnp.dot(xn, w_ref[:, off:off + width],
                             preferred_element_type=F32).astype(o_ref.dtype)
        off += width


def _inproj(h, gain, w_all):
    n = h.shape[0]
    tm = min(PROJ_ROWS, n)
    total = sum(GROUP_WIDTHS)
    return pl.pallas_call(
        _inproj_kernel,
        grid=(n // tm,),
        in_specs=[pl.BlockSpec((tm, D_MODEL), lambda i: (i, 0)),
                  _resident((1, D_MODEL)),
                  _resident((D_MODEL, total))],
        out_specs=[pl.BlockSpec((tm, w), lambda i: (i, 0)) for w in GROUP_WIDTHS],
        out_shape=[jax.ShapeDtypeStruct((n, w), BF16) for w in GROUP_WIDTHS],
        compiler_params=_params("parallel"),
        name="inproj",
    )(h, gain, w_all)


def _gla_kernel(x_ref, wa2_ref, ba2_ref, gain_ref, o_ref, st_ref, *, tb):
    @pl.when(pl.program_id(1) == 0)
    def _():
        st_ref[...] = jnp.zeros_like(st_ref)

    hw = A_HEADS * LANES
    x = x_ref[...]
    z = jnp.dot(x[:, 4 * hw:].astype(F32), wa2_ref[...], precision=HIGHEST,
                preferred_element_type=F32) + ba2_ref[...]
    log_a = -_softplus(-z) * (1.0 / A_GATE_TAU)

    r64 = lax.broadcasted_iota(jnp.int32, (A_CHUNK, A_CHUNK), 0)
    c64 = lax.broadcasted_iota(jnp.int32, (A_CHUNK, A_CHUNK), 1)
    causal = c64 <= r64
    tri = jnp.where(causal, 1.0, 0.0).astype(BF16)
    la_hi = log_a.astype(BF16)
    la_lo = (log_a - la_hi.astype(F32)).astype(BF16)
    b_parts, b_last_parts = [], []
    for c in range(tb // A_CHUNK):
        rs = slice(c * A_CHUNK, (c + 1) * A_CHUNK)
        bc = (jnp.dot(tri, la_hi[rs], preferred_element_type=F32)
              + jnp.dot(tri, la_lo[rs], preferred_element_type=F32))
        b_parts.append(bc)
        b_last_parts.append(jnp.broadcast_to(bc[A_CHUNK - 1:A_CHUNK, :], bc.shape))
    b = jnp.concatenate(b_parts, axis=0)
    b_last = jnp.concatenate(b_last_parts, axis=0)

    q = x[:, 0:hw].astype(F32) * (A_DK ** -0.5)
    k = x[:, hw:2 * hw].astype(F32)
    v = x[:, 2 * hw:3 * hw]
    q_dec = (q * jnp.exp(b)).astype(BF16)
    k_dec = (k * jnp.exp(-b)).astype(BF16)
    k_end = (k * jnp.exp(b_last - b)).astype(BF16)
    decay = jnp.exp(b_last)

    heads = []
    for h in range(A_HEADS):
        cs = slice(h * LANES, (h + 1) * LANES)
        state_t = st_ref[h]
        chunks = []
        for c in range(tb // A_CHUNK):
            rs = slice(c * A_CHUNK, (c + 1) * A_CHUNK)
            qc, kc, kec, vc = q_dec[rs, cs], k_dec[rs, cs], k_end[rs, cs], v[rs, cs]
            att = lax.dot_general(qc, kc, NT_DIMS, preferred_element_type=F32)
            att = jnp.where(causal, att, 0.0).astype(BF16)
            oc = jnp.dot(att, vc, preferred_element_type=F32)
            oc += lax.dot_general(qc, state_t.astype(BF16), NT_DIMS, preferred_element_type=F32)
            inc_t = lax.dot_general(vc, kec, TN_DIMS, preferred_element_type=F32)
            state_t = state_t * decay[c * A_CHUNK:c * A_CHUNK + 1, cs] + inc_t
            chunks.append(oc)
        st_ref[h] = state_t
        o_h = jnp.concatenate(chunks, axis=0)
        ms = jnp.sum(o_h * o_h, axis=-1, keepdims=True) * (1.0 / A_DV)
        heads.append(o_h * lax.rsqrt(ms + EPS))
    o = jnp.concatenate(heads, axis=1) * gain_ref[...]
    g = x[:, 3 * hw:4 * hw].astype(F32)
    o_ref[...] = (o * (g * _sigmoid(g))).astype(o_ref.dtype)


def _gla(x3, wa2, ba2, gain):
    bsz, t, _ = x3.shape
    tb = min(SEQ_BLOCK, t)
    hw = A_HEADS * LANES
    return pl.pallas_call(
        functools.partial(_gla_kernel, tb=tb),
        grid=(bsz, t // tb),
        in_specs=[pl.BlockSpec((None, tb, GLA_W), lambda b, i: (b, i, 0)),
                  _resident((LANES, hw)), _resident((1, hw)), _resident((1, hw))],
        out_specs=pl.BlockSpec((None, tb, hw), lambda b, i: (b, i, 0)),
        out_shape=jax.ShapeDtypeStruct((bsz, t, hw), BF16),
        scratch_shapes=[pltpu.VMEM((A_HEADS, LANES, LANES), F32)],
        compiler_params=_params("parallel", "arbitrary"),
        name="gla",
    )(x3, wa2, ba2, gain)


def _rglru_kernel(d_ref, cw_ref, cb_ref, wri_ref, bri_ref, lam_ref, o_ref, xbuf, hc_ref, *, tb):
    halo = 8

    @pl.when(pl.program_id(1) == 0)
    def _():
        xbuf[0:halo, :] = jnp.zeros((halo, D_WIDTH), F32)
        hc_ref[...] = jnp.zeros_like(hc_ref)

    d = d_ref[...]
    y = d[:, 0:D_WIDTH].astype(F32)
    x = d[:, D_WIDTH:].astype(F32)
    xbuf[halo:halo + tb, :] = x
    xc = cb_ref[...]
    for j in range(D_CONV):
        xc = xc + cw_ref[j:j + 1, :] * xbuf[pl.ds(halo - (D_CONV - 1) + j, tb), :]
    xbuf[0:halo, :] = x[tb - halo:tb, :]

    ri = jnp.dot(xc.astype(BF16), wri_ref[...], preferred_element_type=F32) + bri_ref[...]
    r = _sigmoid(ri[:, 0:D_WIDTH])
    ig = _sigmoid(ri[:, D_WIDTH:])
    log_a = (-D_C) * r * _softplus(-lam_ref[...])
    a = jnp.exp(log_a)
    u = jnp.sqrt(-jnp.tanh(log_a) * (a * a + 1.0)) * (ig * xc)

    row = lax.broadcasted_iota(jnp.int32, (tb, D_WIDTH), 0)
    s = 1
    while s < tb:
        valid = row >= s
        u = jnp.where(valid, a * pltpu.roll(u, s, axis=0), 0.0) + u
        a = jnp.where(valid, a * pltpu.roll(a, s, axis=0), a)
        s *= 2
    hs = u + a * hc_ref[0:1, :]
    hc_ref[...] = jnp.broadcast_to(hs[tb - 1:tb, :], hc_ref.shape)
    gelu = 0.5 * y * (1.0 + jnp.tanh(math.sqrt(2.0 / math.pi) * (y + 0.044715 * (y * y * y))))
    o_ref[...] = (gelu * hs).astype(o_ref.dtype)


def _rglru(d3, conv_w, conv_b, w_ri, b_ri, lam):
    bsz, t, _ = d3.shape
    tb = min(SEQ_BLOCK, t)
    return pl.pallas_call(
        functools.partial(_rglru_kernel, tb=tb),
        grid=(bsz, t // tb),
        in_specs=[pl.BlockSpec((None, tb, LRU_W), lambda b, i: (b, i, 0)),
                  _resident((D_CONV, D_WIDTH)), _resident((1, D_WIDTH)),
                  _resident((D_WIDTH, 2 * D_WIDTH)), _resident((1, 2 * D_WIDTH)),
                  _resident((1, D_WIDTH))],
        out_specs=pl.BlockSpec((None, tb, D_WIDTH), lambda b, i: (b, i, 0)),
        out_shape=jax.ShapeDtypeStruct((bsz, t, D_WIDTH), BF16),
        scratch_shapes=[pltpu.VMEM((tb + 8, D_WIDTH), F32), pltpu.VMEM((8, D_WIDTH), F32)],
        compiler_params=_params("parallel", "arbitrary"),
        name="rglru",
    )(d3, conv_w, conv_b, w_ri, b_ri, lam)


def _bucket_edges():
    n = np.arange(REL_MAX_DIST + 1)
    max_exact = REL_BUCKETS // 2
    nf = np.maximum(n, 1).astype(np.float64)
    large = max_exact + (np.log(nf / max_exact) / math.log(REL_MAX_DIST / max_exact)
                         * (REL_BUCKETS - max_exact)).astype(np.int64)
    bucket = np.where(n < max_exact, n, np.minimum(large, REL_BUCKETS - 1))
    assert np.all(np.diff(bucket) >= 0)
    return [int(np.argmax(bucket >= b)) for b in range(REL_BUCKETS)]


def _tbias_kernel(tab_ref, o_ref, *, bs, edges):
    h = pl.program_id(0)
    d = pl.program_id(1)
    row = lax.broadcasted_iota(jnp.int32, (bs, bs), 0)
    col = lax.broadcasted_iota(jnp.int32, (bs, bs), 1)
    dist = d * bs + row - col
    val = jnp.full((bs, bs), tab_ref[0, h], F32)
    for b in range(1, REL_BUCKETS):
        val = jnp.where(dist >= edges[b], tab_ref[b, h], val)
    mult = jnp.zeros((bs, bs), jnp.int32)
    for window, dil in C_PATTERNS:
        hit = jnp.where((dist & (dil - 1)) == 0, 1, 0)
        mult = mult + jnp.where(dist <= window, hit, 0)
    logm = jnp.full((bs, bs), NEG, F32)
    for m in range(1, len(C_PATTERNS) + 1):
        logm = jnp.where(mult == m, math.log(m), logm)
    val = jnp.where(h >= B_HEADS, val + logm, val)
    o_ref[...] = jnp.where(dist >= 0, val * LOG2E, NEG)


def _tbias(rel_bias, t):
    bs = min(ATT_BLOCK, t)
    nd = t // bs
    for _, dil in C_PATTERNS:
        assert dil & (dil - 1) == 0
    return pl.pallas_call(
        functools.partial(_tbias_kernel, bs=bs, edges=_bucket_edges()),
        grid=(N_SOFTMAX_HEADS, nd),
        in_specs=[pl.BlockSpec(memory_space=pltpu.SMEM)],
        out_specs=pl.BlockSpec((None, None, bs, bs), lambda h, d: (h, d, 0, 0)),
        out_shape=jax.ShapeDtypeStruct((N_SOFTMAX_HEADS, nd, bs, bs), F32),
        compiler_params=_params("parallel", "parallel"),
        name="tbias",
    )(rel_bias)


def _attn_kernel(qb_ref, kb_ref, vb_ref, qc_ref, kc_ref, vc_ref, tb_ref, madd_ref,
                 ob_ref, oc_ref, vp_ref, m_ref, acc_ref, *, bs):
    i = pl.program_id(1)
    streams = ((qb_ref, kb_ref, vb_ref, ob_ref), (qc_ref, kc_ref, vc_ref, oc_ref))
    per_stream = B_HEADS
    n_heads = vp_ref.shape[0]
    half = LANES // 2
    assert B_HEAD_DIM == half and C_HEAD_DIM == half and B_HEADS == C_HEADS

    @pl.when(i == 0)
    def _():
        low_t = lax.broadcasted_iota(jnp.int32, (vb_ref.shape[0], LANES), 1) < half
        for h in range(0, n_heads, 2):
            v_ref = streams[h // per_stream][2]
            hp = (h % per_stream) // 2
            pair = v_ref[:, hp * LANES:(hp + 1) * LANES]
            vp_ref[h] = jnp.where(low_t, pair, 1.0).astype(BF16)
            vp_ref[h + 1] = jnp.where(low_t, 1.0, pair).astype(BF16)

    m_ref[...] = jnp.full(m_ref.shape, NEG, F32)
    acc_ref[...] = jnp.zeros(acc_ref.shape, F32)
    low = lax.broadcasted_iota(jnp.int32, (bs, LANES), 1) < half
    q_heads = []
    for h in range(n_heads):
        q_ref = streams[h // per_stream][0]
        hp = (h % per_stream) // 2
        qp = q_ref[:, hp * LANES:(hp + 1) * LANES].astype(F32) * (B_HEAD_DIM ** -0.5 * LOG2E)
        q_heads.append(jnp.where(low if h % 2 == 0 else jnp.logical_not(low), qp, 0.0).astype(BF16))

    def body(c, _):
        start = pl.multiple_of(c * bs, bs)
        extra = madd_ref[c].astype(F32)
        for h in range(n_heads):
            k_ref = streams[h // per_stream][1]
            hp = (h % per_stream) // 2
            ks = k_ref[pl.ds(start, bs), hp * LANES:(hp + 1) * LANES]
            s = lax.dot_general(q_heads[h], ks, NT_DIMS, preferred_element_type=F32)
            s = s + tb_ref[h, i - c]
            if h < per_stream:
                s = s + extra
            m_prev = m_ref[h]
            m_new = jnp.maximum(m_prev, jnp.max(s, axis=1, keepdims=True))
            alpha = jnp.exp2(m_prev - m_new)
            p = jnp.exp2((s - jnp.concatenate([m_new] * (bs // LANES), axis=1)).astype(BF16))
            acc_ref[h] = alpha * acc_ref[h] + jnp.dot(p, vp_ref[h, pl.ds(start, bs), :],
                                                      preferred_element_type=F32)
            m_ref[h] = m_new
        return 0

    lax.fori_loop(0, i + 1, body, 0)

    for h in range(0, n_heads, 2):
        o_ref = streams[h // per_stream][3]
        hp = (h % per_stream) // 2
        a0, a1 = acc_ref[h], acc_ref[h + 1]
        o0 = a0 / pltpu.roll(a0, half, axis=1)
        o1 = a1 / pltpu.roll(a1, half, axis=1)
        o_ref[:, hp * LANES:(hp + 1) * LANES] = jnp.where(low, o0, o1).astype(o_ref.dtype)


def _attention(dsa3, dil3, tbias, madd):
    bsz, t, width3 = dsa3.shape
    width = width3 // 3
    bs = min(ATT_BLOCK, t)
    nd = t // bs
    q_spec = pl.BlockSpec((None, bs, width), lambda b, i: (b, i, 0))
    k_spec = pl.BlockSpec((None, t, width), lambda b, i: (b, 0, 1), pipeline_mode=pl.Buffered(1))
    v_spec = pl.BlockSpec((None, t, width), lambda b, i: (b, 0, 2), pipeline_mode=pl.Buffered(1))
    out_spec = pl.BlockSpec((None, bs, width), lambda b, i: (b, i, 0))
    out_shape = jax.ShapeDtypeStruct((bsz, t, width), BF16)
    return pl.pallas_call(
        functools.partial(_attn_kernel, bs=bs),
        grid=(bsz, nd),
        in_specs=[q_spec, k_spec, v_spec, q_spec, k_spec, v_spec,
                  _resident(tbias.shape),
                  pl.BlockSpec((None, None, nd, bs, bs), lambda b, i: (b, i, 0, 0, 0))],
        out_specs=[out_spec, out_spec],
        out_shape=[out_shape, out_shape],
        scratch_shapes=[pltpu.VMEM((N_SOFTMAX_HEADS, t, LANES), BF16),
                        pltpu.VMEM((N_SOFTMAX_HEADS, bs, LANES), F32),
                        pltpu.VMEM((N_SOFTMAX_HEADS, bs, LANES), F32)],
        compiler_params=_params("arbitrary", "arbitrary"),
        name="attn",
    )(dsa3, dsa3, dsa3, dil3, dil3, dil3, tbias, madd)


def _dsa_select_kernel(iq_ref, ik_ref, iw_ref, madd_ref, key_ref, thr_ref, need_ref,
                       *, bs, nd, k_sel):
    pair = lambda i, c: i * (i + 1) // 2 + c
    lane = lax.broadcasted_iota(jnp.int32, (bs, LANES), 1)
    row = lax.broadcasted_iota(jnp.int32, (bs, bs), 0)
    col = lax.broadcasted_iota(jnp.int32, (bs, bs), 1)
    tiles = bs // LANES

    for i in range(nd):
        iq = iq_ref[i * bs:(i + 1) * bs, :]
        iw = iw_ref[i * bs:(i + 1) * bs, :].astype(F32) * (IDX_HEADS ** -0.5 * IDX_DIM ** -0.5)
        iq_heads = [jnp.where((lane // IDX_DIM) == hh, iq, 0.0).astype(BF16)
                    for hh in range(IDX_HEADS)]
        for c in range(i + 1):
            ikc = ik_ref[c * bs:(c + 1) * bs, :]
            sc = jnp.zeros((bs, bs), F32)
            for hh in range(IDX_HEADS):
                raw = lax.dot_general(iq_heads[hh], ikc, NT_DIMS, preferred_element_type=F32)
                sc = sc + iw[:, hh:hh + 1] * jnp.maximum(raw, 0.0)
            sc = jnp.where(sc == 0.0, 0.0, sc)
            bits = pltpu.bitcast(sc, jnp.int32)
            key = jnp.where(bits < 0, bits ^ 0x7FFFFFFF, bits)
            if c == i:
                key = jnp.where(col <= row, key, INT_MIN)
            key_ref[pair(i, c)] = key

    thr_ref[...] = jnp.full(thr_ref.shape, INT_MIN, jnp.int32)
    kf = float(k_sel)
    n_bits = 32

    def bisect(it, _):
        searching = it < n_bits
        step = jnp.where(searching, lax.shift_left(jnp.int32(1), jnp.maximum(n_bits - 1 - it, 0)), 1)
        for i in range(nd):
            thr = thr_ref[i]
            cand = thr + step
            acc = jnp.zeros((bs, LANES), F32)
            for c in range(i + 1):
                kk = key_ref[pair(i, c)]
                for j in range(tiles):
                    acc = acc + jnp.where(kk[:, j * LANES:(j + 1) * LANES] >= cand, 1.0, 0.0)
            cnt = jnp.sum(acc, axis=1, keepdims=True)
            thr_ref[i] = jnp.where(jnp.logical_and(searching, cnt >= kf), cand, thr)
            need_ref[i] = jnp.broadcast_to(kf - cnt, (bs, LANES))
        return 0

    lax.fori_loop(0, n_bits + 1, bisect, 0)

    prefix = jnp.where(row <= col, 1.0, 0.0).astype(BF16)
    for i in range(nd):
        thr2 = jnp.concatenate([thr_ref[i]] * tiles, axis=1)
        need = need_ref[i][:, 0:1]
        seen = jnp.zeros((bs, 1), F32)
        for c in range(i + 1):
            kk = key_ref[pair(i, c)]
            tie = kk == thr2
            rank = seen + jnp.dot(jnp.where(tie, 1.0, 0.0).astype(BF16), prefix,
                                  preferred_element_type=F32)
            tie_add = jnp.where(tie, jnp.where(rank <= need, 0.0, NEG), NEG)
            madd_ref[i, c] = jnp.where(kk > thr2, 0.0, tie_add).astype(madd_ref.dtype)
            seen = rank[:, bs - 1:bs]
        for c in range(i + 1, nd):
            madd_ref[i, c] = jnp.full((bs, bs), NEG, madd_ref.dtype)


def _dsa_select(idx3):
    bsz, t, _ = idx3.shape
    bs = min(ATT_BLOCK, t)
    nd = t // bs
    k_sel = min(B_TOPK_MAX, t // 4)
    col_block = lambda j: pl.BlockSpec((None, t, LANES), lambda b: (b, 0, j))
    return pl.pallas_call(
        functools.partial(_dsa_select_kernel, bs=bs, nd=nd, k_sel=k_sel),
        grid=(bsz,),
        in_specs=[col_block(0), col_block(1), col_block(2)],
        out_specs=pl.BlockSpec((None, nd, nd, bs, bs), lambda b: (b, 0, 0, 0, 0)),
        out_shape=jax.ShapeDtypeStruct((bsz, nd, nd, bs, bs), BF16),
        scratch_shapes=[pltpu.VMEM((nd * (nd + 1) // 2, bs, bs), jnp.int32),
                        pltpu.VMEM((nd, bs, LANES), jnp.int32),
                        pltpu.VMEM((nd, bs, LANES), F32)],
        compiler_params=_params("parallel"),
        name="dsa_select",
    )(idx3, idx3, idx3)


def _softmax_mixers(dsa3, idx3, dil3, tbias):
    return _attention(dsa3, dil3, tbias, _dsa_select(idx3))


def _route(logits):
    lane = lax.broadcasted_iota(jnp.int32, logits.shape, 1).astype(F32)
    far = float(LANES)

    def first_argmax(vals):
        top = jnp.max(vals, axis=1, keepdims=True)
        return top, jnp.min(jnp.where(vals == top, lane, far), axis=1, keepdims=True)

    is_group = lane < MOE_GROUPS
    g_top, g_sel = first_argmax(jnp.where(is_group, logits, NEG))
    g_w = 1.0 / jnp.sum(jnp.where(is_group, jnp.exp(logits - g_top), 0.0), axis=1, keepdims=True)
    lo = MOE_GROUPS + MOE_EXPERTS_PER_GROUP * g_sel
    e_logits = jnp.where(lane >= lo, jnp.where(lane < lo + MOE_EXPERTS_PER_GROUP, logits, NEG), NEG)
    v1, i1 = first_argmax(e_logits)
    v2, i2 = first_argmax(jnp.where(lane == i1, NEG, e_logits))
    e2 = jnp.exp(v2 - v1)
    w1 = 1.0 / (1.0 + e2)
    return jnp.where(lane == i1, w1 * g_w, jnp.where(lane == i2, e2 * w1 * g_w, 0.0))


def _merge_kernel(h_ref, g1_ref, oa_ref, ob_ref, oc_ref, od_ref, wg_ref, bg_ref,
                  wba_ref, wbb_ref, wbc_ref, wbd_ref, wo_ref, g2_ref, wr_ref, br_ref,
                  h1_ref, xn2_ref, gates_ref):
    h = h_ref[...]
    xn = _rms(h, g1_ref[...]).astype(BF16)
    merged = None
    for gi, (o_ref, wb_ref) in enumerate(((oa_ref, wba_ref), (ob_ref, wbb_ref),
                                          (oc_ref, wbc_ref), (od_ref, wbd_ref))):
        gate = _sigmoid(jnp.dot(xn, wg_ref[gi], preferred_element_type=F32) + bg_ref[gi:gi + 1, :])
        term = gate * jnp.dot(o_ref[...], wb_ref[...], preferred_element_type=F32)
        merged = term if merged is None else merged + term
    h1 = h + jnp.dot(merged.astype(BF16), wo_ref[...], preferred_element_type=F32)
    h1_ref[...] = h1
    xn2 = _rms(h1, g2_ref[...]).astype(BF16)
    xn2_ref[...] = xn2
    logits = jnp.dot(xn2, wr_ref[...], preferred_element_type=F32) + br_ref[...]
    gates_ref[...] = _route(logits)


def _merge(h, g1, oa, ob, oc, od, wg, bg, wba, wbb, wbc, wbd, wo, g2, wr, br):
    n = h.shape[0]
    tm = min(PROJ_ROWS, n)
    rows = lambda w: pl.BlockSpec((tm, w), lambda i: (i, 0))
    return pl.pallas_call(
        _merge_kernel,
        grid=(n // tm,),
        in_specs=[rows(D_MODEL), _resident(g1.shape),
                  rows(oa.shape[1]), rows(ob.shape[1]), rows(oc.shape[1]), rows(od.shape[1]),
                  _resident(wg.shape), _resident(bg.shape),
                  _resident(wba.shape), _resident(wbb.shape), _resident(wbc.shape),
                  _resident(wbd.shape), _resident(wo.shape), _resident(g2.shape),
                  _resident(wr.shape), _resident(br.shape)],
        out_specs=[rows(D_MODEL), rows(D_MODEL), rows(LANES)],
        out_shape=[jax.ShapeDtypeStruct((n, D_MODEL), F32),
                   jax.ShapeDtypeStruct((n, D_MODEL), BF16),
                   jax.ShapeDtypeStruct((n, LANES), F32)],
        compiler_params=_params("parallel"),
        name="merge",
    )(h, g1, oa, ob, oc, od, wg, bg, wba, wbb, wbc, wbd, wo, g2, wr, br)


def _moe_kernel(x_ref, gates_ref, h1_ref, wg_ref, wu_ref, wd_ref, gf_ref, o_ref, acc_ref,
                *, final_norm):
    e = pl.program_id(1)

    @pl.when(e == 0)
    def _():
        acc_ref[...] = h1_ref[...]

    x = x_ref[...]
    hg = jnp.dot(x, wg_ref[...], preferred_element_type=F32)
    hu = jnp.dot(x, wu_ref[...], preferred_element_type=F32)
    hid = (hg * _sigmoid(hg) * hu).astype(BF16)
    y = jnp.dot(hid, wd_ref[...], preferred_element_type=F32)
    lane = lax.broadcasted_iota(jnp.int32, gates_ref.shape, 1)
    gate = jnp.sum(jnp.where(lane == e + MOE_GROUPS, gates_ref[...], 0.0), axis=1, keepdims=True)
    acc_ref[...] += gate * y

    @pl.when(e == N_EXPERTS - 1)
    def _():
        out = acc_ref[...]
        o_ref[...] = _rms(out, gf_ref[...]) if final_norm else out


def _moe(xn2, gates, h1, wg, wu, wd, gf, final_norm):
    n = xn2.shape[0]
    tm = min(MOE_ROWS, n)
    rows = lambda w: pl.BlockSpec((tm, w), lambda i, e: (i, 0))
    return pl.pallas_call(
        functools.partial(_moe_kernel, final_norm=final_norm),
        grid=(n // tm, N_EXPERTS),
        in_specs=[rows(D_MODEL), rows(LANES), rows(D_MODEL),
                  pl.BlockSpec((None, D_MODEL, MOE_HIDDEN), lambda i, e: (e, 0, 0)),
                  pl.BlockSpec((None, D_MODEL, MOE_HIDDEN), lambda i, e: (e, 0, 0)),
                  pl.BlockSpec((None, MOE_HIDDEN, D_MODEL), lambda i, e: (e, 0, 0)),
                  pl.BlockSpec((1, D_MODEL), lambda i, e: (0, 0))],
        out_specs=rows(D_MODEL),
        out_shape=jax.ShapeDtypeStruct((n, D_MODEL), F32),
        scratch_shapes=[pltpu.VMEM((tm, D_MODEL), F32)],
        compiler_params=_params("parallel", "arbitrary"),
        name="moe",
    )(xn2, gates, h1, wg, wu, wd, gf)


def _inproj_columns():
    offs = np.concatenate([[0], np.cumsum(IN_SPLITS)])
    (aq, ak, av, ag, alr, bq, bk, bv, iq, ik, iw, cq, ck, cv, dy, dx) = offs[:-1]
    cols = []

    def per_head(base, width):
        for h in range(A_HEADS):
            cols.extend(list(range(base + h * width, base + (h + 1) * width)) + [-1] * (LANES - width))

    per_head(aq, A_DK)
    per_head(ak, A_DK)
    per_head(av, A_DV)
    per_head(ag, A_DV)
    cols.extend(list(range(alr, alr + A_LOWRANK)) + [-1] * (LANES - A_LOWRANK))
    cols.extend(range(bq, bq + ATT_W))
    cols.extend(range(iq, iq + IDX_HEADS * IDX_DIM))
    cols.extend(list(range(ik, ik + IDX_DIM)) * IDX_HEADS)
    cols.extend(list(range(iw, iw + IDX_HEADS)) + [-1] * (LANES - IDX_HEADS))
    cols.extend(range(cq, cq + ATT_W))
    cols.extend(range(dy, dy + LRU_W))
    cols = np.asarray(cols, np.int32)
    assert cols.shape[0] == sum(GROUP_WIDTHS)
    return cols


def _pad_heads(a, width, axis):
    a = jnp.moveaxis(a, axis, -1)
    a = a.reshape(a.shape[:-1] + (A_HEADS, width))
    a = jnp.pad(a, [(0, 0)] * (a.ndim - 1) + [(0, LANES - width)])
    return jnp.moveaxis(a.reshape(a.shape[:-2] + (A_HEADS * LANES,)), -1, axis)


def _block_diag(w):
    nb, bw, _ = w.shape
    eye = jnp.eye(nb, dtype=w.dtype)
    return jnp.einsum('ncd,nm->ncmd', w, eye).reshape(nb * bw, nb * bw)


def kernel(x, w_in, a_w2, a_b2, a_gain, conv_w, conv_b, lru_wr, lru_br, lru_wi, lru_bi, lru_lambda, w_gate, b_gate, w_branch, w_out, rel_bias, norm1, norm2, norm_f, moe_wrg, moe_brg, moe_wre, moe_bre, moe_wg, moe_wu, moe_wd):
    bsz, t, _ = x.shape
    n = bsz * t
    depth = w_in.shape[0]
    cols = _inproj_columns()
    tbias = _tbias(rel_bias, t)
    h = x.reshape(n, D_MODEL)
    for l in range(depth):
        w_all = jnp.where(cols[None, :] >= 0, w_in[l][:, np.maximum(cols, 0)], 0.0).astype(BF16)
        gla_in, dsa_in, idx_in, dil_in, lru_in = _inproj(h, norm1[l][None, :], w_all)

        wa2 = jnp.pad(_pad_heads(a_w2[l], A_DK, 1), ((0, LANES - A_LOWRANK), (0, 0)))
        o_a = _gla(gla_in.reshape(bsz, t, GLA_W), wa2, _pad_heads(a_b2[l], A_DK, 0)[None, :],
                   _pad_heads(a_gain[l], A_DV, 0)[None, :])
        o_b, o_c = _softmax_mixers(dsa_in.reshape(bsz, t, ATT_W), idx_in.reshape(bsz, t, IDX_W),
                                   dil_in.reshape(bsz, t, ATT_W), tbias)
        w_ri = jnp.concatenate([_block_diag(lru_wr[l]), _block_diag(lru_wi[l])], axis=1).astype(BF16)
        b_ri = jnp.concatenate([lru_br[l], lru_bi[l]])[None, :]
        o_d = _rglru(lru_in.reshape(bsz, t, LRU_W), conv_w[l], conv_b[l][None, :], w_ri, b_ri,
                     lru_lambda[l][None, :])

        w_router = jnp.pad(jnp.concatenate([moe_wrg[l], moe_wre[l]], axis=1),
                           ((0, 0), (0, LANES - MOE_GROUPS - N_EXPERTS)))
        b_router = jnp.pad(jnp.concatenate([moe_brg[l], moe_bre[l]]),
                           (0, LANES - MOE_GROUPS - N_EXPERTS))[None, :]
        h1, xn2, gates = _merge(
            h, norm1[l][None, :], o_a.reshape(n, -1), o_b.reshape(n, -1), o_c.reshape(n, -1),
            o_d.reshape(n, -1), w_gate[l].astype(BF16), b_gate[l],
            _pad_heads(w_branch[l, 0], A_DV, 0).astype(BF16), w_branch[l, 1].astype(BF16),
            w_branch[l, 2].astype(BF16), w_branch[l, 3].astype(BF16), w_out[l].astype(BF16),
            norm2[l][None, :], w_router.astype(BF16), b_router)
        h = _moe(xn2, gates, h1, moe_wg[l].astype(BF16), moe_wu[l].astype(BF16),
                 moe_wd[l].astype(BF16), norm_f[None, :], final_norm=(l == depth - 1))
    return h.reshape(bsz, t, D_MODEL)
```

```python
import functools
import math

import numpy as np
import jax
import jax.numpy as jnp
from jax import lax
from jax.experimental import pallas as pl
from jax.experimental.pallas import tpu as pltpu

F32 = jnp.float32
BF16 = jnp.bfloat16
HIGHEST = lax.Precision.HIGHEST

D_MODEL = 1024
EPS = 1e-6
A_HEADS, A_DK, A_DV, A_LOWRANK, A_GATE_TAU, A_CHUNK = 4, 48, 96, 16, 16.0, 64
B_HEADS, B_HEAD_DIM, IDX_HEADS, IDX_DIM, B_TOPK_MAX = 6, 64, 4, 32, 256
C_HEADS, C_HEAD_DIM = 6, 64
C_PATTERNS = ((128, 1), (512, 4), (2048, 16))
D_WIDTH, D_BLOCKS, D_CONV, D_C = 384, 8, 4, 8.0
REL_BUCKETS, REL_MAX_DIST = 32, 2048
N_SOFTMAX_HEADS = B_HEADS + C_HEADS
MOE_GROUPS, MOE_EXPERTS_PER_GROUP, MOE_HIDDEN = 4, 4, 512
N_EXPERTS = MOE_GROUPS * MOE_EXPERTS_PER_GROUP
IN_SPLITS = (A_HEADS * A_DK, A_HEADS * A_DK, A_HEADS * A_DV, A_HEADS * A_DV, A_LOWRANK,
             B_HEADS * B_HEAD_DIM, B_HEADS * B_HEAD_DIM, B_HEADS * B_HEAD_DIM,
             IDX_HEADS * IDX_DIM, IDX_DIM, IDX_HEADS,
             C_HEADS * C_HEAD_DIM, C_HEADS * C_HEAD_DIM, C_HEADS * C_HEAD_DIM,
             D_WIDTH, D_WIDTH)

LANES = 128
VMEM_LIMIT = 56 * 1024 * 1024
NEG = -1e30
INT_MIN = -2 ** 31
LOG2E = math.log2(math.e)

ATT_BLOCK = 256
SEQ_BLOCK = 256
PROJ_ROWS = 512
MOE_ROWS = 1024
MOE_WINDOW = 512
MOE_ALIGN = 128

GLA_W = 4 * A_HEADS * LANES + LANES
ATT_W = 3 * B_HEADS * B_HEAD_DIM
IDX_W = 3 * LANES
LRU_W = 2 * D_WIDTH
GROUP_WIDTHS = (GLA_W, ATT_W, IDX_W, ATT_W, LRU_W)
NT_DIMS = (((1,), (1,)), ((), ()))
TN_DIMS = (((0,), (0,)), ((), ()))


def _params(*sem):
    return pltpu.CompilerParams(dimension_semantics=sem, vmem_limit_bytes=VMEM_LIMIT)


def _resident(shape):
    zeros = (0,) * len(shape)
    return pl.BlockSpec(shape, lambda *_: zeros, pipeline_mode=pl.Buffered(1))


def _rms(x, gain):
    return x * lax.rsqrt(jnp.mean(x * x, axis=-1, keepdims=True) + EPS) * gain


def _sigmoid(x):
    return 1.0 / (1.0 + jnp.exp(-x))


def _softplus(x):
    return jnp.maximum(x, 0.0) + jnp.log1p(jnp.exp(-jnp.abs(x)))


def _inproj_kernel(x_ref, g_ref, w_ref, *out_refs):
    xn = _rms(x_ref[...], g_ref[...]).astype(BF16)
    off = 0
    for o_ref in out_refs:
        width = o_ref.shape[-1]
        o_ref[...] = jnp.dot(xn, w_ref[:, off:off + width],
                             preferred_element_type=F32).astype(o_ref.dtype)
        off += width


def _inproj(h, gain, w_all):
    n = h.shape[0]
    tm = min(PROJ_ROWS, n)
    total = sum(GROUP_WIDTHS)
    return pl.pallas_call(
        _inproj_kernel,
        grid=(n // tm,),
        in_specs=[pl.BlockSpec((tm, D_MODEL), lambda i: (i, 0)),
                  _resident((1, D_MODEL)),
                  _resident((D_MODEL, total))],
        out_specs=[pl.BlockSpec((tm, w), lambda i: (i, 0)) for w in GROUP_WIDTHS],
        out_shape=[jax.ShapeDtypeStruct((n, w), BF16) for w in GROUP_WIDTHS],
        compiler_params=_params("parallel"),
        name="inproj",
    )(h, gain, w_all)


def _gla_kernel(x_ref, wa2_ref, ba2_ref, gain_ref, o_ref, st_ref, *, tb):
    @pl.when(pl.program_id(1) == 0)
    def _():
        st_ref[...] = jnp.zeros_like(st_ref)

    hw = A_HEADS * LANES
    x = x_ref[...]
    z = jnp.dot(x[:, 4 * hw:].astype(F32), wa2_ref[...], precision=HIGHEST,
                preferred_element_type=F32) + ba2_ref[...]
    log_a = -_softplus(-z) * (1.0 / A_GATE_TAU)

    r64 = lax.broadcasted_iota(jnp.int32, (A_CHUNK, A_CHUNK), 0)
    c64 = lax.broadcasted_iota(jnp.int32, (A_CHUNK, A_CHUNK), 1)
    causal = c64 <= r64
    tri = jnp.where(causal, 1.0, 0.0).astype(BF16)
    la_hi = log_a.astype(BF16)
    la_lo = (log_a - la_hi.astype(F32)).astype(BF16)
    b_parts, b_last_parts = [], []
    for c in range(tb // A_CHUNK):
        rs = slice(c * A_CHUNK, (c + 1) * A_CHUNK)
        bc = (jnp.dot(tri, la_hi[rs], preferred_element_type=F32)
              + jnp.dot(tri, la_lo[rs], preferred_element_type=F32))
        b_parts.append(bc)
        b_last_parts.append(jnp.broadcast_to(bc[A_CHUNK - 1:A_CHUNK, :], bc.shape))
    b = jnp.concatenate(b_parts, axis=0)
    b_last = jnp.concatenate(b_last_parts, axis=0)

    q = x[:, 0:hw].astype(F32) * (A_DK ** -0.5)
    k = x[:, hw:2 * hw].astype(F32)
    v = x[:, 2 * hw:3 * hw]
    q_dec = (q * jnp.exp(b)).astype(BF16)
    k_dec = (k * jnp.exp(-b)).astype(BF16)
    k_end = (k * jnp.exp(b_last - b)).astype(BF16)
    decay = jnp.exp(b_last)

    heads = []
    for h in range(A_HEADS):
        cs = slice(h * LANES, (h + 1) * LANES)
        state_t = st_ref[h]
        chunks = []
        for c in range(tb // A_CHUNK):
            rs = slice(c * A_CHUNK, (c + 1) * A_CHUNK)
            qc, kc, kec, vc = q_dec[rs, cs], k_dec[rs, cs], k_end[rs, cs], v[rs, cs]
            att = lax.dot_general(qc, kc, NT_DIMS, preferred_element_type=F32)
            att = jnp.where(causal, att, 0.0).astype(BF16)
            oc = jnp.dot(att, vc, preferred_element_type=F32)
            oc += lax.dot_general(qc, state_t.astype(BF16), NT_DIMS, preferred_element_type=F32)
            inc_t = lax.dot_general(vc, kec, TN_DIMS, preferred_element_type=F32)
            state_t = state_t * decay[c * A_CHUNK:c * A_CHUNK + 1, cs] + inc_t
            chunks.append(oc)
        st_ref[h] = state_t
        o_h = jnp.concatenate(chunks, axis=0)
        ms = jnp.sum(o_h * o_h, axis=-1, keepdims=True) * (1.0 / A_DV)
        heads.append(o_h * lax.rsqrt(ms + EPS))
    o = jnp.concatenate(heads, axis=1) * gain_ref[...]
    g = x[:, 3 * hw:4 * hw].astype(F32)
    o_ref[...] = (o * (g * _sigmoid(g))).astype(o_ref.dtype)


def _gla(x3, wa2, ba2, gain):
    bsz, t, _ = x3.shape
    tb = min(SEQ_BLOCK, t)
    hw = A_HEADS * LANES
    return pl.pallas_call(
        functools.partial(_gla_kernel, tb=tb),
        grid=(bsz, t // tb),
        in_specs=[pl.BlockSpec((None, tb, GLA_W), lambda b, i: (b, i, 0)),
                  _resident((LANES, hw)), _resident((1, hw)), _resident((1, hw))],
        out_specs=pl.BlockSpec((None, tb, hw), lambda b, i: (b, i, 0)),
        out_shape=jax.ShapeDtypeStruct((bsz, t, hw), BF16),
        scratch_shapes=[pltpu.VMEM((A_HEADS, LANES, LANES), F32)],
        compiler_params=_params("parallel", "arbitrary"),
        name="gla",
    )(x3, wa2, ba2, gain)


def _rglru_kernel(d_ref, cw_ref, cb_ref, wri_ref, bri_ref, lam_ref, o_ref, xbuf, hc_ref, *, tb):
    halo = 8

    @pl.when(pl.program_id(1) == 0)
    def _():
        xbuf[0:halo, :] = jnp.zeros((halo, D_WIDTH), F32)
        hc_ref[...] = jnp.zeros_like(hc_ref)

    d = d_ref[...]
    y = d[:, 0:D_WIDTH].astype(F32)
    x = d[:, D_WIDTH:].astype(F32)
    xbuf[halo:halo + tb, :] = x
    xc = cb_ref[...]
    for j in range(D_CONV):
        xc = xc + cw_ref[j:j + 1, :] * xbuf[pl.ds(halo - (D_CONV - 1) + j, tb), :]
    xbuf[0:halo, :] = x[tb - halo:tb, :]

    ri = jnp.dot(xc.astype(BF16), wri_ref[...], preferred_element_type=F32) + bri_ref[...]
    r = _sigmoid(ri[:, 0:D_WIDTH])
    ig = _sigmoid(ri[:, D_WIDTH:])
    log_a = (-D_C) * r * _softplus(-lam_ref[...])
    a = jnp.exp(log_a)
    u = jnp.sqrt(-jnp.tanh(log_a) * (a * a + 1.0)) * (ig * xc)

    row = lax.broadcasted_iota(jnp.int32, (tb, D_WIDTH), 0)
    s = 1
    while s < tb:
        valid = row >= s
        u = jnp.where(valid, a * pltpu.roll(u, s, axis=0), 0.0) + u
        a = jnp.where(valid, a * pltpu.roll(a, s, axis=0), a)
        s *= 2
    hs = u + a * hc_ref[0:1, :]
    hc_ref[...] = jnp.broadcast_to(hs[tb - 1:tb, :], hc_ref.shape)
    gelu = 0.5 * y * (1.0 + jnp.tanh(math.sqrt(2.0 / math.pi) * (y + 0.044715 * (y * y * y))))
    o_ref[...] = (gelu * hs).astype(o_ref.dtype)


def _rglru(d3, conv_w, conv_b, w_ri, b_ri, lam):
    bsz, t, _ = d3.shape
    tb = min(SEQ_BLOCK, t)
    return pl.pallas_call(
        functools.partial(_rglru_kernel, tb=tb),
        grid=(bsz, t // tb),
        in_specs=[pl.BlockSpec((None, tb, LRU_W), lambda b, i: (b, i, 0)),
                  _resident((D_CONV, D_WIDTH)), _resident((1, D_WIDTH)),
                  _resident((D_WIDTH, 2 * D_WIDTH)), _resident((1, 2 * D_WIDTH)),
                  _resident((1, D_WIDTH))],
        out_specs=pl.BlockSpec((None, tb, D_WIDTH), lambda b, i: (b, i, 0)),
        out_shape=jax.ShapeDtypeStruct((bsz, t, D_WIDTH), BF16),
        scratch_shapes=[pltpu.VMEM((tb + 8, D_WIDTH), F32), pltpu.VMEM((8, D_WIDTH), F32)],
        compiler_params=_params("parallel", "arbitrary"),
        name="rglru",
    )(d3, conv_w, conv_b, w_ri, b_ri, lam)


def _bucket_edges():
    n = np.arange(REL_MAX_DIST + 1)
    max_exact = REL_BUCKETS // 2
    nf = np.maximum(n, 1).astype(np.float64)
    large = max_exact + (np.log(nf / max_exact) / math.log(REL_MAX_DIST / max_exact)
                         * (REL_BUCKETS - max_exact)).astype(np.int64)
    bucket = np.where(n < max_exact, n, np.minimum(large, REL_BUCKETS - 1))
    assert np.all(np.diff(bucket) >= 0)
    return [int(np.argmax(bucket >= b)) for b in range(REL_BUCKETS)]


def _tbias_kernel(tab_ref, o_ref, *, bs, edges):
    h = pl.program_id(0)
    d = pl.program_id(1)
    row = lax.broadcasted_iota(jnp.int32, (bs, bs), 0)
    col = lax.broadcasted_iota(jnp.int32, (bs, bs), 1)
    dist = d * bs + row - col
    val = jnp.full((bs, bs), tab_ref[0, h], F32)
    for b in range(1, REL_BUCKETS):
        val = jnp.where(dist >= edges[b], tab_ref[b, h], val)
    mult = jnp.zeros((bs, bs), jnp.int32)
    for window, dil in C_PATTERNS:
        hit = jnp.where((dist & (dil - 1)) == 0, 1, 0)
        mult = mult + jnp.where(dist <= window, hit, 0)
    logm = jnp.full((bs, bs), NEG, F32)
    for m in range(1, len(C_PATTERNS) + 1):
        logm = jnp.where(mult == m, math.log(m), logm)
    val = jnp.where(h >= B_HEADS, val + logm, val)
    o_ref[...] = jnp.where(dist >= 0, val * LOG2E, NEG)


def _tbias(rel_bias, t):
    bs = min(ATT_BLOCK, t)
    nd = t // bs
    for _, dil in C_PATTERNS:
        assert dil & (dil - 1) == 0
    return pl.pallas_call(
        functools.partial(_tbias_kernel, bs=bs, edges=_bucket_edges()),
        grid=(N_SOFTMAX_HEADS, nd),
        in_specs=[pl.BlockSpec(memory_space=pltpu.SMEM)],
        out_specs=pl.BlockSpec((None, None, bs, bs), lambda h, d: (h, d, 0, 0)),
        out_shape=jax.ShapeDtypeStruct((N_SOFTMAX_HEADS, nd, bs, bs), F32),
        compiler_params=_params("parallel", "parallel"),
        name="tbias",
    )(rel_bias)


def _attn_kernel(qb_ref, kb_ref, vb_ref, qc_ref, kc_ref, vc_ref, tb_ref, madd_ref,
                 ob_ref, oc_ref, vp_ref, m_ref, acc_ref, *, bs):
    i = pl.program_id(1)
    streams = ((qb_ref, kb_ref, vb_ref, ob_ref), (qc_ref, kc_ref, vc_ref, oc_ref))
    per_stream = B_HEADS
    n_heads = vp_ref.shape[0]
    half = LANES // 2
    assert B_HEAD_DIM == half and C_HEAD_DIM == half and B_HEADS == C_HEADS

    @pl.when(i == 0)
    def _():
        low_t = lax.broadcasted_iota(jnp.int32, (vb_ref.shape[0], LANES), 1) < half
        for h in range(0, n_heads, 2):
            v_ref = streams[h // per_stream][2]
            hp = (h % per_stream) // 2
            pair = v_ref[:, hp * LANES:(hp + 1) * LANES]
            vp_ref[h] = jnp.where(low_t, pair, 1.0).astype(BF16)
            vp_ref[h + 1] = jnp.where(low_t, 1.0, pair).astype(BF16)

    m_ref[...] = jnp.full(m_ref.shape, NEG, F32)
    acc_ref[...] = jnp.zeros(acc_ref.shape, F32)
    low = lax.broadcasted_iota(jnp.int32, (bs, LANES), 1) < half
    q_heads = []
    for h in range(n_heads):
        q_ref = streams[h // per_stream][0]
        hp = (h % per_stream) // 2
        qp = q_ref[:, hp * LANES:(hp + 1) * LANES].astype(F32) * (B_HEAD_DIM ** -0.5 * LOG2E)
        q_heads.append(jnp.where(low if h % 2 == 0 else jnp.logical_not(low), qp, 0.0).astype(BF16))

    def body(c, _):
        start = pl.multiple_of(c * bs, bs)
        extra = madd_ref[c].astype(F32)
        for h in range(n_heads):
            k_ref = streams[h // per_stream][1]
            hp = (h % per_stream) // 2
            ks = k_ref[pl.ds(start, bs), hp * LANES:(hp + 1) * LANES]
            s = lax.dot_general(q_heads[h], ks, NT_DIMS, preferred_element_type=F32)
            s = s + tb_ref[h, i - c]
            if h < per_stream:
                s = s + extra
            m_prev = m_ref[h]
            m_new = jnp.maximum(m_prev, jnp.max(s, axis=1, keepdims=True))
            alpha = jnp.exp2(m_prev - m_new)
            p = jnp.exp2((s - jnp.concatenate([m_new] * (bs // LANES), axis=1)).astype(BF16))
            acc_ref[h] = alpha * acc_ref[h] + jnp.dot(p, vp_ref[h, pl.ds(start, bs), :],
                                                      preferred_element_type=F32)
            m_ref[h] = m_new
        return 0

    lax.fori_loop(0, i + 1, body, 0)

    for h in range(0, n_heads, 2):
        o_ref = streams[h // per_stream][3]
        hp = (h % per_stream) // 2
        a0, a1 = acc_ref[h], acc_ref[h + 1]
        o0 = a0 / pltpu.roll(a0, half, axis=1)
        o1 = a1 / pltpu.roll(a1, half, axis=1)
        o_ref[:, hp * LANES:(hp + 1) * LANES] = jnp.where(low, o0, o1).astype(o_ref.dtype)


def _attention(dsa3, dil3, tbias, madd):
    bsz, t, width3 = dsa3.shape
    width = width3 // 3
    bs = min(ATT_BLOCK, t)
    nd = t // bs
    q_spec = pl.BlockSpec((None, bs, width), lambda b, i: (b, i, 0))
    k_spec = pl.BlockSpec((None, t, width), lambda b, i: (b, 0, 1), pipeline_mode=pl.Buffered(1))
    v_spec = pl.BlockSpec((None, t, width), lambda b, i: (b, 0, 2), pipeline_mode=pl.Buffered(1))
    out_spec = pl.BlockSpec((None, bs, width), lambda b, i: (b, i, 0))
    out_shape = jax.ShapeDtypeStruct((bsz, t, width), BF16)
    return pl.pallas_call(
        functools.partial(_attn_kernel, bs=bs),
        grid=(bsz, nd),
        in_specs=[q_spec, k_spec, v_spec, q_spec, k_spec, v_spec,
                  _resident(tbias.shape),
                  pl.BlockSpec((None, None, nd, bs, bs), lambda b, i: (b, i, 0, 0, 0))],
        out_specs=[out_spec, out_spec],
        out_shape=[out_shape, out_shape],
        scratch_shapes=[pltpu.VMEM((N_SOFTMAX_HEADS, t, LANES), BF16),
                        pltpu.VMEM((N_SOFTMAX_HEADS, bs, LANES), F32),
                        pltpu.VMEM((N_SOFTMAX_HEADS, bs, LANES), F32)],
        compiler_params=_params("arbitrary", "arbitrary"),
        name="attn",
    )(dsa3, dsa3, dsa3, dil3, dil3, dil3, tbias, madd)


def _dsa_select_kernel(iq_ref, ik_ref, iw_ref, madd_ref, key_ref, thr_ref, need_ref,
                       *, bs, nd, k_sel):
    pair = lambda i, c: i * (i + 1) // 2 + c
    lane = lax.broadcasted_iota(jnp.int32, (bs, LANES), 1)
    row = lax.broadcasted_iota(jnp.int32, (bs, bs), 0)
    col = lax.broadcasted_iota(jnp.int32, (bs, bs), 1)
    tiles = bs // LANES

    for i in range(nd):
        iq = iq_ref[i * bs:(i + 1) * bs, :]
        iw = iw_ref[i * bs:(i + 1) * bs, :].astype(F32) * (IDX_HEADS ** -0.5 * IDX_DIM ** -0.5)
        iq_heads = [jnp.where((lane // IDX_DIM) == hh, iq, 0.0).astype(BF16)
                    for hh in range(IDX_HEADS)]
        for c in range(i + 1):
            ikc = ik_ref[c * bs:(c + 1) * bs, :]
            sc = jnp.zeros((bs, bs), F32)
            for hh in range(IDX_HEADS):
                raw = lax.dot_general(iq_heads[hh], ikc, NT_DIMS, preferred_element_type=F32)
                sc = sc + iw[:, hh:hh + 1] * jnp.maximum(raw, 0.0)
            sc = jnp.where(sc == 0.0, 0.0, sc)
            bits = pltpu.bitcast(sc, jnp.int32)
            key = jnp.where(bits < 0, bits ^ 0x7FFFFFFF, bits)
            if c == i:
                key = jnp.where(col <= row, key, INT_MIN)
            key_ref[pair(i, c)] = key

    thr_ref[...] = jnp.full(thr_ref.shape, INT_MIN, jnp.int32)
    kf = float(k_sel)
    n_bits = 32

    def bisect(it, _):
        searching = it < n_bits
        step = jnp.where(searching, lax.shift_left(jnp.int32(1), jnp.maximum(n_bits - 1 - it, 0)), 1)
        for i in range(nd):
            thr = thr_ref[i]
            cand = thr + step
            acc = jnp.zeros((bs, LANES), F32)
            for c in range(i + 1):
                kk = key_ref[pair(i, c)]
                for j in range(tiles):
                    acc = acc + jnp.where(kk[:, j * LANES:(j + 1) * LANES] >= cand, 1.0, 0.0)
            cnt = jnp.sum(acc, axis=1, keepdims=True)
            thr_ref[i] = jnp.where(jnp.logical_and(searching, cnt >= kf), cand, thr)
            need_ref[i] = jnp.broadcast_to(kf - cnt, (bs, LANES))
        return 0

    lax.fori_loop(0, n_bits + 1, bisect, 0)

    prefix = jnp.where(row <= col, 1.0, 0.0).astype(BF16)
    for i in range(nd):
        thr2 = jnp.concatenate([thr_ref[i]] * tiles, axis=1)
        need = need_ref[i][:, 0:1]
        seen = jnp.zeros((bs, 1), F32)
        for c in range(i + 1):
            kk = key_ref[pair(i, c)]
            tie = kk == thr2
            rank = seen + jnp.dot(jnp.where(tie, 1.0, 0.0).astype(BF16), prefix,
                                  preferred_element_type=F32)
            tie_add = jnp.where(tie, jnp.where(rank <= need, 0.0, NEG), NEG)
            madd_ref[i, c] = jnp.where(kk > thr2, 0.0, tie_add).astype(madd_ref.dtype)
            seen = rank[:, bs - 1:bs]
        for c in range(i + 1, nd):
            madd_ref[i, c] = jnp.full((bs, bs), NEG, madd_ref.dtype)


def _dsa_select(idx3):
    bsz, t, _ = idx3.shape
    bs = min(ATT_BLOCK, t)
    nd = t // bs
    k_sel = min(B_TOPK_MAX, t // 4)
    col_block = lambda j: pl.BlockSpec((None, t, LANES), lambda b: (b, 0, j))
    return pl.pallas_call(
        functools.partial(_dsa_select_kernel, bs=bs, nd=nd, k_sel=k_sel),
        grid=(bsz,),
        in_specs=[col_block(0), col_block(1), col_block(2)],
        out_specs=pl.BlockSpec((None, nd, nd, bs, bs), lambda b: (b, 0, 0, 0, 0)),
        out_shape=jax.ShapeDtypeStruct((bsz, nd, nd, bs, bs), BF16),
        scratch_shapes=[pltpu.VMEM((nd * (nd + 1) // 2, bs, bs), jnp.int32),
                        pltpu.VMEM((nd, bs, LANES), jnp.int32),
                        pltpu.VMEM((nd, bs, LANES), F32)],
        compiler_params=_params("parallel"),
        name="dsa_select",
    )(idx3, idx3, idx3)


def _softmax_mixers(dsa3, idx3, dil3, tbias):
    return _attention(dsa3, dil3, tbias, _dsa_select(idx3))


def _route(logits):
    lane = lax.broadcasted_iota(jnp.int32, logits.shape, 1).astype(F32)
    far = float(LANES)

    def first_argmax(vals):
        top = jnp.max(vals, axis=1, keepdims=True)
        return top, jnp.min(jnp.where(vals == top, lane, far), axis=1, keepdims=True)

    is_group = lane < MOE_GROUPS
    g_top, g_sel = first_argmax(jnp.where(is_group, logits, NEG))
    g_w = 1.0 / jnp.sum(jnp.where(is_group, jnp.exp(logits - g_top), 0.0), axis=1, keepdims=True)
    lo = MOE_GROUPS + MOE_EXPERTS_PER_GROUP * g_sel
    e_logits = jnp.where(lane >= lo, jnp.where(lane < lo + MOE_EXPERTS_PER_GROUP, logits, NEG), NEG)
    v1, i1 = first_argmax(e_logits)
    v2, i2 = first_argmax(jnp.where(lane == i1, NEG, e_logits))
    e2 = jnp.exp(v2 - v1)
    w1 = 1.0 / (1.0 + e2)
    return jnp.where(lane == i1, w1 * g_w, jnp.where(lane == i2, e2 * w1 * g_w, 0.0))


def _merge_kernel(h_ref, g1_ref, oa_ref, ob_ref, oc_ref, od_ref, wg_ref, bg_ref,
                  wba_ref, wbb_ref, wbc_ref, wbd_ref, wo_ref, g2_ref, wr_ref, br_ref,
                  h1_ref, xn2_ref, gates_ref):
    h = h_ref[...]
    xn = _rms(h, g1_ref[...]).astype(BF16)
    merged = None
    for gi, (o_ref, wb_ref) in enumerate(((oa_ref, wba_ref), (ob_ref, wbb_ref),
                                          (oc_ref, wbc_ref), (od_ref, wbd_ref))):
        gate = _sigmoid(jnp.dot(xn, wg_ref[gi], preferred_element_type=F32) + bg_ref[gi:gi + 1, :])
        term = gate * jnp.dot(o_ref[...], wb_ref[...], preferred_element_type=F32)
        merged = term if merged is None else merged + term
    h1 = h + jnp.dot(merged.astype(BF16), wo_ref[...], preferred_element_type=F32)
    h1_ref[...] = h1
    xn2 = _rms(h1, g2_ref[...]).astype(BF16)
    xn2_ref[...] = xn2
    logits = jnp.dot(xn2, wr_ref[...], preferred_element_type=F32) + br_ref[...]
    gates_ref[...] = _route(logits)


def _merge(h, g1, oa, ob, oc, od, wg, bg, wba, wbb, wbc, wbd, wo, g2, wr, br):
    n = h.shape[0]
    tm = min(PROJ_ROWS, n)
    rows = lambda w: pl.BlockSpec((tm, w), lambda i: (i, 0))
    return pl.pallas_call(
        _merge_kernel,
        grid=(n // tm,),
        in_specs=[rows(D_MODEL), _resident(g1.shape),
                  rows(oa.shape[1]), rows(ob.shape[1]), rows(oc.shape[1]), rows(od.shape[1]),
                  _resident(wg.shape), _resident(bg.shape),
                  _resident(wba.shape), _resident(wbb.shape), _resident(wbc.shape),
                  _resident(wbd.shape), _resident(wo.shape), _resident(g2.shape),
                  _resident(wr.shape), _resident(br.shape)],
        out_specs=[rows(D_MODEL), rows(D_MODEL), rows(LANES)],
        out_shape=[jax.ShapeDtypeStruct((n, D_MODEL), F32),
                   jax.ShapeDtypeStruct((n, D_MODEL), BF16),
                   jax.ShapeDtypeStruct((n, LANES), F32)],
        compiler_params=_params("parallel"),
        name="merge",
    )(h, g1, oa, ob, oc, od, wg, bg, wba, wbb, wbc, wbd, wo, g2, wr, br)


def _moe_kernel(x_ref, gates_ref, h1_ref, wg_ref, wu_ref, wd_ref, gf_ref, o_ref,
                xs_ref, ys_ref, gs_ref, pos_ref, win_ref, *, final_norm, tm):
    e = pl.program_id(1)
    grp = e // MOE_EXPERTS_PER_GROUP
    row_sq = lax.broadcasted_iota(jnp.int32, (tm, tm), 0)
    col_sq = lax.broadcasted_iota(jnp.int32, (tm, tm), 1)

    @pl.when(e == 0)
    def _():
        gates = gates_ref[...]
        lane = lax.broadcasted_iota(jnp.int32, (tm, LANES), 1)
        member = jnp.zeros((tm, LANES), F32)
        for gg in range(MOE_GROUPS):
            lo = MOE_GROUPS + gg * MOE_EXPERTS_PER_GROUP
            in_g = jnp.where(lane >= lo, jnp.where(lane < lo + MOE_EXPERTS_PER_GROUP, gates, 0.0), 0.0)
            total = jnp.sum(in_g, axis=1, keepdims=True)
            member = jnp.where(lane == gg, jnp.where(total > 0.0, 1.0, 0.0), member)
        before = jnp.dot(jnp.where(col_sq < row_sq, 1.0, 0.0).astype(BF16), member.astype(BF16),
                         preferred_element_type=F32)
        count = before[tm - 1:tm, :] + member[tm - 1:tm, :]
        r128 = lax.broadcasted_iota(jnp.int32, (LANES, LANES), 0)
        c128 = lax.broadcasted_iota(jnp.int32, (LANES, LANES), 1)
        first = jnp.dot(jnp.broadcast_to(count, (8, LANES)), jnp.where(r128 < c128, 1.0, 0.0),
                        precision=HIGHEST, preferred_element_type=F32)[0:1, :]
        slot = member * (first + before)
        pos_ref[...] = jnp.broadcast_to(jnp.sum(slot, axis=1, keepdims=True), (tm, LANES))
        pos_t = lax.dot_general(jnp.ones((8, LANES), F32), slot, NT_DIMS, precision=HIGHEST,
                                preferred_element_type=F32)[0:1, :]
        lane_row = lax.broadcasted_iota(jnp.int32, (1, LANES), 1)
        for gg in range(MOE_GROUPS):
            seg_start = jnp.sum(jnp.where(lane_row == gg, first, 0.0)).astype(jnp.int32)
            seg_rows = jnp.sum(jnp.where(lane_row == gg, count, 0.0)).astype(jnp.int32)
            w0 = (seg_start // MOE_ALIGN) * MOE_ALIGN
            win_ref[gg] = w0
            win_ref[MOE_GROUPS + gg] = (seg_start + seg_rows - w0 + MOE_WINDOW - 1) // MOE_WINDOW
        perm = jnp.where(pos_t == row_sq.astype(F32), 1.0, 0.0).astype(BF16)
        xs_ref[0:tm, :] = jnp.dot(perm, x_ref[...], preferred_element_type=F32).astype(BF16)
        xs_ref[tm:, :] = jnp.zeros((xs_ref.shape[0] - tm, D_MODEL), BF16)
        g_hi = gates.astype(BF16)
        g_lo = (gates - g_hi.astype(F32)).astype(BF16)
        gs_ref[0:tm, :] = (jnp.dot(perm, g_hi, preferred_element_type=F32)
                           + jnp.dot(perm, g_lo, preferred_element_type=F32))
        gs_ref[tm:, :] = jnp.zeros((gs_ref.shape[0] - tm, LANES), F32)
        ys_ref[...] = jnp.zeros(ys_ref.shape, F32)

    lane_w = lax.broadcasted_iota(jnp.int32, (MOE_WINDOW, LANES), 1)

    def window(k, _):
        start = pl.multiple_of(win_ref[grp] + k * MOE_WINDOW, MOE_ALIGN)
        xw = xs_ref[pl.ds(start, MOE_WINDOW), :]
        hg = jnp.dot(xw, wg_ref[...], preferred_element_type=F32)
        hu = jnp.dot(xw, wu_ref[...], preferred_element_type=F32)
        hid = (hg * _sigmoid(hg) * hu).astype(BF16)
        y = jnp.dot(hid, wd_ref[...], preferred_element_type=F32)
        gate = jnp.sum(jnp.where(lane_w == e + MOE_GROUPS, gs_ref[pl.ds(start, MOE_WINDOW), :], 0.0),
                       axis=1, keepdims=True)
        ys_ref[pl.ds(start, MOE_WINDOW), :] += gate * y
        return 0

    lax.fori_loop(0, win_ref[MOE_GROUPS + grp], window, 0)

    @pl.when(e == N_EXPERTS - 1)
    def _():
        pos_sq = jnp.concatenate([pos_ref[...]] * (tm // LANES), axis=1)
        unperm = jnp.where(pos_sq == col_sq.astype(F32), 1.0, 0.0).astype(BF16)
        ys = ys_ref[0:tm, :]
        y_hi = ys.astype(BF16)
        y_lo = (ys - y_hi.astype(F32)).astype(BF16)
        out = h1_ref[...] + (jnp.dot(unperm, y_hi, preferred_element_type=F32)
                             + jnp.dot(unperm, y_lo, preferred_element_type=F32))
        o_ref[...] = _rms(out, gf_ref[...]) if final_norm else out


def _moe(xn2, gates, h1, wg, wu, wd, gf, final_norm):
    n = xn2.shape[0]
    tm = min(MOE_ROWS, n)
    rows = lambda w: pl.BlockSpec((tm, w), lambda i, e: (i, 0))
    return pl.pallas_call(
        functools.partial(_moe_kernel, final_norm=final_norm, tm=tm),
        grid=(n // tm, N_EXPERTS),
        in_specs=[rows(D_MODEL), rows(LANES), rows(D_MODEL),
                  pl.BlockSpec((None, D_MODEL, MOE_HIDDEN), lambda i, e: (e, 0, 0)),
                  pl.BlockSpec((None, D_MODEL, MOE_HIDDEN), lambda i, e: (e, 0, 0)),
                  pl.BlockSpec((None, MOE_HIDDEN, D_MODEL), lambda i, e: (e, 0, 0)),
                  pl.BlockSpec((1, D_MODEL), lambda i, e: (0, 0))],
        out_specs=rows(D_MODEL),
        out_shape=jax.ShapeDtypeStruct((n, D_MODEL), F32),
        scratch_shapes=[pltpu.VMEM((tm + MOE_WINDOW, D_MODEL), BF16),
                        pltpu.VMEM((tm + MOE_WINDOW, D_MODEL), F32),
                        pltpu.VMEM((tm + MOE_WINDOW, LANES), F32),
                        pltpu.VMEM((tm, LANES), F32),
                        pltpu.SMEM((2 * MOE_GROUPS,), jnp.int32)],
        compiler_params=_params("parallel", "arbitrary"),
        name="moe",
    )(xn2, gates, h1, wg, wu, wd, gf)


def _inproj_columns():
    offs = np.concatenate([[0], np.cumsum(IN_SPLITS)])
    (aq, ak, av, ag, alr, bq, bk, bv, iq, ik, iw, cq, ck, cv, dy, dx) = offs[:-1]
    cols = []

    def per_head(base, width):
        for h in range(A_HEADS):
            cols.extend(list(range(base + h * width, base + (h + 1) * width)) + [-1] * (LANES - width))

    per_head(aq, A_DK)
    per_head(ak, A_DK)
    per_head(av, A_DV)
    per_head(ag, A_DV)
    cols.extend(list(range(alr, alr + A_LOWRANK)) + [-1] * (LANES - A_LOWRANK))
    cols.extend(range(bq, bq + ATT_W))
    cols.extend(range(iq, iq + IDX_HEADS * IDX_DIM))
    cols.extend(list(range(ik, ik + IDX_DIM)) * IDX_HEADS)
    cols.extend(list(range(iw, iw + IDX_HEADS)) + [-1] * (LANES - IDX_HEADS))
    cols.extend(range(cq, cq + ATT_W))
    cols.extend(range(dy, dy + LRU_W))
    cols = np.asarray(cols, np.int32)
    assert cols.shape[0] == sum(GROUP_WIDTHS)
    return cols


def _pad_heads(a, width, axis):
    a = jnp.moveaxis(a, axis, -1)
    a = a.reshape(a.shape[:-1] + (A_HEADS, width))
    a = jnp.pad(a, [(0, 0)] * (a.ndim - 1) + [(0, LANES - width)])
    return jnp.moveaxis(a.reshape(a.shape[:-2] + (A_HEADS * LANES,)), -1, axis)


def _block_diag(w):
    nb, bw, _ = w.shape
    eye = jnp.eye(nb, dtype=w.dtype)
    return jnp.einsum('ncd,nm->ncmd', w, eye).reshape(nb * bw, nb * bw)


def kernel(x, w_in, a_w2, a_b2, a_gain, conv_w, conv_b, lru_wr, lru_br, lru_wi, lru_bi, lru_lambda, w_gate, b_gate, w_branch, w_out, rel_bias, norm1, norm2, norm_f, moe_wrg, moe_brg, moe_wre, moe_bre, moe_wg, moe_wu, moe_wd):
    bsz, t, _ = x.shape
    n = bsz * t
    depth = w_in.shape[0]
    cols = _inproj_columns()
    tbias = _tbias(rel_bias, t)
    h = x.reshape(n, D_MODEL)
    for l in range(depth):
        w_all = jnp.where(cols[None, :] >= 0, w_in[l][:, np.maximum(cols, 0)], 0.0).astype(BF16)
        gla_in, dsa_in, idx_in, dil_in, lru_in = _inproj(h, norm1[l][None, :], w_all)

        wa2 = jnp.pad(_pad_heads(a_w2[l], A_DK, 1), ((0, LANES - A_LOWRANK), (0, 0)))
        o_a = _gla(gla_in.reshape(bsz, t, GLA_W), wa2, _pad_heads(a_b2[l], A_DK, 0)[None, :],
                   _pad_heads(a_gain[l], A_DV, 0)[None, :])
        o_b, o_c = _softmax_mixers(dsa_in.reshape(bsz, t, ATT_W), idx_in.reshape(bsz, t, IDX_W),
                                   dil_in.reshape(bsz, t, ATT_W), tbias)
        w_ri = jnp.concatenate([_block_diag(lru_wr[l]), _block_diag(lru_wi[l])], axis=1).astype(BF16)
        b_ri = jnp.concatenate([lru_br[l], lru_bi[l]])[None, :]
        o_d = _rglru(lru_in.reshape(bsz, t, LRU_W), conv_w[l], conv_b[l][None, :], w_ri, b_ri,
                     lru_lambda[l][None, :])

        w_router = jnp.pad(jnp.concatenate([moe_wrg[l], moe_wre[l]], axis=1),
                           ((0, 0), (0, LANES - MOE_GROUPS - N_EXPERTS)))
        b_router = jnp.pad(jnp.concatenate([moe_brg[l], moe_bre[l]]),
                           (0, LANES - MOE_GROUPS - N_EXPERTS))[None, :]
        h1, xn2, gates = _merge(
            h, norm1[l][None, :], o_a.reshape(n, -1), o_b.reshape(n, -1), o_c.reshape(n, -1),
            o_d.reshape(n, -1), w_gate[l].astype(BF16), b_gate[l],
            _pad_heads(w_branch[l, 0], A_DV, 0).astype(BF16), w_branch[l, 1].astype(BF16),
            w_branch[l, 2].astype(BF16), w_branch[l, 3].astype(BF16), w_out[l].astype(BF16),
            norm2[l][None, :], w_router.astype(BF16), b_router)
        h = _moe(xn2, gates, h1, moe_wg[l].astype(BF16), moe_wu[l].astype(BF16),
                 moe_wd[l].astype(BF16), norm_f[None, :], final_norm=(l == depth - 1))
    return h.reshape(bsz, t, D_MODEL)
```

```python
import functools
import math

import numpy as np
import jax
import jax.numpy as jnp
from jax import lax
from jax.experimental import pallas as pl
from jax.experimental.pallas import tpu as pltpu

F32 = jnp.float32
BF16 = jnp.bfloat16
HIGHEST = lax.Precision.HIGHEST

D_MODEL = 1024
EPS = 1e-6
A_HEADS, A_DK, A_DV, A_LOWRANK, A_GATE_TAU, A_CHUNK = 4, 48, 96, 16, 16.0, 64
B_HEADS, B_HEAD_DIM, IDX_HEADS, IDX_DIM, B_TOPK_MAX = 6, 64, 4, 32, 256
C_HEADS, C_HEAD_DIM = 6, 64
C_PATTERNS = ((128, 1), (512, 4), (2048, 16))
D_WIDTH, D_BLOCKS, D_CONV, D_C = 384, 8, 4, 8.0
REL_BUCKETS, REL_MAX_DIST = 32, 2048
N_SOFTMAX_HEADS = B_HEADS + C_HEADS
MOE_GROUPS, MOE_EXPERTS_PER_GROUP, MOE_HIDDEN = 4, 4, 512
N_EXPERTS = MOE_GROUPS * MOE_EXPERTS_PER_GROUP
IN_SPLITS = (A_HEADS * A_DK, A_HEADS * A_DK, A_HEADS * A_DV, A_HEADS * A_DV, A_LOWRANK,
             B_HEADS * B_HEAD_DIM, B_HEADS * B_HEAD_DIM, B_HEADS * B_HEAD_DIM,
             IDX_HEADS * IDX_DIM, IDX_DIM, IDX_HEADS,
             C_HEADS * C_HEAD_DIM, C_HEADS * C_HEAD_DIM, C_HEADS * C_HEAD_DIM,
             D_WIDTH, D_WIDTH)

LANES = 128
VMEM_LIMIT = 56 * 1024 * 1024
NEG = -1e30
INT_MIN = -2 ** 31
LOG2E = math.log2(math.e)

ATT_BLOCK = 256
SEQ_BLOCK = 256
PROJ_ROWS = 512
MOE_ROWS = 1024
MOE_WINDOW = 384
MOE_ALIGN = 128

GLA_W = 4 * A_HEADS * LANES + LANES
ATT_W = 3 * B_HEADS * B_HEAD_DIM
IDX_W = 3 * LANES
LRU_W = 2 * D_WIDTH
GROUP_WIDTHS = (GLA_W, ATT_W, IDX_W, ATT_W, LRU_W)
NT_DIMS = (((1,), (1,)), ((), ()))
TN_DIMS = (((0,), (0,)), ((), ()))


def _params(*sem):
    return pltpu.CompilerParams(dimension_semantics=sem, vmem_limit_bytes=VMEM_LIMIT)


def _resident(shape):
    zeros = (0,) * len(shape)
    return pl.BlockSpec(shape, lambda *_: zeros, pipeline_mode=pl.Buffered(1))


def _rms(x, gain):
    return x * lax.rsqrt(jnp.mean(x * x, axis=-1, keepdims=True) + EPS) * gain


def _sigmoid(x):
    return 1.0 / (1.0 + jnp.exp(-x))


def _softplus(x):
    return jnp.maximum(x, 0.0) + jnp.log1p(jnp.exp(-jnp.abs(x)))


def _inproj_kernel(x_ref, g_ref, w_ref, *out_refs):
    xn = _rms(x_ref[...], g_ref[...]).astype(BF16)
    off = 0
    for o_ref in out_refs:
        width = o_ref.shape[-1]
        o_ref[...] = jnp.dot(xn, w_ref[:, off:off + width],
                             preferred_element_type=F32).astype(o_ref.dtype)
        off += width


def _inproj(h, gain, w_all):
    n = h.shape[0]
    tm = min(PROJ_ROWS, n)
    total = sum(GROUP_WIDTHS)
    return pl.pallas_call(
        _inproj_kernel,
        grid=(n // tm,),
        in_specs=[pl.BlockSpec((tm, D_MODEL), lambda i: (i, 0)),
                  _resident((1, D_MODEL)),
                  _resident((D_MODEL, total))],
        out_specs=[pl.BlockSpec((tm, w), lambda i: (i, 0)) for w in GROUP_WIDTHS],
        out_shape=[jax.ShapeDtypeStruct((n, w), BF16) for w in GROUP_WIDTHS],
        compiler_params=_params("parallel"),
        name="inproj",
    )(h, gain, w_all)


def _gla_kernel(x_ref, wa2_ref, ba2_ref, gain_ref, o_ref, st_ref, *, tb):
    @pl.when(pl.program_id(1) == 0)
    def _():
        st_ref[...] = jnp.zeros_like(st_ref)

    hw = A_HEADS * LANES
    x = x_ref[...]
    z = jnp.dot(x[:, 4 * hw:].astype(F32), wa2_ref[...], precision=HIGHEST,
                preferred_element_type=F32) + ba2_ref[...]
    log_a = -_softplus(-z) * (1.0 / A_GATE_TAU)

    r64 = lax.broadcasted_iota(jnp.int32, (A_CHUNK, A_CHUNK), 0)
    c64 = lax.broadcasted_iota(jnp.int32, (A_CHUNK, A_CHUNK), 1)
    causal = c64 <= r64
    tri = jnp.where(causal, 1.0, 0.0).astype(BF16)
    la_hi = log_a.astype(BF16)
    la_lo = (log_a - la_hi.astype(F32)).astype(BF16)
    b_parts, b_last_parts = [], []
    for c in range(tb // A_CHUNK):
        rs = slice(c * A_CHUNK, (c + 1) * A_CHUNK)
        bc = (jnp.dot(tri, la_hi[rs], preferred_element_type=F32)
              + jnp.dot(tri, la_lo[rs], preferred_element_type=F32))
        b_parts.append(bc)
        b_last_parts.append(jnp.broadcast_to(bc[A_CHUNK - 1:A_CHUNK, :], bc.shape))
    b = jnp.concatenate(b_parts, axis=0)
    b_last = jnp.concatenate(b_last_parts, axis=0)

    q = x[:, 0:hw].astype(F32) * (A_DK ** -0.5)
    k = x[:, hw:2 * hw].astype(F32)
    v = x[:, 2 * hw:3 * hw]
    q_dec = (q * jnp.exp(b)).astype(BF16)
    k_dec = (k * jnp.exp(-b)).astype(BF16)
    k_end = (k * jnp.exp(b_last - b)).astype(BF16)
    decay = jnp.exp(b_last)

    heads = []
    for h in range(A_HEADS):
        cs = slice(h * LANES, (h + 1) * LANES)
        state_t = st_ref[h]
        chunks = []
        for c in range(tb // A_CHUNK):
            rs = slice(c * A_CHUNK, (c + 1) * A_CHUNK)
            qc, kc, kec, vc = q_dec[rs, cs], k_dec[rs, cs], k_end[rs, cs], v[rs, cs]
            att = lax.dot_general(qc, kc, NT_DIMS, preferred_element_type=F32)
            att = jnp.where(causal, att, 0.0).astype(BF16)
            oc = jnp.dot(att, vc, preferred_element_type=F32)
            oc += lax.dot_general(qc, state_t.astype(BF16), NT_DIMS, preferred_element_type=F32)
            inc_t = lax.dot_general(vc, kec, TN_DIMS, preferred_element_type=F32)
            state_t = state_t * decay[c * A_CHUNK:c * A_CHUNK + 1, cs] + inc_t
            chunks.append(oc)
        st_ref[h] = state_t
        o_h = jnp.concatenate(chunks, axis=0)
        ms = jnp.sum(o_h * o_h, axis=-1, keepdims=True) * (1.0 / A_DV)
        heads.append(o_h * lax.rsqrt(ms + EPS))
    o = jnp.concatenate(heads, axis=1) * gain_ref[...]
    g = x[:, 3 * hw:4 * hw].astype(F32)
    o_ref[...] = (o * (g * _sigmoid(g))).astype(o_ref.dtype)


def _gla(x3, wa2, ba2, gain):
    bsz, t, _ = x3.shape
    tb = min(SEQ_BLOCK, t)
    hw = A_HEADS * LANES
    return pl.pallas_call(
        functools.partial(_gla_kernel, tb=tb),
        grid=(bsz, t // tb),
        in_specs=[pl.BlockSpec((None, tb, GLA_W), lambda b, i: (b, i, 0)),
                  _resident((LANES, hw)), _resident((1, hw)), _resident((1, hw))],
        out_specs=pl.BlockSpec((None, tb, hw), lambda b, i: (b, i, 0)),
        out_shape=jax.ShapeDtypeStruct((bsz, t, hw), BF16),
        scratch_shapes=[pltpu.VMEM((A_HEADS, LANES, LANES), F32)],
        compiler_params=_params("parallel", "arbitrary"),
        name="gla",
    )(x3, wa2, ba2, gain)


def _rglru_kernel(d_ref, cw_ref, cb_ref, wri_ref, bri_ref, lam_ref, o_ref, xbuf, hc_ref, *, tb):
    halo = 8

    @pl.when(pl.program_id(1) == 0)
    def _():
        xbuf[0:halo, :] = jnp.zeros((halo, D_WIDTH), F32)
        hc_ref[...] = jnp.zeros_like(hc_ref)

    d = d_ref[...]
    y = d[:, 0:D_WIDTH].astype(F32)
    x = d[:, D_WIDTH:].astype(F32)
    xbuf[halo:halo + tb, :] = x
    xc = cb_ref[...]
    for j in range(D_CONV):
        xc = xc + cw_ref[j:j + 1, :] * xbuf[pl.ds(halo - (D_CONV - 1) + j, tb), :]
    xbuf[0:halo, :] = x[tb - halo:tb, :]

    ri = jnp.dot(xc.astype(BF16), wri_ref[...], preferred_element_type=F32) + bri_ref[...]
    r = _sigmoid(ri[:, 0:D_WIDTH])
    ig = _sigmoid(ri[:, D_WIDTH:])
    log_a = (-D_C) * r * _softplus(-lam_ref[...])
    a = jnp.exp(log_a)
    u = jnp.sqrt(-jnp.tanh(log_a) * (a * a + 1.0)) * (ig * xc)

    row = lax.broadcasted_iota(jnp.int32, (tb, D_WIDTH), 0)
    s = 1
    while s < tb:
        valid = row >= s
        u = jnp.where(valid, a * pltpu.roll(u, s, axis=0), 0.0) + u
        a = jnp.where(valid, a * pltpu.roll(a, s, axis=0), a)
        s *= 2
    hs = u + a * hc_ref[0:1, :]
    hc_ref[...] = jnp.broadcast_to(hs[tb - 1:tb, :], hc_ref.shape)
    gelu = 0.5 * y * (1.0 + jnp.tanh(math.sqrt(2.0 / math.pi) * (y + 0.044715 * (y * y * y))))
    o_ref[...] = (gelu * hs).astype(o_ref.dtype)


def _rglru(d3, conv_w, conv_b, w_ri, b_ri, lam):
    bsz, t, _ = d3.shape
    tb = min(SEQ_BLOCK, t)
    return pl.pallas_call(
        functools.partial(_rglru_kernel, tb=tb),
        grid=(bsz, t // tb),
        in_specs=[pl.BlockSpec((None, tb, LRU_W), lambda b, i: (b, i, 0)),
                  _resident((D_CONV, D_WIDTH)), _resident((1, D_WIDTH)),
                  _resident((D_WIDTH, 2 * D_WIDTH)), _resident((1, 2 * D_WIDTH)),
                  _resident((1, D_WIDTH))],
        out_specs=pl.BlockSpec((None, tb, D_WIDTH), lambda b, i: (b, i, 0)),
        out_shape=jax.ShapeDtypeStruct((bsz, t, D_WIDTH), BF16),
        scratch_shapes=[pltpu.VMEM((tb + 8, D_WIDTH), F32), pltpu.VMEM((8, D_WIDTH), F32)],
        compiler_params=_params("parallel", "arbitrary"),
        name="rglru",
    )(d3, conv_w, conv_b, w_ri, b_ri, lam)


def _bucket_edges():
    n = np.arange(REL_MAX_DIST + 1)
    max_exact = REL_BUCKETS // 2
    nf = np.maximum(n, 1).astype(np.float64)
    large = max_exact + (np.log(nf / max_exact) / math.log(REL_MAX_DIST / max_exact)
                         * (REL_BUCKETS - max_exact)).astype(np.int64)
    bucket = np.where(n < max_exact, n, np.minimum(large, REL_BUCKETS - 1))
    assert np.all(np.diff(bucket) >= 0)
    return [int(np.argmax(bucket >= b)) for b in range(REL_BUCKETS)]


def _tbias_kernel(tab_ref, o_ref, *, bs, edges):
    h = pl.program_id(0)
    d = pl.program_id(1)
    row = lax.broadcasted_iota(jnp.int32, (bs, bs), 0)
    col = lax.broadcasted_iota(jnp.int32, (bs, bs), 1)
    dist = d * bs + row - col
    val = jnp.full((bs, bs), tab_ref[0, h], F32)
    for b in range(1, REL_BUCKETS):
        val = jnp.where(dist >= edges[b], tab_ref[b, h], val)
    mult = jnp.zeros((bs, bs), jnp.int32)
    for window, dil in C_PATTERNS:
        hit = jnp.where((dist & (dil - 1)) == 0, 1, 0)
        mult = mult + jnp.where(dist <= window, hit, 0)
    logm = jnp.full((bs, bs), NEG, F32)
    for m in range(1, len(C_PATTERNS) + 1):
        logm = jnp.where(mult == m, math.log(m), logm)
    val = jnp.where(h >= B_HEADS, val + logm, val)
    o_ref[...] = jnp.where(dist >= 0, val * LOG2E, NEG)


def _tbias(rel_bias, t):
    bs = min(ATT_BLOCK, t)
    nd = t // bs
    for _, dil in C_PATTERNS:
        assert dil & (dil - 1) == 0
    return pl.pallas_call(
        functools.partial(_tbias_kernel, bs=bs, edges=_bucket_edges()),
        grid=(N_SOFTMAX_HEADS, nd),
        in_specs=[pl.BlockSpec(memory_space=pltpu.SMEM)],
        out_specs=pl.BlockSpec((None, None, bs, bs), lambda h, d: (h, d, 0, 0)),
        out_shape=jax.ShapeDtypeStruct((N_SOFTMAX_HEADS, nd, bs, bs), F32),
        compiler_params=_params("parallel", "parallel"),
        name="tbias",
    )(rel_bias)


def _attn_kernel(qb_ref, kb_ref, vb_ref, qc_ref, kc_ref, vc_ref, tb_ref, madd_ref,
                 ob_ref, oc_ref, vp_ref, m_ref, acc_ref, *, bs):
    i = pl.program_id(1)
    streams = ((qb_ref, kb_ref, vb_ref, ob_ref), (qc_ref, kc_ref, vc_ref, oc_ref))
    per_stream = B_HEADS
    n_heads = vp_ref.shape[0]
    half = LANES // 2
    assert B_HEAD_DIM == half and C_HEAD_DIM == half and B_HEADS == C_HEADS

    @pl.when(i == 0)
    def _():
        low_t = lax.broadcasted_iota(jnp.int32, (vb_ref.shape[0], LANES), 1) < half
        for h in range(0, n_heads, 2):
            v_ref = streams[h // per_stream][2]
            hp = (h % per_stream) // 2
            pair = v_ref[:, hp * LANES:(hp + 1) * LANES]
            vp_ref[h] = jnp.where(low_t, pair, 1.0).astype(BF16)
            vp_ref[h + 1] = jnp.where(low_t, 1.0, pair).astype(BF16)

    m_ref[...] = jnp.full(m_ref.shape, NEG, F32)
    acc_ref[...] = jnp.zeros(acc_ref.shape, F32)
    low = lax.broadcasted_iota(jnp.int32, (bs, LANES), 1) < half
    q_heads = []
    for h in range(n_heads):
        q_ref = streams[h // per_stream][0]
        hp = (h % per_stream) // 2
        qp = q_ref[:, hp * LANES:(hp + 1) * LANES].astype(F32) * (B_HEAD_DIM ** -0.5 * LOG2E)
        q_heads.append(jnp.where(low if h % 2 == 0 else jnp.logical_not(low), qp, 0.0).astype(BF16))

    def body(c, _):
        start = pl.multiple_of(c * bs, bs)
        extra = madd_ref[c].astype(F32)
        for h in range(n_heads):
            k_ref = streams[h // per_stream][1]
            hp = (h % per_stream) // 2
            ks = k_ref[pl.ds(start, bs), hp * LANES:(hp + 1) * LANES]
            s = lax.dot_general(q_heads[h], ks, NT_DIMS, preferred_element_type=F32)
            s = s + tb_ref[h, i - c]
            if h < per_stream:
                s = s + extra
            m_prev = m_ref[h]
            m_new = jnp.maximum(m_prev, jnp.max(s, axis=1, keepdims=True))
            alpha = jnp.exp2(m_prev - m_new)
            p = jnp.exp2((s - jnp.concatenate([m_new] * (bs // LANES), axis=1)).astype(BF16))
            acc_ref[h] = alpha * acc_ref[h] + jnp.dot(p, vp_ref[h, pl.ds(start, bs), :],
                                                      preferred_element_type=F32)
            m_ref[h] = m_new
        return 0

    lax.fori_loop(0, i + 1, body, 0)

    for h in range(0, n_heads, 2):
        o_ref = streams[h // per_stream][3]
        hp = (h % per_stream) // 2
        a0, a1 = acc_ref[h], acc_ref[h + 1]
        o0 = a0 / pltpu.roll(a0, half, axis=1)
        o1 = a1 / pltpu.roll(a1, half, axis=1)
        o_ref[:, hp * LANES:(hp + 1) * LANES] = jnp.where(low, o0, o1).astype(o_ref.dtype)


def _attention(dsa3, dil3, tbias, madd):
    bsz, t, width3 = dsa3.shape
    width = width3 // 3
    bs = min(ATT_BLOCK, t)
    nd = t // bs
    q_spec = pl.BlockSpec((None, bs, width), lambda b, i: (b, i, 0))
    k_spec = pl.BlockSpec((None, t, width), lambda b, i: (b, 0, 1), pipeline_mode=pl.Buffered(1))
    v_spec = pl.BlockSpec((None, t, width), lambda b, i: (b, 0, 2), pipeline_mode=pl.Buffered(1))
    out_spec = pl.BlockSpec((None, bs, width), lambda b, i: (b, i, 0))
    out_shape = jax.ShapeDtypeStruct((bsz, t, width), BF16)
    return pl.pallas_call(
        functools.partial(_attn_kernel, bs=bs),
        grid=(bsz, nd),
        in_specs=[q_spec, k_spec, v_spec, q_spec, k_spec, v_spec,
                  _resident(tbias.shape),
                  pl.BlockSpec((None, None, nd, bs, bs), lambda b, i: (b, i, 0, 0, 0))],
        out_specs=[out_spec, out_spec],
        out_shape=[out_shape, out_shape],
        scratch_shapes=[pltpu.VMEM((N_SOFTMAX_HEADS, t, LANES), BF16),
                        pltpu.VMEM((N_SOFTMAX_HEADS, bs, LANES), F32),
                        pltpu.VMEM((N_SOFTMAX_HEADS, bs, LANES), F32)],
        compiler_params=_params("arbitrary", "arbitrary"),
        name="attn",
    )(dsa3, dsa3, dsa3, dil3, dil3, dil3, tbias, madd)


def _dsa_select_kernel(iq_ref, ik_ref, iw_ref, madd_ref, key_ref, thr_ref, need_ref,
                       *, bs, nd, k_sel):
    pair = lambda i, c: i * (i + 1) // 2 + c
    lane = lax.broadcasted_iota(jnp.int32, (bs, LANES), 1)
    row = lax.broadcasted_iota(jnp.int32, (bs, bs), 0)
    col = lax.broadcasted_iota(jnp.int32, (bs, bs), 1)
    tiles = bs // LANES

    for i in range(nd):
        iq = iq_ref[i * bs:(i + 1) * bs, :]
        iw = iw_ref[i * bs:(i + 1) * bs, :].astype(F32) * (IDX_HEADS ** -0.5 * IDX_DIM ** -0.5)
        iq_heads = [jnp.where((lane // IDX_DIM) == hh, iq, 0.0).astype(BF16)
                    for hh in range(IDX_HEADS)]
        for c in range(i + 1):
            ikc = ik_ref[c * bs:(c + 1) * bs, :]
            sc = jnp.zeros((bs, bs), F32)
            for hh in range(IDX_HEADS):
                raw = lax.dot_general(iq_heads[hh], ikc, NT_DIMS, preferred_element_type=F32)
                sc = sc + iw[:, hh:hh + 1] * jnp.maximum(raw, 0.0)
            sc = jnp.where(sc == 0.0, 0.0, sc)
            bits = pltpu.bitcast(sc, jnp.int32)
            key = jnp.where(bits < 0, bits ^ 0x7FFFFFFF, bits)
            if c == i:
                key = jnp.where(col <= row, key, INT_MIN)
            key_ref[pair(i, c)] = key

    thr_ref[...] = jnp.full(thr_ref.shape, INT_MIN, jnp.int32)
    kf = float(k_sel)
    n_bits = 32

    def bisect(it, _):
        searching = it < n_bits
        step = jnp.where(searching, lax.shift_left(jnp.int32(1), jnp.maximum(n_bits - 1 - it, 0)), 1)
        for i in range(nd):
            thr = thr_ref[i]
            cand = thr + step
            acc = jnp.zeros((bs, LANES), F32)
            for c in range(i + 1):
                kk = key_ref[pair(i, c)]
                for j in range(tiles):
                    acc = acc + jnp.where(kk[:, j * LANES:(j + 1) * LANES] >= cand, 1.0, 0.0)
            cnt = jnp.sum(acc, axis=1, keepdims=True)
            thr_ref[i] = jnp.where(jnp.logical_and(searching, cnt >= kf), cand, thr)
            need_ref[i] = jnp.broadcast_to(kf - cnt, (bs, LANES))
        return 0

    lax.fori_loop(0, n_bits + 1, bisect, 0)

    prefix = jnp.where(row <= col, 1.0, 0.0).astype(BF16)
    for i in range(nd):
        thr2 = jnp.concatenate([thr_ref[i]] * tiles, axis=1)
        need = need_ref[i][:, 0:1]
        seen = jnp.zeros((bs, 1), F32)
        for c in range(i + 1):
            kk = key_ref[pair(i, c)]
            tie = kk == thr2
            rank = seen + jnp.dot(jnp.where(tie, 1.0, 0.0).astype(BF16), prefix,
                                  preferred_element_type=F32)
            tie_add = jnp.where(tie, jnp.where(rank <= need, 0.0, NEG), NEG)
            madd_ref[i, c] = jnp.where(kk > thr2, 0.0, tie_add).astype(madd_ref.dtype)
            seen = rank[:, bs - 1:bs]
        for c in range(i + 1, nd):
            madd_ref[i, c] = jnp.full((bs, bs), NEG, madd_ref.dtype)


def _dsa_select(idx3):
    bsz, t, _ = idx3.shape
    bs = min(ATT_BLOCK, t)
    nd = t // bs
    k_sel = min(B_TOPK_MAX, t // 4)
    col_block = lambda j: pl.BlockSpec((None, t, LANES), lambda b: (b, 0, j))
    return pl.pallas_call(
        functools.partial(_dsa_select_kernel, bs=bs, nd=nd, k_sel=k_sel),
        grid=(bsz,),
        in_specs=[col_block(0), col_block(1), col_block(2)],
        out_specs=pl.BlockSpec((None, nd, nd, bs, bs), lambda b: (b, 0, 0, 0, 0)),
        out_shape=jax.ShapeDtypeStruct((bsz, nd, nd, bs, bs), BF16),
        scratch_shapes=[pltpu.VMEM((nd * (nd + 1) // 2, bs, bs), jnp.int32),
                        pltpu.VMEM((nd, bs, LANES), jnp.int32),
                        pltpu.VMEM((nd, bs, LANES), F32)],
        compiler_params=_params("parallel"),
        name="dsa_select",
    )(idx3, idx3, idx3)


def _softmax_mixers(dsa3, idx3, dil3, tbias):
    return _attention(dsa3, dil3, tbias, _dsa_select(idx3))


def _route(logits):
    lane = lax.broadcasted_iota(jnp.int32, logits.shape, 1).astype(F32)
    far = float(LANES)

    def first_argmax(vals):
        top = jnp.max(vals, axis=1, keepdims=True)
        return top, jnp.min(jnp.where(vals == top, lane, far), axis=1, keepdims=True)

    is_group = lane < MOE_GROUPS
    g_top, g_sel = first_argmax(jnp.where(is_group, logits, NEG))
    g_w = 1.0 / jnp.sum(jnp.where(is_group, jnp.exp(logits - g_top), 0.0), axis=1, keepdims=True)
    lo = MOE_GROUPS + MOE_EXPERTS_PER_GROUP * g_sel
    e_logits = jnp.where(lane >= lo, jnp.where(lane < lo + MOE_EXPERTS_PER_GROUP, logits, NEG), NEG)
    v1, i1 = first_argmax(e_logits)
    v2, i2 = first_argmax(jnp.where(lane == i1, NEG, e_logits))
    e2 = jnp.exp(v2 - v1)
    w1 = 1.0 / (1.0 + e2)
    return jnp.where(lane == i1, w1 * g_w, jnp.where(lane == i2, e2 * w1 * g_w, 0.0))


def _merge_kernel(h_ref, g1_ref, oa_ref, ob_ref, oc_ref, od_ref, wg_ref, bg_ref,
                  wba_ref, wbb_ref, wbc_ref, wbd_ref, wo_ref, g2_ref, wr_ref, br_ref,
                  h1_ref, xn2_ref, gates_ref):
    h = h_ref[...]
    xn = _rms(h, g1_ref[...]).astype(BF16)
    merged = None
    for gi, (o_ref, wb_ref) in enumerate(((oa_ref, wba_ref), (ob_ref, wbb_ref),
                                          (oc_ref, wbc_ref), (od_ref, wbd_ref))):
        gate = _sigmoid(jnp.dot(xn, wg_ref[gi], preferred_element_type=F32) + bg_ref[gi:gi + 1, :])
        term = gate * jnp.dot(o_ref[...], wb_ref[...], preferred_element_type=F32)
        merged = term if merged is None else merged + term
    h1 = h + jnp.dot(merged.astype(BF16), wo_ref[...], preferred_element_type=F32)
    h1_ref[...] = h1
    xn2 = _rms(h1, g2_ref[...]).astype(BF16)
    xn2_ref[...] = xn2
    logits = jnp.dot(xn2, wr_ref[...], preferred_element_type=F32) + br_ref[...]
    gates_ref[...] = _route(logits)


def _merge(h, g1, oa, ob, oc, od, wg, bg, wba, wbb, wbc, wbd, wo, g2, wr, br):
    n = h.shape[0]
    tm = min(PROJ_ROWS, n)
    rows = lambda w: pl.BlockSpec((tm, w), lambda i: (i, 0))
    return pl.pallas_call(
        _merge_kernel,
        grid=(n // tm,),
        in_specs=[rows(D_MODEL), _resident(g1.shape),
                  rows(oa.shape[1]), rows(ob.shape[1]), rows(oc.shape[1]), rows(od.shape[1]),
                  _resident(wg.shape), _resident(bg.shape),
                  _resident(wba.shape), _resident(wbb.shape), _resident(wbc.shape),
                  _resident(wbd.shape), _resident(wo.shape), _resident(g2.shape),
                  _resident(wr.shape), _resident(br.shape)],
        out_specs=[rows(D_MODEL), rows(D_MODEL), rows(LANES)],
        out_shape=[jax.ShapeDtypeStruct((n, D_MODEL), F32),
                   jax.ShapeDtypeStruct((n, D_MODEL), BF16),
                   jax.ShapeDtypeStruct((n, LANES), F32)],
        compiler_params=_params("parallel"),
        name="merge",
    )(h, g1, oa, ob, oc, od, wg, bg, wba, wbb, wbc, wbd, wo, g2, wr, br)


def _moe_kernel(x_ref, gates_ref, h1_ref, wg_ref, wu_ref, wd_ref, gf_ref, o_ref,
                xs_ref, ys_ref, gs_ref, pos_ref, win_ref, *, final_norm, tm):
    e = pl.program_id(1)
    grp = e // MOE_EXPERTS_PER_GROUP
    ts = tm + MOE_GROUPS * MOE_ALIGN

    @pl.when(e == 0)
    def _():
        gates = gates_ref[...]
        lane = lax.broadcasted_iota(jnp.int32, (tm, LANES), 1)
        member = jnp.zeros((tm, LANES), F32)
        for gg in range(MOE_GROUPS):
            lo = MOE_GROUPS + gg * MOE_EXPERTS_PER_GROUP
            in_g = jnp.where(lane >= lo, jnp.where(lane < lo + MOE_EXPERTS_PER_GROUP, gates, 0.0), 0.0)
            total = jnp.sum(in_g, axis=1, keepdims=True)
            member = jnp.where(lane == gg, jnp.where(total > 0.0, 1.0, 0.0), member)
        row_sq = lax.broadcasted_iota(jnp.int32, (tm, tm), 0)
        col_sq = lax.broadcasted_iota(jnp.int32, (tm, tm), 1)
        before = jnp.dot(jnp.where(col_sq < row_sq, 1.0, 0.0).astype(BF16), member.astype(BF16),
                         preferred_element_type=F32)
        count = before[tm - 1:tm, :] + member[tm - 1:tm, :]
        padded = jnp.ceil(count * (1.0 / MOE_ALIGN)) * MOE_ALIGN
        r128 = lax.broadcasted_iota(jnp.int32, (LANES, LANES), 0)
        c128 = lax.broadcasted_iota(jnp.int32, (LANES, LANES), 1)
        first = jnp.dot(jnp.broadcast_to(padded, (8, LANES)), jnp.where(r128 < c128, 1.0, 0.0),
                        precision=HIGHEST, preferred_element_type=F32)[0:1, :]
        slot = member * (first + before)
        pos_ref[...] = jnp.broadcast_to(jnp.sum(slot, axis=1, keepdims=True), (tm, LANES))
        pos_t = lax.dot_general(jnp.ones((8, LANES), F32), slot, NT_DIMS, precision=HIGHEST,
                                preferred_element_type=F32)[0:1, :]
        lane_row = lax.broadcasted_iota(jnp.int32, (1, LANES), 1)
        for gg in range(MOE_GROUPS):
            seg_start = jnp.sum(jnp.where(lane_row == gg, first, 0.0)).astype(jnp.int32)
            seg_rows = jnp.sum(jnp.where(lane_row == gg, count, 0.0)).astype(jnp.int32)
            win_ref[gg] = seg_start
            win_ref[MOE_GROUPS + gg] = (seg_rows + MOE_WINDOW - 1) // MOE_WINDOW
        row_s = lax.broadcasted_iota(jnp.int32, (ts, tm), 0)
        perm = jnp.where(pos_t == row_s.astype(F32), 1.0, 0.0).astype(BF16)
        xs_ref[0:ts, :] = jnp.dot(perm, x_ref[...], preferred_element_type=F32).astype(BF16)
        xs_ref[ts:, :] = jnp.zeros((xs_ref.shape[0] - ts, D_MODEL), BF16)
        g_hi = gates.astype(BF16)
        g_lo = (gates - g_hi.astype(F32)).astype(BF16)
        gs_ref[0:ts, :] = (jnp.dot(perm, g_hi, preferred_element_type=F32)
                           + jnp.dot(perm, g_lo, preferred_element_type=F32))
        gs_ref[ts:, :] = jnp.zeros((gs_ref.shape[0] - ts, LANES), F32)
        ys_ref[...] = jnp.zeros(ys_ref.shape, F32)

    lane_w = lax.broadcasted_iota(jnp.int32, (MOE_WINDOW, LANES), 1)

    def window(k, _):
        start = pl.multiple_of(win_ref[grp] + k * MOE_WINDOW, MOE_ALIGN)
        xw = xs_ref[pl.ds(start, MOE_WINDOW), :]
        hg = jnp.dot(xw, wg_ref[...], preferred_element_type=F32)
        hu = jnp.dot(xw, wu_ref[...], preferred_element_type=F32)
        hid = (hg * _sigmoid(hg) * hu).astype(BF16)
        y = jnp.dot(hid, wd_ref[...], preferred_element_type=F32)
        gate = jnp.sum(jnp.where(lane_w == e + MOE_GROUPS, gs_ref[pl.ds(start, MOE_WINDOW), :], 0.0),
                       axis=1, keepdims=True)
        ys_ref[pl.ds(start, MOE_WINDOW), :] += gate * y
        return 0

    lax.fori_loop(0, win_ref[MOE_GROUPS + grp], window, 0)

    @pl.when(e == N_EXPERTS - 1)
    def _():
        pos_s = jnp.concatenate([pos_ref[...]] * (ts // LANES), axis=1)
        col_s = lax.broadcasted_iota(jnp.int32, (tm, ts), 1)
        unperm = jnp.where(pos_s == col_s.astype(F32), 1.0, 0.0).astype(BF16)
        ys = ys_ref[0:ts, :]
        y_hi = ys.astype(BF16)
        y_lo = (ys - y_hi.astype(F32)).astype(BF16)
        out = h1_ref[...] + (jnp.dot(unperm, y_hi, preferred_element_type=F32)
                             + jnp.dot(unperm, y_lo, preferred_element_type=F32))
        o_ref[...] = _rms(out, gf_ref[...]) if final_norm else out


def _moe(xn2, gates, h1, wg, wu, wd, gf, final_norm):
    n = xn2.shape[0]
    tm = min(MOE_ROWS, n)
    sorted_rows = tm + MOE_GROUPS * MOE_ALIGN + MOE_WINDOW
    rows = lambda w: pl.BlockSpec((tm, w), lambda i, e: (i, 0))
    return pl.pallas_call(
        functools.partial(_moe_kernel, final_norm=final_norm, tm=tm),
        grid=(n // tm, N_EXPERTS),
        in_specs=[rows(D_MODEL), rows(LANES), rows(D_MODEL),
                  pl.BlockSpec((None, D_MODEL, MOE_HIDDEN), lambda i, e: (e, 0, 0)),
                  pl.BlockSpec((None, D_MODEL, MOE_HIDDEN), lambda i, e: (e, 0, 0)),
                  pl.BlockSpec((None, MOE_HIDDEN, D_MODEL), lambda i, e: (e, 0, 0)),
                  pl.BlockSpec((1, D_MODEL), lambda i, e: (0, 0))],
        out_specs=rows(D_MODEL),
        out_shape=jax.ShapeDtypeStruct((n, D_MODEL), F32),
        scratch_shapes=[pltpu.VMEM((sorted_rows, D_MODEL), BF16),
                        pltpu.VMEM((sorted_rows, D_MODEL), F32),
                        pltpu.VMEM((sorted_rows, LANES), F32),
                        pltpu.VMEM((tm, LANES), F32),
                        pltpu.SMEM((2 * MOE_GROUPS,), jnp.int32)],
        compiler_params=_params("parallel", "arbitrary"),
        name="moe",
    )(xn2, gates, h1, wg, wu, wd, gf)


def _inproj_columns():
    offs = np.concatenate([[0], np.cumsum(IN_SPLITS)])
    (aq, ak, av, ag, alr, bq, bk, bv, iq, ik, iw, cq, ck, cv, dy, dx) = offs[:-1]
    cols = []

    def per_head(base, width):
        for h in range(A_HEADS):
            cols.extend(list(range(base + h * width, base + (h + 1) * width)) + [-1] * (LANES - width))

    per_head(aq, A_DK)
    per_head(ak, A_DK)
    per_head(av, A_DV)
    per_head(ag, A_DV)
    cols.extend(list(range(alr, alr + A_LOWRANK)) + [-1] * (LANES - A_LOWRANK))
    cols.extend(range(bq, bq + ATT_W))
    cols.extend(range(iq, iq + IDX_HEADS * IDX_DIM))
    cols.extend(list(range(ik, ik + IDX_DIM)) * IDX_HEADS)
    cols.extend(list(range(iw, iw + IDX_HEADS)) + [-1] * (LANES - IDX_HEADS))
    cols.extend(range(cq, cq + ATT_W))
    cols.extend(range(dy, dy + LRU_W))
    cols = np.asarray(cols, np.int32)
    assert cols.shape[0] == sum(GROUP_WIDTHS)
    return cols


def _pad_heads(a, width, axis):
    a = jnp.moveaxis(a, axis, -1)
    a = a.reshape(a.shape[:-1] + (A_HEADS, width))
    a = jnp.pad(a, [(0, 0)] * (a.ndim - 1) + [(0, LANES - width)])
    return jnp.moveaxis(a.reshape(a.shape[:-2] + (A_HEADS * LANES,)), -1, axis)


def _block_diag(w):
    nb, bw, _ = w.shape
    eye = jnp.eye(nb, dtype=w.dtype)
    return jnp.einsum('ncd,nm->ncmd', w, eye).reshape(nb * bw, nb * bw)


def kernel(x, w_in, a_w2, a_b2, a_gain, conv_w, conv_b, lru_wr, lru_br, lru_wi, lru_bi, lru_lambda, w_gate, b_gate, w_branch, w_out, rel_bias, norm1, norm2, norm_f, moe_wrg, moe_brg, moe_wre, moe_bre, moe_wg, moe_wu, moe_wd):
    bsz, t, _ = x.shape
    n = bsz * t
    depth = w_in.shape[0]
    cols = _inproj_columns()
    tbias = _tbias(rel_bias, t)
    h = x.reshape(n, D_MODEL)
    for l in range(depth):
        w_all = jnp.where(cols[None, :] >= 0, w_in[l][:, np.maximum(cols, 0)], 0.0).astype(BF16)
        gla_in, dsa_in, idx_in, dil_in, lru_in = _inproj(h, norm1[l][None, :], w_all)

        wa2 = jnp.pad(_pad_heads(a_w2[l], A_DK, 1), ((0, LANES - A_LOWRANK), (0, 0)))
        o_a = _gla(gla_in.reshape(bsz, t, GLA_W), wa2, _pad_heads(a_b2[l], A_DK, 0)[None, :],
                   _pad_heads(a_gain[l], A_DV, 0)[None, :])
        o_b, o_c = _softmax_mixers(dsa_in.reshape(bsz, t, ATT_W), idx_in.reshape(bsz, t, IDX_W),
                                   dil_in.reshape(bsz, t, ATT_W), tbias)
        w_ri = jnp.concatenate([_block_diag(lru_wr[l]), _block_diag(lru_wi[l])], axis=1).astype(BF16)
        b_ri = jnp.concatenate([lru_br[l], lru_bi[l]])[None, :]
        o_d = _rglru(lru_in.reshape(bsz, t, LRU_W), conv_w[l], conv_b[l][None, :], w_ri, b_ri,
                     lru_lambda[l][None, :])

        w_router = jnp.pad(jnp.concatenate([moe_wrg[l], moe_wre[l]], axis=1),
                           ((0, 0), (0, LANES - MOE_GROUPS - N_EXPERTS)))
        b_router = jnp.pad(jnp.concatenate([moe_brg[l], moe_bre[l]]),
                           (0, LANES - MOE_GROUPS - N_EXPERTS))[None, :]
        h1, xn2, gates = _merge(
            h, norm1[l][None, :], o_a.reshape(n, -1), o_b.reshape(n, -1), o_c.reshape(n, -1),
            o_d.reshape(n, -1), w_gate[l].astype(BF16), b_gate[l],
            _pad_heads(w_branch[l, 0], A_DV, 0).astype(BF16), w_branch[l, 1].astype(BF16),
            w_branch[l, 2].astype(BF16), w_branch[l, 3].astype(BF16), w_out[l].astype(BF16),
            norm2[l][None, :], w_router.astype(BF16), b_router)
        h = _moe(xn2, gates, h1, moe_wg[l].astype(BF16), moe_wu[l].astype(BF16),
                 moe_wd[l].astype(BF16), norm_f[None, :], final_norm=(l == depth - 1))
    return h.reshape(bsz, t, D_MODEL)
```

```python
import functools
import math

import numpy as np
import jax
import jax.numpy as jnp
from jax import lax
from jax.experimental import pallas as pl
from jax.experimental.pallas import tpu as pltpu

F32 = jnp.float32
BF16 = jnp.bfloat16
HIGHEST = lax.Precision.HIGHEST

D_MODEL = 1024
EPS = 1e-6
A_HEADS, A_DK, A_DV, A_LOWRANK, A_GATE_TAU, A_CHUNK = 4, 48, 96, 16, 16.0, 64
B_HEADS, B_HEAD_DIM, IDX_HEADS, IDX_DIM, B_TOPK_MAX = 6, 64, 4, 32, 256
C_HEADS, C_HEAD_DIM = 6, 64
C_PATTERNS = ((128, 1), (512, 4), (2048, 16))
D_WIDTH, D_BLOCKS, D_CONV, D_C = 384, 8, 4, 8.0
REL_BUCKETS, REL_MAX_DIST = 32, 2048
N_SOFTMAX_HEADS = B_HEADS + C_HEADS
MOE_GROUPS, MOE_EXPERTS_PER_GROUP, MOE_HIDDEN = 4, 4, 512
N_EXPERTS = MOE_GROUPS * MOE_EXPERTS_PER_GROUP
IN_SPLITS = (A_HEADS * A_DK, A_HEADS * A_DK, A_HEADS * A_DV, A_HEADS * A_DV, A_LOWRANK,
             B_HEADS * B_HEAD_DIM, B_HEADS * B_HEAD_DIM, B_HEADS * B_HEAD_DIM,
             IDX_HEADS * IDX_DIM, IDX_DIM, IDX_HEADS,
             C_HEADS * C_HEAD_DIM, C_HEADS * C_HEAD_DIM, C_HEADS * C_HEAD_DIM,
             D_WIDTH, D_WIDTH)

LANES = 128
VMEM_LIMIT = 56 * 1024 * 1024
NEG = -1e30
INT_MIN = -2 ** 31
LOG2E = math.log2(math.e)

ATT_BLOCK = 256
SEQ_BLOCK = 256
PROJ_ROWS = 512
MOE_ROWS = 1024
MOE_WINDOW = 320
MOE_ALIGN = 64

GLA_W = 4 * A_HEADS * LANES + LANES
ATT_W = 3 * B_HEADS * B_HEAD_DIM
IDX_W = 3 * LANES
LRU_W = 2 * D_WIDTH
GROUP_WIDTHS = (GLA_W, ATT_W, IDX_W, ATT_W, LRU_W)
NT_DIMS = (((1,), (1,)), ((), ()))
TN_DIMS = (((0,), (0,)), ((), ()))


def _params(*sem):
    return pltpu.CompilerParams(dimension_semantics=sem, vmem_limit_bytes=VMEM_LIMIT)


def _resident(shape):
    zeros = (0,) * len(shape)
    return pl.BlockSpec(shape, lambda *_: zeros, pipeline_mode=pl.Buffered(1))


def _rms(x, gain):
    return x * lax.rsqrt(jnp.mean(x * x, axis=-1, keepdims=True) + EPS) * gain


def _sigmoid(x):
    return 1.0 / (1.0 + jnp.exp(-x))


def _softplus(x):
    return jnp.maximum(x, 0.0) + jnp.log1p(jnp.exp(-jnp.abs(x)))


def _inproj_kernel(x_ref, g_ref, w_ref, *out_refs):
    xn = _rms(x_ref[...], g_ref[...]).astype(BF16)
    off = 0
    for o_ref in out_refs:
        width = o_ref.shape[-1]
        o_ref[...] = jnp.dot(xn, w_ref[:, off:off + width],
                             preferred_element_type=F32).astype(o_ref.dtype)
        off += width


def _inproj(h, gain, w_all):
    n = h.shape[0]
    tm = min(PROJ_ROWS, n)
    total = sum(GROUP_WIDTHS)
    return pl.pallas_call(
        _inproj_kernel,
        grid=(n // tm,),
        in_specs=[pl.BlockSpec((tm, D_MODEL), lambda i: (i, 0)),
                  _resident((1, D_MODEL)),
                  _resident((D_MODEL, total))],
        out_specs=[pl.BlockSpec((tm, w), lambda i: (i, 0)) for w in GROUP_WIDTHS],
        out_shape=[jax.ShapeDtypeStruct((n, w), BF16) for w in GROUP_WIDTHS],
        compiler_params=_params("parallel"),
        name="inproj",
    )(h, gain, w_all)


def _gla_kernel(x_ref, wa2_ref, ba2_ref, gain_ref, o_ref, st_ref, *, tb):
    @pl.when(pl.program_id(1) == 0)
    def _():
        st_ref[...] = jnp.zeros_like(st_ref)

    hw = A_HEADS * LANES
    x = x_ref[...]
    z = jnp.dot(x[:, 4 * hw:].astype(F32), wa2_ref[...], precision=HIGHEST,
                preferred_element_type=F32) + ba2_ref[...]
    log_a = -_softplus(-z) * (1.0 / A_GATE_TAU)

    r64 = lax.broadcasted_iota(jnp.int32, (A_CHUNK, A_CHUNK), 0)
    c64 = lax.broadcasted_iota(jnp.int32, (A_CHUNK, A_CHUNK), 1)
    causal = c64 <= r64
    tri = jnp.where(causal, 1.0, 0.0).astype(BF16)
    la_hi = log_a.astype(BF16)
    la_lo = (log_a - la_hi.astype(F32)).astype(BF16)
    b_parts, b_last_parts = [], []
    for c in range(tb // A_CHUNK):
        rs = slice(c * A_CHUNK, (c + 1) * A_CHUNK)
        bc = (jnp.dot(tri, la_hi[rs], preferred_element_type=F32)
              + jnp.dot(tri, la_lo[rs], preferred_element_type=F32))
        b_parts.append(bc)
        b_last_parts.append(jnp.broadcast_to(bc[A_CHUNK - 1:A_CHUNK, :], bc.shape))
    b = jnp.concatenate(b_parts, axis=0)
    b_last = jnp.concatenate(b_last_parts, axis=0)

    q = x[:, 0:hw].astype(F32) * (A_DK ** -0.5)
    k = x[:, hw:2 * hw].astype(F32)
    v = x[:, 2 * hw:3 * hw]
    q_dec = (q * jnp.exp(b)).astype(BF16)
    k_dec = (k * jnp.exp(-b)).astype(BF16)
    k_end = (k * jnp.exp(b_last - b)).astype(BF16)
    decay = jnp.exp(b_last)

    heads = []
    for h in range(A_HEADS):
        cs = slice(h * LANES, (h + 1) * LANES)
        state_t = st_ref[h]
        chunks = []
        for c in range(tb // A_CHUNK):
            rs = slice(c * A_CHUNK, (c + 1) * A_CHUNK)
            qc, kc, kec, vc = q_dec[rs, cs], k_dec[rs, cs], k_end[rs, cs], v[rs, cs]
            att = lax.dot_general(qc, kc, NT_DIMS, preferred_element_type=F32)
            att = jnp.where(causal, att, 0.0).astype(BF16)
            oc = jnp.dot(att, vc, preferred_element_type=F32)
            oc += lax.dot_general(qc, state_t.astype(BF16), NT_DIMS, preferred_element_type=F32)
            inc_t = lax.dot_general(vc, kec, TN_DIMS, preferred_element_type=F32)
            state_t = state_t * decay[c * A_CHUNK:c * A_CHUNK + 1, cs] + inc_t
            chunks.append(oc)
        st_ref[h] = state_t
        o_h = jnp.concatenate(chunks, axis=0)
        ms = jnp.sum(o_h * o_h, axis=-1, keepdims=True) * (1.0 / A_DV)
        heads.append(o_h * lax.rsqrt(ms + EPS))
    o = jnp.concatenate(heads, axis=1) * gain_ref[...]
    g = x[:, 3 * hw:4 * hw].astype(F32)
    o_ref[...] = (o * (g * _sigmoid(g))).astype(o_ref.dtype)


def _gla(x3, wa2, ba2, gain):
    bsz, t, _ = x3.shape
    tb = min(SEQ_BLOCK, t)
    hw = A_HEADS * LANES
    return pl.pallas_call(
        functools.partial(_gla_kernel, tb=tb),
        grid=(bsz, t // tb),
        in_specs=[pl.BlockSpec((None, tb, GLA_W), lambda b, i: (b, i, 0)),
                  _resident((LANES, hw)), _resident((1, hw)), _resident((1, hw))],
        out_specs=pl.BlockSpec((None, tb, hw), lambda b, i: (b, i, 0)),
        out_shape=jax.ShapeDtypeStruct((bsz, t, hw), BF16),
        scratch_shapes=[pltpu.VMEM((A_HEADS, LANES, LANES), F32)],
        compiler_params=_params("parallel", "arbitrary"),
        name="gla",
    )(x3, wa2, ba2, gain)


def _rglru_kernel(d_ref, cw_ref, cb_ref, wri_ref, bri_ref, lam_ref, o_ref, xbuf, hc_ref, *, tb):
    halo = 8

    @pl.when(pl.program_id(1) == 0)
    def _():
        xbuf[0:halo, :] = jnp.zeros((halo, D_WIDTH), F32)
        hc_ref[...] = jnp.zeros_like(hc_ref)

    d = d_ref[...]
    y = d[:, 0:D_WIDTH].astype(F32)
    x = d[:, D_WIDTH:].astype(F32)
    xbuf[halo:halo + tb, :] = x
    xc = cb_ref[...]
    for j in range(D_CONV):
        xc = xc + cw_ref[j:j + 1, :] * xbuf[pl.ds(halo - (D_CONV - 1) + j, tb), :]
    xbuf[0:halo, :] = x[tb - halo:tb, :]

    ri = jnp.dot(xc.astype(BF16), wri_ref[...], preferred_element_type=F32) + bri_ref[...]
    r = _sigmoid(ri[:, 0:D_WIDTH])
    ig = _sigmoid(ri[:, D_WIDTH:])
    log_a = (-D_C) * r * _softplus(-lam_ref[...])
    a = jnp.exp(log_a)
    u = jnp.sqrt(-jnp.tanh(log_a) * (a * a + 1.0)) * (ig * xc)

    row = lax.broadcasted_iota(jnp.int32, (tb, D_WIDTH), 0)
    s = 1
    while s < tb:
        valid = row >= s
        u = jnp.where(valid, a * pltpu.roll(u, s, axis=0), 0.0) + u
        a = jnp.where(valid, a * pltpu.roll(a, s, axis=0), a)
        s *= 2
    hs = u + a * hc_ref[0:1, :]
    hc_ref[...] = jnp.broadcast_to(hs[tb - 1:tb, :], hc_ref.shape)
    gelu = 0.5 * y * (1.0 + jnp.tanh(math.sqrt(2.0 / math.pi) * (y + 0.044715 * (y * y * y))))
    o_ref[...] = (gelu * hs).astype(o_ref.dtype)


def _rglru(d3, conv_w, conv_b, w_ri, b_ri, lam):
    bsz, t, _ = d3.shape
    tb = min(SEQ_BLOCK, t)
    return pl.pallas_call(
        functools.partial(_rglru_kernel, tb=tb),
        grid=(bsz, t // tb),
        in_specs=[pl.BlockSpec((None, tb, LRU_W), lambda b, i: (b, i, 0)),
                  _resident((D_CONV, D_WIDTH)), _resident((1, D_WIDTH)),
                  _resident((D_WIDTH, 2 * D_WIDTH)), _resident((1, 2 * D_WIDTH)),
                  _resident((1, D_WIDTH))],
        out_specs=pl.BlockSpec((None, tb, D_WIDTH), lambda b, i: (b, i, 0)),
        out_shape=jax.ShapeDtypeStruct((bsz, t, D_WIDTH), BF16),
        scratch_shapes=[pltpu.VMEM((tb + 8, D_WIDTH), F32), pltpu.VMEM((8, D_WIDTH), F32)],
        compiler_params=_params("parallel", "arbitrary"),
        name="rglru",
    )(d3, conv_w, conv_b, w_ri, b_ri, lam)


def _bucket_edges():
    n = np.arange(REL_MAX_DIST + 1)
    max_exact = REL_BUCKETS // 2
    nf = np.maximum(n, 1).astype(np.float64)
    large = max_exact + (np.log(nf / max_exact) / math.log(REL_MAX_DIST / max_exact)
                         * (REL_BUCKETS - max_exact)).astype(np.int64)
    bucket = np.where(n < max_exact, n, np.minimum(large, REL_BUCKETS - 1))
    assert np.all(np.diff(bucket) >= 0)
    return [int(np.argmax(bucket >= b)) for b in range(REL_BUCKETS)]


def _tbias_kernel(tab_ref, o_ref, *, bs, edges):
    h = pl.program_id(0)
    d = pl.program_id(1)
    row = lax.broadcasted_iota(jnp.int32, (bs, bs), 0)
    col = lax.broadcasted_iota(jnp.int32, (bs, bs), 1)
    dist = d * bs + row - col
    val = jnp.full((bs, bs), tab_ref[0, h], F32)
    for b in range(1, REL_BUCKETS):
        val = jnp.where(dist >= edges[b], tab_ref[b, h], val)
    mult = jnp.zeros((bs, bs), jnp.int32)
    for window, dil in C_PATTERNS:
        hit = jnp.where((dist & (dil - 1)) == 0, 1, 0)
        mult = mult + jnp.where(dist <= window, hit, 0)
    logm = jnp.full((bs, bs), NEG, F32)
    for m in range(1, len(C_PATTERNS) + 1):
        logm = jnp.where(mult == m, math.log(m), logm)
    val = jnp.where(h >= B_HEADS, val + logm, val)
    o_ref[...] = jnp.where(dist >= 0, val * LOG2E, NEG)


def _tbias(rel_bias, t):
    bs = min(ATT_BLOCK, t)
    nd = t // bs
    for _, dil in C_PATTERNS:
        assert dil & (dil - 1) == 0
    return pl.pallas_call(
        functools.partial(_tbias_kernel, bs=bs, edges=_bucket_edges()),
        grid=(N_SOFTMAX_HEADS, nd),
        in_specs=[pl.BlockSpec(memory_space=pltpu.SMEM)],
        out_specs=pl.BlockSpec((None, None, bs, bs), lambda h, d: (h, d, 0, 0)),
        out_shape=jax.ShapeDtypeStruct((N_SOFTMAX_HEADS, nd, bs, bs), F32),
        compiler_params=_params("parallel", "parallel"),
        name="tbias",
    )(rel_bias)


def _attn_kernel(qb_ref, kb_ref, vb_ref, qc_ref, kc_ref, vc_ref, tb_ref, madd_ref,
                 ob_ref, oc_ref, vp_ref, m_ref, acc_ref, *, bs):
    i = pl.program_id(1)
    streams = ((qb_ref, kb_ref, vb_ref, ob_ref), (qc_ref, kc_ref, vc_ref, oc_ref))
    per_stream = B_HEADS
    n_heads = vp_ref.shape[0]
    half = LANES // 2
    assert B_HEAD_DIM == half and C_HEAD_DIM == half and B_HEADS == C_HEADS

    @pl.when(i == 0)
    def _():
        low_t = lax.broadcasted_iota(jnp.int32, (vb_ref.shape[0], LANES), 1) < half
        for h in range(0, n_heads, 2):
            v_ref = streams[h // per_stream][2]
            hp = (h % per_stream) // 2
            pair = v_ref[:, hp * LANES:(hp + 1) * LANES]
            vp_ref[h] = jnp.where(low_t, pair, 1.0).astype(BF16)
            vp_ref[h + 1] = jnp.where(low_t, 1.0, pair).astype(BF16)

    m_ref[...] = jnp.full(m_ref.shape, NEG, F32)
    acc_ref[...] = jnp.zeros(acc_ref.shape, F32)
    low = lax.broadcasted_iota(jnp.int32, (bs, LANES), 1) < half
    q_heads = []
    for h in range(n_heads):
        q_ref = streams[h // per_stream][0]
        hp = (h % per_stream) // 2
        qp = q_ref[:, hp * LANES:(hp + 1) * LANES].astype(F32) * (B_HEAD_DIM ** -0.5 * LOG2E)
        q_heads.append(jnp.where(low if h % 2 == 0 else jnp.logical_not(low), qp, 0.0).astype(BF16))

    def body(c, _):
        start = pl.multiple_of(c * bs, bs)
        extra = madd_ref[c].astype(F32)
        for h in range(n_heads):
            k_ref = streams[h // per_stream][1]
            hp = (h % per_stream) // 2
            ks = k_ref[pl.ds(start, bs), hp * LANES:(hp + 1) * LANES]
            s = lax.dot_general(q_heads[h], ks, NT_DIMS, preferred_element_type=F32)
            s = s + tb_ref[h, i - c]
            if h < per_stream:
                s = s + extra
            m_prev = m_ref[h]
            m_new = jnp.maximum(m_prev, jnp.max(s, axis=1, keepdims=True))
            alpha = jnp.exp2(m_prev - m_new)
            p = jnp.exp2((s - jnp.concatenate([m_new] * (bs // LANES), axis=1)).astype(BF16))
            acc_ref[h] = alpha * acc_ref[h] + jnp.dot(p, vp_ref[h, pl.ds(start, bs), :],
                                                      preferred_element_type=F32)
            m_ref[h] = m_new
        return 0

    lax.fori_loop(0, i + 1, body, 0)

    for h in range(0, n_heads, 2):
        o_ref = streams[h // per_stream][3]
        hp = (h % per_stream) // 2
        a0, a1 = acc_ref[h], acc_ref[h + 1]
        o0 = a0 / pltpu.roll(a0, half, axis=1)
        o1 = a1 / pltpu.roll(a1, half, axis=1)
        o_ref[:, hp * LANES:(hp + 1) * LANES] = jnp.where(low, o0, o1).astype(o_ref.dtype)


def _attention(dsa3, dil3, tbias, madd):
    bsz, t, width3 = dsa3.shape
    width = width3 // 3
    bs = min(ATT_BLOCK, t)
    nd = t // bs
    q_spec = pl.BlockSpec((None, bs, width), lambda b, i: (b, i, 0))
    k_spec = pl.BlockSpec((None, t, width), lambda b, i: (b, 0, 1), pipeline_mode=pl.Buffered(1))
    v_spec = pl.BlockSpec((None, t, width), lambda b, i: (b, 0, 2), pipeline_mode=pl.Buffered(1))
    out_spec = pl.BlockSpec((None, bs, width), lambda b, i: (b, i, 0))
    out_shape = jax.ShapeDtypeStruct((bsz, t, width), BF16)
    return pl.pallas_call(
        functools.partial(_attn_kernel, bs=bs),
        grid=(bsz, nd),
        in_specs=[q_spec, k_spec, v_spec, q_spec, k_spec, v_spec,
                  _resident(tbias.shape),
                  pl.BlockSpec((None, None, nd, bs, bs), lambda b, i: (b, i, 0, 0, 0))],
        out_specs=[out_spec, out_spec],
        out_shape=[out_shape, out_shape],
        scratch_shapes=[pltpu.VMEM((N_SOFTMAX_HEADS, t, LANES), BF16),
                        pltpu.VMEM((N_SOFTMAX_HEADS, bs, LANES), F32),
                        pltpu.VMEM((N_SOFTMAX_HEADS, bs, LANES), F32)],
        compiler_params=_params("arbitrary", "arbitrary"),
        name="attn",
    )(dsa3, dsa3, dsa3, dil3, dil3, dil3, tbias, madd)


def _dsa_select_kernel(iq_ref, ik_ref, iw_ref, madd_ref, key_ref, thr_ref, need_ref,
                       *, bs, nd, k_sel):
    pair = lambda i, c: i * (i + 1) // 2 + c
    lane = lax.broadcasted_iota(jnp.int32, (bs, LANES), 1)
    row = lax.broadcasted_iota(jnp.int32, (bs, bs), 0)
    col = lax.broadcasted_iota(jnp.int32, (bs, bs), 1)
    tiles = bs // LANES

    for i in range(nd):
        iq = iq_ref[i * bs:(i + 1) * bs, :]
        iw = iw_ref[i * bs:(i + 1) * bs, :].astype(F32) * (IDX_HEADS ** -0.5 * IDX_DIM ** -0.5)
        iq_heads = [jnp.where((lane // IDX_DIM) == hh, iq, 0.0).astype(BF16)
                    for hh in range(IDX_HEADS)]
        for c in range(i + 1):
            ikc = ik_ref[c * bs:(c + 1) * bs, :]
            sc = jnp.zeros((bs, bs), F32)
            for hh in range(IDX_HEADS):
                raw = lax.dot_general(iq_heads[hh], ikc, NT_DIMS, preferred_element_type=F32)
                sc = sc + iw[:, hh:hh + 1] * jnp.maximum(raw, 0.0)
            sc = jnp.where(sc == 0.0, 0.0, sc)
            bits = pltpu.bitcast(sc, jnp.int32)
            key = jnp.where(bits < 0, bits ^ 0x7FFFFFFF, bits)
            if c == i:
                key = jnp.where(col <= row, key, INT_MIN)
            key_ref[pair(i, c)] = key

    thr_ref[...] = jnp.full(thr_ref.shape, INT_MIN, jnp.int32)
    kf = float(k_sel)
    n_bits = 32

    def bisect(it, _):
        searching = it < n_bits
        step = jnp.where(searching, lax.shift_left(jnp.int32(1), jnp.maximum(n_bits - 1 - it, 0)), 1)
        for i in range(nd):
            thr = thr_ref[i]
            cand = thr + step
            acc = jnp.zeros((bs, LANES), F32)
            for c in range(i + 1):
                kk = key_ref[pair(i, c)]
                for j in range(tiles):
                    acc = acc + jnp.where(kk[:, j * LANES:(j + 1) * LANES] >= cand, 1.0, 0.0)
            cnt = jnp.sum(acc, axis=1, keepdims=True)
            thr_ref[i] = jnp.where(jnp.logical_and(searching, cnt >= kf), cand, thr)
            need_ref[i] = jnp.broadcast_to(kf - cnt, (bs, LANES))
        return 0

    lax.fori_loop(0, n_bits + 1, bisect, 0)

    prefix = jnp.where(row <= col, 1.0, 0.0).astype(BF16)
    for i in range(nd):
        thr2 = jnp.concatenate([thr_ref[i]] * tiles, axis=1)
        need = need_ref[i][:, 0:1]
        seen = jnp.zeros((bs, 1), F32)
        for c in range(i + 1):
            kk = key_ref[pair(i, c)]
            tie = kk == thr2
            rank = seen + jnp.dot(jnp.where(tie, 1.0, 0.0).astype(BF16), prefix,
                                  preferred_element_type=F32)
            tie_add = jnp.where(tie, jnp.where(rank <= need, 0.0, NEG), NEG)
            madd_ref[i, c] = jnp.where(kk > thr2, 0.0, tie_add).astype(madd_ref.dtype)
            seen = rank[:, bs - 1:bs]
        for c in range(i + 1, nd):
            madd_ref[i, c] = jnp.full((bs, bs), NEG, madd_ref.dtype)


def _dsa_select(idx3):
    bsz, t, _ = idx3.shape
    bs = min(ATT_BLOCK, t)
    nd = t // bs
    k_sel = min(B_TOPK_MAX, t // 4)
    col_block = lambda j: pl.BlockSpec((None, t, LANES), lambda b: (b, 0, j))
    return pl.pallas_call(
        functools.partial(_dsa_select_kernel, bs=bs, nd=nd, k_sel=k_sel),
        grid=(bsz,),
        in_specs=[col_block(0), col_block(1), col_block(2)],
        out_specs=pl.BlockSpec((None, nd, nd, bs, bs), lambda b: (b, 0, 0, 0, 0)),
        out_shape=jax.ShapeDtypeStruct((bsz, nd, nd, bs, bs), BF16),
        scratch_shapes=[pltpu.VMEM((nd * (nd + 1) // 2, bs, bs), jnp.int32),
                        pltpu.VMEM((nd, bs, LANES), jnp.int32),
                        pltpu.VMEM((nd, bs, LANES), F32)],
        compiler_params=_params("parallel"),
        name="dsa_select",
    )(idx3, idx3, idx3)


def _softmax_mixers(dsa3, idx3, dil3, tbias):
    return _attention(dsa3, dil3, tbias, _dsa_select(idx3))


def _route(logits):
    lane = lax.broadcasted_iota(jnp.int32, logits.shape, 1).astype(F32)
    far = float(LANES)

    def first_argmax(vals):
        top = jnp.max(vals, axis=1, keepdims=True)
        return top, jnp.min(jnp.where(vals == top, lane, far), axis=1, keepdims=True)

    is_group = lane < MOE_GROUPS
    g_top, g_sel = first_argmax(jnp.where(is_group, logits, NEG))
    g_w = 1.0 / jnp.sum(jnp.where(is_group, jnp.exp(logits - g_top), 0.0), axis=1, keepdims=True)
    lo = MOE_GROUPS + MOE_EXPERTS_PER_GROUP * g_sel
    e_logits = jnp.where(lane >= lo, jnp.where(lane < lo + MOE_EXPERTS_PER_GROUP, logits, NEG), NEG)
    v1, i1 = first_argmax(e_logits)
    v2, i2 = first_argmax(jnp.where(lane == i1, NEG, e_logits))
    e2 = jnp.exp(v2 - v1)
    w1 = 1.0 / (1.0 + e2)
    return jnp.where(lane == i1, w1 * g_w, jnp.where(lane == i2, e2 * w1 * g_w, 0.0))


def _merge_kernel(h_ref, g1_ref, oa_ref, ob_ref, oc_ref, od_ref, wg_ref, bg_ref,
                  wba_ref, wbb_ref, wbc_ref, wbd_ref, wo_ref, g2_ref, wr_ref, br_ref,
                  h1_ref, xn2_ref, gates_ref):
    h = h_ref[...]
    xn = _rms(h, g1_ref[...]).astype(BF16)
    merged = None
    for gi, (o_ref, wb_ref) in enumerate(((oa_ref, wba_ref), (ob_ref, wbb_ref),
                                          (oc_ref, wbc_ref), (od_ref, wbd_ref))):
        gate = _sigmoid(jnp.dot(xn, wg_ref[gi], preferred_element_type=F32) + bg_ref[gi:gi + 1, :])
        term = gate * jnp.dot(o_ref[...], wb_ref[...], preferred_element_type=F32)
        merged = term if merged is None else merged + term
    h1 = h + jnp.dot(merged.astype(BF16), wo_ref[...], preferred_element_type=F32)
    h1_ref[...] = h1
    xn2 = _rms(h1, g2_ref[...]).astype(BF16)
    xn2_ref[...] = xn2
    logits = jnp.dot(xn2, wr_ref[...], preferred_element_type=F32) + br_ref[...]
    gates_ref[...] = _route(logits)


def _merge(h, g1, oa, ob, oc, od, wg, bg, wba, wbb, wbc, wbd, wo, g2, wr, br):
    n = h.shape[0]
    tm = min(PROJ_ROWS, n)
    rows = lambda w: pl.BlockSpec((tm, w), lambda i: (i, 0))
    return pl.pallas_call(
        _merge_kernel,
        grid=(n // tm,),
        in_specs=[rows(D_MODEL), _resident(g1.shape),
                  rows(oa.shape[1]), rows(ob.shape[1]), rows(oc.shape[1]), rows(od.shape[1]),
                  _resident(wg.shape), _resident(bg.shape),
                  _resident(wba.shape), _resident(wbb.shape), _resident(wbc.shape),
                  _resident(wbd.shape), _resident(wo.shape), _resident(g2.shape),
                  _resident(wr.shape), _resident(br.shape)],
        out_specs=[rows(D_MODEL), rows(D_MODEL), rows(LANES)],
        out_shape=[jax.ShapeDtypeStruct((n, D_MODEL), F32),
                   jax.ShapeDtypeStruct((n, D_MODEL), BF16),
                   jax.ShapeDtypeStruct((n, LANES), F32)],
        compiler_params=_params("parallel"),
        name="merge",
    )(h, g1, oa, ob, oc, od, wg, bg, wba, wbb, wbc, wbd, wo, g2, wr, br)


def _moe_kernel(x_ref, gates_ref, h1_ref, wg_ref, wu_ref, wd_ref, gf_ref, o_ref,
                xs_ref, ys_ref, gs_ref, pos_ref, win_ref, *, final_norm, tm):
    e = pl.program_id(1)
    grp = e // MOE_EXPERTS_PER_GROUP
    ts = tm + MOE_GROUPS * MOE_ALIGN

    @pl.when(e == 0)
    def _():
        gates = gates_ref[...]
        lane = lax.broadcasted_iota(jnp.int32, (tm, LANES), 1)
        member = jnp.zeros((tm, LANES), F32)
        for gg in range(MOE_GROUPS):
            lo = MOE_GROUPS + gg * MOE_EXPERTS_PER_GROUP
            in_g = jnp.where(lane >= lo, jnp.where(lane < lo + MOE_EXPERTS_PER_GROUP, gates, 0.0), 0.0)
            total = jnp.sum(in_g, axis=1, keepdims=True)
            member = jnp.where(lane == gg, jnp.where(total > 0.0, 1.0, 0.0), member)
        row_sq = lax.broadcasted_iota(jnp.int32, (tm, tm), 0)
        col_sq = lax.broadcasted_iota(jnp.int32, (tm, tm), 1)
        before = jnp.dot(jnp.where(col_sq < row_sq, 1.0, 0.0).astype(BF16), member.astype(BF16),
                         preferred_element_type=F32)
        count = before[tm - 1:tm, :] + member[tm - 1:tm, :]
        padded = jnp.ceil(count * (1.0 / MOE_ALIGN)) * MOE_ALIGN
        r128 = lax.broadcasted_iota(jnp.int32, (LANES, LANES), 0)
        c128 = lax.broadcasted_iota(jnp.int32, (LANES, LANES), 1)
        first = jnp.dot(jnp.broadcast_to(padded, (8, LANES)), jnp.where(r128 < c128, 1.0, 0.0),
                        precision=HIGHEST, preferred_element_type=F32)[0:1, :]
        slot = member * (first + before)
        pos_ref[...] = jnp.broadcast_to(jnp.sum(slot, axis=1, keepdims=True), (tm, LANES))
        pos_t = lax.dot_general(jnp.ones((8, LANES), F32), slot, NT_DIMS, precision=HIGHEST,
                                preferred_element_type=F32)[0:1, :]
        lane_row = lax.broadcasted_iota(jnp.int32, (1, LANES), 1)
        for gg in range(MOE_GROUPS):
            seg_start = jnp.sum(jnp.where(lane_row == gg, first, 0.0)).astype(jnp.int32)
            seg_rows = jnp.sum(jnp.where(lane_row == gg, count, 0.0)).astype(jnp.int32)
            win_ref[gg] = seg_start
            win_ref[MOE_GROUPS + gg] = (seg_rows + MOE_WINDOW - 1) // MOE_WINDOW
        row_s = lax.broadcasted_iota(jnp.int32, (ts, tm), 0)
        perm = jnp.where(pos_t == row_s.astype(F32), 1.0, 0.0).astype(BF16)
        xs_ref[0:ts, :] = jnp.dot(perm, x_ref[...], preferred_element_type=F32).astype(BF16)
        xs_ref[ts:, :] = jnp.zeros((xs_ref.shape[0] - ts, D_MODEL), BF16)
        g_hi = gates.astype(BF16)
        g_lo = (gates - g_hi.astype(F32)).astype(BF16)
        gs_ref[0:ts, :] = (jnp.dot(perm, g_hi, preferred_element_type=F32)
                           + jnp.dot(perm, g_lo, preferred_element_type=F32))
        gs_ref[ts:, :] = jnp.zeros((gs_ref.shape[0] - ts, LANES), F32)
        ys_ref[...] = jnp.zeros(ys_ref.shape, F32)

    lane_w = lax.broadcasted_iota(jnp.int32, (MOE_WINDOW, LANES), 1)

    def window(k, _):
        start = pl.multiple_of(win_ref[grp] + k * MOE_WINDOW, MOE_ALIGN)
        xw = xs_ref[pl.ds(start, MOE_WINDOW), :]
        hg = jnp.dot(xw, wg_ref[...], preferred_element_type=F32)
        hu = jnp.dot(xw, wu_ref[...], preferred_element_type=F32)
        hid = (hg * _sigmoid(hg) * hu).astype(BF16)
        y = jnp.dot(hid, wd_ref[...], preferred_element_type=F32)
        gate = jnp.sum(jnp.where(lane_w == e + MOE_GROUPS, gs_ref[pl.ds(start, MOE_WINDOW), :], 0.0),
                       axis=1, keepdims=True)
        ys_ref[pl.ds(start, MOE_WINDOW), :] += gate * y
        return 0

    lax.fori_loop(0, win_ref[MOE_GROUPS + grp], window, 0)

    @pl.when(e == N_EXPERTS - 1)
    def _():
        pos_s = jnp.concatenate([pos_ref[...]] * (ts // LANES), axis=1)
        col_s = lax.broadcasted_iota(jnp.int32, (tm, ts), 1)
        unperm = jnp.where(pos_s == col_s.astype(F32), 1.0, 0.0).astype(BF16)
        ys = ys_ref[0:ts, :]
        y_hi = ys.astype(BF16)
        y_lo = (ys - y_hi.astype(F32)).astype(BF16)
        out = h1_ref[...] + (jnp.dot(unperm, y_hi, preferred_element_type=F32)
                             + jnp.dot(unperm, y_lo, preferred_element_type=F32))
        o_ref[...] = _rms(out, gf_ref[...]) if final_norm else out


def _moe(xn2, gates, h1, wg, wu, wd, gf, final_norm):
    n = xn2.shape[0]
    tm = min(MOE_ROWS, n)
    sorted_rows = tm + MOE_GROUPS * MOE_ALIGN + MOE_WINDOW
    rows = lambda w: pl.BlockSpec((tm, w), lambda i, e: (i, 0))
    return pl.pallas_call(
        functools.partial(_moe_kernel, final_norm=final_norm, tm=tm),
        grid=(n // tm, N_EXPERTS),
        in_specs=[rows(D_MODEL), rows(LANES), rows(D_MODEL),
                  pl.BlockSpec((None, D_MODEL, MOE_HIDDEN), lambda i, e: (e, 0, 0)),
                  pl.BlockSpec((None, D_MODEL, MOE_HIDDEN), lambda i, e: (e, 0, 0)),
                  pl.BlockSpec((None, MOE_HIDDEN, D_MODEL), lambda i, e: (e, 0, 0)),
                  pl.BlockSpec((1, D_MODEL), lambda i, e: (0, 0))],
        out_specs=rows(D_MODEL),
        out_shape=jax.ShapeDtypeStruct((n, D_MODEL), F32),
        scratch_shapes=[pltpu.VMEM((sorted_rows, D_MODEL), BF16),
                        pltpu.VMEM((sorted_rows, D_MODEL), F32),
                        pltpu.VMEM((sorted_rows, LANES), F32),
                        pltpu.VMEM((tm, LANES), F32),
                        pltpu.SMEM((2 * MOE_GROUPS,), jnp.int32)],
        compiler_params=_params("parallel", "arbitrary"),
        name="moe",
    )(xn2, gates, h1, wg, wu, wd, gf)


def _inproj_columns():
    offs = np.concatenate([[0], np.cumsum(IN_SPLITS)])
    (aq, ak, av, ag, alr, bq, bk, bv, iq, ik, iw, cq, ck, cv, dy, dx) = offs[:-1]
    cols = []

    def per_head(base, width):
        for h in range(A_HEADS):
            cols.extend(list(range(base + h * width, base + (h + 1) * width)) + [-1] * (LANES - width))

    per_head(aq, A_DK)
    per_head(ak, A_DK)
    per_head(av, A_DV)
    per_head(ag, A_DV)
    cols.extend(list(range(alr, alr + A_LOWRANK)) + [-1] * (LANES - A_LOWRANK))
    cols.extend(range(bq, bq + ATT_W))
    cols.extend(range(iq, iq + IDX_HEADS * IDX_DIM))
    cols.extend(list(range(ik, ik + IDX_DIM)) * IDX_HEADS)
    cols.extend(list(range(iw, iw + IDX_HEADS)) + [-1] * (LANES - IDX_HEADS))
    cols.extend(range(cq, cq + ATT_W))
    cols.extend(range(dy, dy + LRU_W))
    cols = np.asarray(cols, np.int32)
    assert cols.shape[0] == sum(GROUP_WIDTHS)
    return cols


def _pad_heads(a, width, axis):
    a = jnp.moveaxis(a, axis, -1)
    a = a.reshape(a.shape[:-1] + (A_HEADS, width))
    a = jnp.pad(a, [(0, 0)] * (a.ndim - 1) + [(0, LANES - width)])
    return jnp.moveaxis(a.reshape(a.shape[:-2] + (A_HEADS * LANES,)), -1, axis)


def _block_diag(w):
    nb, bw, _ = w.shape
    eye = jnp.eye(nb, dtype=w.dtype)
    return jnp.einsum('ncd,nm->ncmd', w, eye).reshape(nb * bw, nb * bw)


def kernel(x, w_in, a_w2, a_b2, a_gain, conv_w, conv_b, lru_wr, lru_br, lru_wi, lru_bi, lru_lambda, w_gate, b_gate, w_branch, w_out, rel_bias, norm1, norm2, norm_f, moe_wrg, moe_brg, moe_wre, moe_bre, moe_wg, moe_wu, moe_wd):
    bsz, t, _ = x.shape
    n = bsz * t
    depth = w_in.shape[0]
    cols = _inproj_columns()
    tbias = _tbias(rel_bias, t)
    h = x.reshape(n, D_MODEL)
    for l in range(depth):
        w_all = jnp.where(cols[None, :] >= 0, w_in[l][:, np.maximum(cols, 0)], 0.0).astype(BF16)
        gla_in, dsa_in, idx_in, dil_in, lru_in = _inproj(h, norm1[l][None, :], w_all)

        wa2 = jnp.pad(_pad_heads(a_w2[l], A_DK, 1), ((0, LANES - A_LOWRANK), (0, 0)))
        o_a = _gla(gla_in.reshape(bsz, t, GLA_W), wa2, _pad_heads(a_b2[l], A_DK, 0)[None, :],
                   _pad_heads(a_gain[l], A_DV, 0)[None, :])
        o_b, o_c = _softmax_mixers(dsa_in.reshape(bsz, t, ATT_W), idx_in.reshape(bsz, t, IDX_W),
                                   dil_in.reshape(bsz, t, ATT_W), tbias)
        w_ri = jnp.concatenate([_block_diag(lru_wr[l]), _block_diag(lru_wi[l])], axis=1).astype(BF16)
        b_ri = jnp.concatenate([lru_br[l], lru_bi[l]])[None, :]
        o_d = _rglru(lru_in.reshape(bsz, t, LRU_W), conv_w[l], conv_b[l][None, :], w_ri, b_ri,
                     lru_lambda[l][None, :])

        w_router = jnp.pad(jnp.concatenate([moe_wrg[l], moe_wre[l]], axis=1),
                           ((0, 0), (0, LANES - MOE_GROUPS - N_EXPERTS)))
        b_router = jnp.pad(jnp.concatenate([moe_brg[l], moe_bre[l]]),
                           (0, LANES - MOE_GROUPS - N_EXPERTS))[None, :]
        h1, xn2, gates = _merge(
            h, norm1[l][None, :], o_a.reshape(n, -1), o_b.reshape(n, -1), o_c.reshape(n, -1),
            o_d.reshape(n, -1), w_gate[l].astype(BF16), b_gate[l],
            _pad_heads(w_branch[l, 0], A_DV, 0).astype(BF16), w_branch[l, 1].astype(BF16),
            w_branch[l, 2].astype(BF16), w_branch[l, 3].astype(BF16), w_out[l].astype(BF16),
            norm2[l][None, :], w_router.astype(BF16), b_router)
        h = _moe(xn2, gates, h1, moe_wg[l].astype(BF16), moe_wu[l].astype(BF16),
                 moe_wd[l].astype(BF16), norm_f[None, :], final_norm=(l == depth - 1))
    return h.reshape(bsz, t, D_MODEL)
```

```python
import functools
import math

import numpy as np
import jax
import jax.numpy as jnp
from jax import lax
from jax.experimental import pallas as pl
from jax.experimental.pallas import tpu as pltpu

F32 = jnp.float32
BF16 = jnp.bfloat16
HIGHEST = lax.Precision.HIGHEST

D_MODEL = 1024
EPS = 1e-6
A_HEADS, A_DK, A_DV, A_LOWRANK, A_GATE_TAU, A_CHUNK = 4, 48, 96, 16, 16.0, 64
B_HEADS, B_HEAD_DIM, IDX_HEADS, IDX_DIM, B_TOPK_MAX = 6, 64, 4, 32, 256
C_HEADS, C_HEAD_DIM = 6, 64
C_PATTERNS = ((128, 1), (512, 4), (2048, 16))
D_WIDTH, D_BLOCKS, D_CONV, D_C = 384, 8, 4, 8.0
REL_BUCKETS, REL_MAX_DIST = 32, 2048
N_SOFTMAX_HEADS = B_HEADS + C_HEADS
MOE_GROUPS, MOE_EXPERTS_PER_GROUP, MOE_HIDDEN = 4, 4, 512
N_EXPERTS = MOE_GROUPS * MOE_EXPERTS_PER_GROUP
IN_SPLITS = (A_HEADS * A_DK, A_HEADS * A_DK, A_HEADS * A_DV, A_HEADS * A_DV, A_LOWRANK,
             B_HEADS * B_HEAD_DIM, B_HEADS * B_HEAD_DIM, B_HEADS * B_HEAD_DIM,
             IDX_HEADS * IDX_DIM, IDX_DIM, IDX_HEADS,
             C_HEADS * C_HEAD_DIM, C_HEADS * C_HEAD_DIM, C_HEADS * C_HEAD_DIM,
             D_WIDTH, D_WIDTH)

LANES = 128
VMEM_LIMIT = 56 * 1024 * 1024
NEG = -1e30
INT_MIN = -2 ** 31
LOG2E = math.log2(math.e)

ATT_BLOCK = 256
SEQ_BLOCK = 256
PROJ_ROWS = 512
MOE_ROWS = 1024
MOE_WINDOW = 288
MOE_ALIGN = 32

GLA_W = 4 * A_HEADS * LANES + LANES
ATT_W = 3 * B_HEADS * B_HEAD_DIM
IDX_W = 3 * LANES
LRU_W = 2 * D_WIDTH
GROUP_WIDTHS = (GLA_W, ATT_W, IDX_W, ATT_W, LRU_W)
NT_DIMS = (((1,), (1,)), ((), ()))
TN_DIMS = (((0,), (0,)), ((), ()))


def _params(*sem):
    return pltpu.CompilerParams(dimension_semantics=sem, vmem_limit_bytes=VMEM_LIMIT)


def _resident(shape):
    zeros = (0,) * len(shape)
    return pl.BlockSpec(shape, lambda *_: zeros, pipeline_mode=pl.Buffered(1))


def _rms(x, gain):
    return x * lax.rsqrt(jnp.mean(x * x, axis=-1, keepdims=True) + EPS) * gain


def _sigmoid(x):
    return 1.0 / (1.0 + jnp.exp(-x))


def _softplus(x):
    return jnp.maximum(x, 0.0) + jnp.log1p(jnp.exp(-jnp.abs(x)))


def _inproj_kernel(x_ref, g_ref, w_ref, *out_refs):
    xn = _rms(x_ref[...], g_ref[...]).astype(BF16)
    off = 0
    for o_ref in out_refs:
        width = o_ref.shape[-1]
        o_ref[...] = jnp.dot(xn, w_ref[:, off:off + width],
                             preferred_element_type=F32).astype(o_ref.dtype)
        off += width


def _inproj(h, gain, w_all):
    n = h.shape[0]
    tm = min(PROJ_ROWS, n)
    total = sum(GROUP_WIDTHS)
    return pl.pallas_call(
        _inproj_kernel,
        grid=(n // tm,),
        in_specs=[pl.BlockSpec((tm, D_MODEL), lambda i: (i, 0)),
                  _resident((1, D_MODEL)),
                  _resident((D_MODEL, total))],
        out_specs=[pl.BlockSpec((tm, w), lambda i: (i, 0)) for w in GROUP_WIDTHS],
        out_shape=[jax.ShapeDtypeStruct((n, w), BF16) for w in GROUP_WIDTHS],
        compiler_params=_params("parallel"),
        name="inproj",
    )(h, gain, w_all)


def _gla_kernel(x_ref, wa2_ref, ba2_ref, gain_ref, o_ref, st_ref, *, tb):
    @pl.when(pl.program_id(1) == 0)
    def _():
        st_ref[...] = jnp.zeros_like(st_ref)

    hw = A_HEADS * LANES
    x = x_ref[...]
    z = jnp.dot(x[:, 4 * hw:].astype(F32), wa2_ref[...], precision=HIGHEST,
                preferred_element_type=F32) + ba2_ref[...]
    log_a = -_softplus(-z) * (1.0 / A_GATE_TAU)

    r64 = lax.broadcasted_iota(jnp.int32, (A_CHUNK, A_CHUNK), 0)
    c64 = lax.broadcasted_iota(jnp.int32, (A_CHUNK, A_CHUNK), 1)
    causal = c64 <= r64
    tri = jnp.where(causal, 1.0, 0.0).astype(BF16)
    la_hi = log_a.astype(BF16)
    la_lo = (log_a - la_hi.astype(F32)).astype(BF16)
    b_parts, b_last_parts = [], []
    for c in range(tb // A_CHUNK):
        rs = slice(c * A_CHUNK, (c + 1) * A_CHUNK)
        bc = (jnp.dot(tri, la_hi[rs], preferred_element_type=F32)
              + jnp.dot(tri, la_lo[rs], preferred_element_type=F32))
        b_parts.append(bc)
        b_last_parts.append(jnp.broadcast_to(bc[A_CHUNK - 1:A_CHUNK, :], bc.shape))
    b = jnp.concatenate(b_parts, axis=0)
    b_last = jnp.concatenate(b_last_parts, axis=0)

    q = x[:, 0:hw].astype(F32) * (A_DK ** -0.5)
    k = x[:, hw:2 * hw].astype(F32)
    v = x[:, 2 * hw:3 * hw]
    q_dec = (q * jnp.exp(b)).astype(BF16)
    k_dec = (k * jnp.exp(-b)).astype(BF16)
    k_end = (k * jnp.exp(b_last - b)).astype(BF16)
    decay = jnp.exp(b_last)

    heads = []
    for h in range(A_HEADS):
        cs = slice(h * LANES, (h + 1) * LANES)
        state_t = st_ref[h]
        chunks = []
        for c in range(tb // A_CHUNK):
            rs = slice(c * A_CHUNK, (c + 1) * A_CHUNK)
            qc, kc, kec, vc = q_dec[rs, cs], k_dec[rs, cs], k_end[rs, cs], v[rs, cs]
            att = lax.dot_general(qc, kc, NT_DIMS, preferred_element_type=F32)
            att = jnp.where(causal, att, 0.0).astype(BF16)
            oc = jnp.dot(att, vc, preferred_element_type=F32)
            oc += lax.dot_general(qc, state_t.astype(BF16), NT_DIMS, preferred_element_type=F32)
            inc_t = lax.dot_general(vc, kec, TN_DIMS, preferred_element_type=F32)
            state_t = state_t * decay[c * A_CHUNK:c * A_CHUNK + 1, cs] + inc_t
            chunks.append(oc)
        st_ref[h] = state_t
        o_h = jnp.concatenate(chunks, axis=0)
        ms = jnp.sum(o_h * o_h, axis=-1, keepdims=True) * (1.0 / A_DV)
        heads.append(o_h * lax.rsqrt(ms + EPS))
    o = jnp.concatenate(heads, axis=1) * gain_ref[...]
    g = x[:, 3 * hw:4 * hw].astype(F32)
    o_ref[...] = (o * (g * _sigmoid(g))).astype(o_ref.dtype)


def _gla(x3, wa2, ba2, gain):
    bsz, t, _ = x3.shape
    tb = min(SEQ_BLOCK, t)
    hw = A_HEADS * LANES
    return pl.pallas_call(
        functools.partial(_gla_kernel, tb=tb),
        grid=(bsz, t // tb),
        in_specs=[pl.BlockSpec((None, tb, GLA_W), lambda b, i: (b, i, 0)),
                  _resident((LANES, hw)), _resident((1, hw)), _resident((1, hw))],
        out_specs=pl.BlockSpec((None, tb, hw), lambda b, i: (b, i, 0)),
        out_shape=jax.ShapeDtypeStruct((bsz, t, hw), BF16),
        scratch_shapes=[pltpu.VMEM((A_HEADS, LANES, LANES), F32)],
        compiler_params=_params("parallel", "arbitrary"),
        name="gla",
    )(x3, wa2, ba2, gain)


def _rglru_kernel(d_ref, cw_ref, cb_ref, wri_ref, bri_ref, lam_ref, o_ref, xbuf, hc_ref, *, tb):
    halo = 8

    @pl.when(pl.program_id(1) == 0)
    def _():
        xbuf[0:halo, :] = jnp.zeros((halo, D_WIDTH), F32)
        hc_ref[...] = jnp.zeros_like(hc_ref)

    d = d_ref[...]
    y = d[:, 0:D_WIDTH].astype(F32)
    x = d[:, D_WIDTH:].astype(F32)
    xbuf[halo:halo + tb, :] = x
    xc = cb_ref[...]
    for j in range(D_CONV):
        xc = xc + cw_ref[j:j + 1, :] * xbuf[pl.ds(halo - (D_CONV - 1) + j, tb), :]
    xbuf[0:halo, :] = x[tb - halo:tb, :]

    ri = jnp.dot(xc.astype(BF16), wri_ref[...], preferred_element_type=F32) + bri_ref[...]
    r = _sigmoid(ri[:, 0:D_WIDTH])
    ig = _sigmoid(ri[:, D_WIDTH:])
    log_a = (-D_C) * r * _softplus(-lam_ref[...])
    a = jnp.exp(log_a)
    u = jnp.sqrt(-jnp.tanh(log_a) * (a * a + 1.0)) * (ig * xc)

    row = lax.broadcasted_iota(jnp.int32, (tb, D_WIDTH), 0)
    s = 1
    while s < tb:
        valid = row >= s
        u = jnp.where(valid, a * pltpu.roll(u, s, axis=0), 0.0) + u
        a = jnp.where(valid, a * pltpu.roll(a, s, axis=0), a)
        s *= 2
    hs = u + a * hc_ref[0:1, :]
    hc_ref[...] = jnp.broadcast_to(hs[tb - 1:tb, :], hc_ref.shape)
    gelu = 0.5 * y * (1.0 + jnp.tanh(math.sqrt(2.0 / math.pi) * (y + 0.044715 * (y * y * y))))
    o_ref[...] = (gelu * hs).astype(o_ref.dtype)


def _rglru(d3, conv_w, conv_b, w_ri, b_ri, lam):
    bsz, t, _ = d3.shape
    tb = min(SEQ_BLOCK, t)
    return pl.pallas_call(
        functools.partial(_rglru_kernel, tb=tb),
        grid=(bsz, t // tb),
        in_specs=[pl.BlockSpec((None, tb, LRU_W), lambda b, i: (b, i, 0)),
                  _resident((D_CONV, D_WIDTH)), _resident((1, D_WIDTH)),
                  _resident((D_WIDTH, 2 * D_WIDTH)), _resident((1, 2 * D_WIDTH)),
                  _resident((1, D_WIDTH))],
        out_specs=pl.BlockSpec((None, tb, D_WIDTH), lambda b, i: (b, i, 0)),
        out_shape=jax.ShapeDtypeStruct((bsz, t, D_WIDTH), BF16),
        scratch_shapes=[pltpu.VMEM((tb + 8, D_WIDTH), F32), pltpu.VMEM((8, D_WIDTH), F32)],
        compiler_params=_params("parallel", "arbitrary"),
        name="rglru",
    )(d3, conv_w, conv_b, w_ri, b_ri, lam)


def _bucket_edges():
    n = np.arange(REL_MAX_DIST + 1)
    max_exact = REL_BUCKETS // 2
    nf = np.maximum(n, 1).astype(np.float64)
    large = max_exact + (np.log(nf / max_exact) / math.log(REL_MAX_DIST / max_exact)
                         * (REL_BUCKETS - max_exact)).astype(np.int64)
    bucket = np.where(n < max_exact, n, np.minimum(large, REL_BUCKETS - 1))
    assert np.all(np.diff(bucket) >= 0)
    return [int(np.argmax(bucket >= b)) for b in range(REL_BUCKETS)]


def _tbias_kernel(tab_ref, o_ref, *, bs, edges):
    h = pl.program_id(0)
    d = pl.program_id(1)
    row = lax.broadcasted_iota(jnp.int32, (bs, bs), 0)
    col = lax.broadcasted_iota(jnp.int32, (bs, bs), 1)
    dist = d * bs + row - col
    val = jnp.full((bs, bs), tab_ref[0, h], F32)
    for b in range(1, REL_BUCKETS):
        val = jnp.where(dist >= edges[b], tab_ref[b, h], val)
    mult = jnp.zeros((bs, bs), jnp.int32)
    for window, dil in C_PATTERNS:
        hit = jnp.where((dist & (dil - 1)) == 0, 1, 0)
        mult = mult + jnp.where(dist <= window, hit, 0)
    logm = jnp.full((bs, bs), NEG, F32)
    for m in range(1, len(C_PATTERNS) + 1):
        logm = jnp.where(mult == m, math.log(m), logm)
    val = jnp.where(h >= B_HEADS, val + logm, val)
    o_ref[...] = jnp.where(dist >= 0, val * LOG2E, NEG)


def _tbias(rel_bias, t):
    bs = min(ATT_BLOCK, t)
    nd = t // bs
    for _, dil in C_PATTERNS:
        assert dil & (dil - 1) == 0
    return pl.pallas_call(
        functools.partial(_tbias_kernel, bs=bs, edges=_bucket_edges()),
        grid=(N_SOFTMAX_HEADS, nd),
        in_specs=[pl.BlockSpec(memory_space=pltpu.SMEM)],
        out_specs=pl.BlockSpec((None, None, bs, bs), lambda h, d: (h, d, 0, 0)),
        out_shape=jax.ShapeDtypeStruct((N_SOFTMAX_HEADS, nd, bs, bs), F32),
        compiler_params=_params("parallel", "parallel"),
        name="tbias",
    )(rel_bias)


def _attn_kernel(qb_ref, kb_ref, vb_ref, qc_ref, kc_ref, vc_ref, tb_ref, madd_ref,
                 ob_ref, oc_ref, vp_ref, m_ref, acc_ref, *, bs):
    i = pl.program_id(1)
    streams = ((qb_ref, kb_ref, vb_ref, ob_ref), (qc_ref, kc_ref, vc_ref, oc_ref))
    per_stream = B_HEADS
    n_heads = vp_ref.shape[0]
    half = LANES // 2
    assert B_HEAD_DIM == half and C_HEAD_DIM == half and B_HEADS == C_HEADS

    @pl.when(i == 0)
    def _():
        low_t = lax.broadcasted_iota(jnp.int32, (vb_ref.shape[0], LANES), 1) < half
        for h in range(0, n_heads, 2):
            v_ref = streams[h // per_stream][2]
            hp = (h % per_stream) // 2
            pair = v_ref[:, hp * LANES:(hp + 1) * LANES]
            vp_ref[h] = jnp.where(low_t, pair, 1.0).astype(BF16)
            vp_ref[h + 1] = jnp.where(low_t, 1.0, pair).astype(BF16)

    m_ref[...] = jnp.full(m_ref.shape, NEG, F32)
    acc_ref[...] = jnp.zeros(acc_ref.shape, F32)
    low = lax.broadcasted_iota(jnp.int32, (bs, LANES), 1) < half
    q_heads = []
    for h in range(n_heads):
        q_ref = streams[h // per_stream][0]
        hp = (h % per_stream) // 2
        qp = q_ref[:, hp * LANES:(hp + 1) * LANES].astype(F32) * (B_HEAD_DIM ** -0.5 * LOG2E)
        q_heads.append(jnp.where(low if h % 2 == 0 else jnp.logical_not(low), qp, 0.0).astype(BF16))

    def body(c, _):
        start = pl.multiple_of(c * bs, bs)
        extra = madd_ref[c].astype(F32)
        for h in range(n_heads):
            k_ref = streams[h // per_stream][1]
            hp = (h % per_stream) // 2
            ks = k_ref[pl.ds(start, bs), hp * LANES:(hp + 1) * LANES]
            s = lax.dot_general(q_heads[h], ks, NT_DIMS, preferred_element_type=F32)
            s = s + tb_ref[h, i - c]
            if h < per_stream:
                s = s + extra
            m_prev = m_ref[h]
            m_new = jnp.maximum(m_prev, jnp.max(s, axis=1, keepdims=True))
            alpha = jnp.exp2(m_prev - m_new)
            p = jnp.exp2((s - jnp.concatenate([m_new] * (bs // LANES), axis=1)).astype(BF16))
            acc_ref[h] = alpha * acc_ref[h] + jnp.dot(p, vp_ref[h, pl.ds(start, bs), :],
                                                      preferred_element_type=F32)
            m_ref[h] = m_new
        return 0

    lax.fori_loop(0, i + 1, body, 0)

    for h in range(0, n_heads, 2):
        o_ref = streams[h // per_stream][3]
        hp = (h % per_stream) // 2
        a0, a1 = acc_ref[h], acc_ref[h + 1]
        o0 = a0 / pltpu.roll(a0, half, axis=1)
        o1 = a1 / pltpu.roll(a1, half, axis=1)
        o_ref[:, hp * LANES:(hp + 1) * LANES] = jnp.where(low, o0, o1).astype(o_ref.dtype)


def _attention(dsa3, dil3, tbias, madd):
    bsz, t, width3 = dsa3.shape
    width = width3 // 3
    bs = min(ATT_BLOCK, t)
    nd = t // bs
    q_spec = pl.BlockSpec((None, bs, width), lambda b, i: (b, i, 0))
    k_spec = pl.BlockSpec((None, t, width), lambda b, i: (b, 0, 1), pipeline_mode=pl.Buffered(1))
    v_spec = pl.BlockSpec((None, t, width), lambda b, i: (b, 0, 2), pipeline_mode=pl.Buffered(1))
    out_spec = pl.BlockSpec((None, bs, width), lambda b, i: (b, i, 0))
    out_shape = jax.ShapeDtypeStruct((bsz, t, width), BF16)
    return pl.pallas_call(
        functools.partial(_attn_kernel, bs=bs),
        grid=(bsz, nd),
        in_specs=[q_spec, k_spec, v_spec, q_spec, k_spec, v_spec,
                  _resident(tbias.shape),
                  pl.BlockSpec((None, None, nd, bs, bs), lambda b, i: (b, i, 0, 0, 0))],
        out_specs=[out_spec, out_spec],
        out_shape=[out_shape, out_shape],
        scratch_shapes=[pltpu.VMEM((N_SOFTMAX_HEADS, t, LANES), BF16),
                        pltpu.VMEM((N_SOFTMAX_HEADS, bs, LANES), F32),
                        pltpu.VMEM((N_SOFTMAX_HEADS, bs, LANES), F32)],
        compiler_params=_params("arbitrary", "arbitrary"),
        name="attn",
    )(dsa3, dsa3, dsa3, dil3, dil3, dil3, tbias, madd)


def _dsa_select_kernel(iq_ref, ik_ref, iw_ref, madd_ref, key_ref, thr_ref, need_ref,
                       *, bs, nd, k_sel):
    pair = lambda i, c: i * (i + 1) // 2 + c
    lane = lax.broadcasted_iota(jnp.int32, (bs, LANES), 1)
    row = lax.broadcasted_iota(jnp.int32, (bs, bs), 0)
    col = lax.broadcasted_iota(jnp.int32, (bs, bs), 1)
    tiles = bs // LANES

    for i in range(nd):
        iq = iq_ref[i * bs:(i + 1) * bs, :]
        iw = iw_ref[i * bs:(i + 1) * bs, :].astype(F32) * (IDX_HEADS ** -0.5 * IDX_DIM ** -0.5)
        iq_heads = [jnp.where((lane // IDX_DIM) == hh, iq, 0.0).astype(BF16)
                    for hh in range(IDX_HEADS)]
        for c in range(i + 1):
            ikc = ik_ref[c * bs:(c + 1) * bs, :]
            sc = jnp.zeros((bs, bs), F32)
            for hh in range(IDX_HEADS):
                raw = lax.dot_general(iq_heads[hh], ikc, NT_DIMS, preferred_element_type=F32)
                sc = sc + iw[:, hh:hh + 1] * jnp.maximum(raw, 0.0)
            sc = jnp.where(sc == 0.0, 0.0, sc)
            bits = pltpu.bitcast(sc, jnp.int32)
            key = jnp.where(bits < 0, bits ^ 0x7FFFFFFF, bits)
            if c == i:
                key = jnp.where(col <= row, key, INT_MIN)
            key_ref[pair(i, c)] = key

    thr_ref[...] = jnp.full(thr_ref.shape, INT_MIN, jnp.int32)
    kf = float(k_sel)
    n_bits = 32

    def bisect(it, _):
        searching = it < n_bits
        step = jnp.where(searching, lax.shift_left(jnp.int32(1), jnp.maximum(n_bits - 1 - it, 0)), 1)
        for i in range(nd):
            thr = thr_ref[i]
            cand = thr + step
            acc = jnp.zeros((bs, LANES), F32)
            for c in range(i + 1):
                kk = key_ref[pair(i, c)]
                for j in range(tiles):
                    acc = acc + jnp.where(kk[:, j * LANES:(j + 1) * LANES] >= cand, 1.0, 0.0)
            cnt = jnp.sum(acc, axis=1, keepdims=True)
            thr_ref[i] = jnp.where(jnp.logical_and(searching, cnt >= kf), cand, thr)
            need_ref[i] = jnp.broadcast_to(kf - cnt, (bs, LANES))
        return 0

    lax.fori_loop(0, n_bits + 1, bisect, 0)

    prefix = jnp.where(row <= col, 1.0, 0.0).astype(BF16)
    for i in range(nd):
        thr2 = jnp.concatenate([thr_ref[i]] * tiles, axis=1)
        need = need_ref[i][:, 0:1]
        seen = jnp.zeros((bs, 1), F32)
        for c in range(i + 1):
            kk = key_ref[pair(i, c)]
            tie = kk == thr2
            rank = seen + jnp.dot(jnp.where(tie, 1.0, 0.0).astype(BF16), prefix,
                                  preferred_element_type=F32)
            tie_add = jnp.where(tie, jnp.where(rank <= need, 0.0, NEG), NEG)
            madd_ref[i, c] = jnp.where(kk > thr2, 0.0, tie_add).astype(madd_ref.dtype)
            seen = rank[:, bs - 1:bs]
        for c in range(i + 1, nd):
            madd_ref[i, c] = jnp.full((bs, bs), NEG, madd_ref.dtype)


def _dsa_select(idx3):
    bsz, t, _ = idx3.shape
    bs = min(ATT_BLOCK, t)
    nd = t // bs
    k_sel = min(B_TOPK_MAX, t // 4)
    col_block = lambda j: pl.BlockSpec((None, t, LANES), lambda b: (b, 0, j))
    return pl.pallas_call(
        functools.partial(_dsa_select_kernel, bs=bs, nd=nd, k_sel=k_sel),
        grid=(bsz,),
        in_specs=[col_block(0), col_block(1), col_block(2)],
        out_specs=pl.BlockSpec((None, nd, nd, bs, bs), lambda b: (b, 0, 0, 0, 0)),
        out_shape=jax.ShapeDtypeStruct((bsz, nd, nd, bs, bs), BF16),
        scratch_shapes=[pltpu.VMEM((nd * (nd + 1) // 2, bs, bs), jnp.int32),
                        pltpu.VMEM((nd, bs, LANES), jnp.int32),
                        pltpu.VMEM((nd, bs, LANES), F32)],
        compiler_params=_params("parallel"),
        name="dsa_select",
    )(idx3, idx3, idx3)


def _softmax_mixers(dsa3, idx3, dil3, tbias):
    return _attention(dsa3, dil3, tbias, _dsa_select(idx3))


def _route(logits):
    lane = lax.broadcasted_iota(jnp.int32, logits.shape, 1).astype(F32)
    far = float(LANES)

    def first_argmax(vals):
        top = jnp.max(vals, axis=1, keepdims=True)
        return top, jnp.min(jnp.where(vals == top, lane, far), axis=1, keepdims=True)

    is_group = lane < MOE_GROUPS
    g_top, g_sel = first_argmax(jnp.where(is_group, logits, NEG))
    g_w = 1.0 / jnp.sum(jnp.where(is_group, jnp.exp(logits - g_top), 0.0), axis=1, keepdims=True)
    lo = MOE_GROUPS + MOE_EXPERTS_PER_GROUP * g_sel
    e_logits = jnp.where(lane >= lo, jnp.where(lane < lo + MOE_EXPERTS_PER_GROUP, logits, NEG), NEG)
    v1, i1 = first_argmax(e_logits)
    v2, i2 = first_argmax(jnp.where(lane == i1, NEG, e_logits))
    e2 = jnp.exp(v2 - v1)
    w1 = 1.0 / (1.0 + e2)
    return jnp.where(lane == i1, w1 * g_w, jnp.where(lane == i2, e2 * w1 * g_w, 0.0))


def _merge_kernel(h_ref, g1_ref, oa_ref, ob_ref, oc_ref, od_ref, wg_ref, bg_ref,
                  wba_ref, wbb_ref, wbc_ref, wbd_ref, wo_ref, g2_ref, wr_ref, br_ref,
                  h1_ref, xn2_ref, gates_ref):
    h = h_ref[...]
    xn = _rms(h, g1_ref[...]).astype(BF16)
    merged = None
    for gi, (o_ref, wb_ref) in enumerate(((oa_ref, wba_ref), (ob_ref, wbb_ref),
                                          (oc_ref, wbc_ref), (od_ref, wbd_ref))):
        gate = _sigmoid(jnp.dot(xn, wg_ref[gi], preferred_element_type=F32) + bg_ref[gi:gi + 1, :])
        term = gate * jnp.dot(o_ref[...], wb_ref[...], preferred_element_type=F32)
        merged = term if merged is None else merged + term
    h1 = h + jnp.dot(merged.astype(BF16), wo_ref[...], preferred_element_type=F32)
    h1_ref[...] = h1
    xn2 = _rms(h1, g2_ref[...]).astype(BF16)
    xn2_ref[...] = xn2
    logits = jnp.dot(xn2, wr_ref[...], preferred_element_type=F32) + br_ref[...]
    gates_ref[...] = _route(logits)


def _merge(h, g1, oa, ob, oc, od, wg, bg, wba, wbb, wbc, wbd, wo, g2, wr, br):
    n = h.shape[0]
    tm = min(PROJ_ROWS, n)
    rows = lambda w: pl.BlockSpec((tm, w), lambda i: (i, 0))
    return pl.pallas_call(
        _merge_kernel,
        grid=(n // tm,),
        in_specs=[rows(D_MODEL), _resident(g1.shape),
                  rows(oa.shape[1]), rows(ob.shape[1]), rows(oc.shape[1]), rows(od.shape[1]),
                  _resident(wg.shape), _resident(bg.shape),
                  _resident(wba.shape), _resident(wbb.shape), _resident(wbc.shape),
                  _resident(wbd.shape), _resident(wo.shape), _resident(g2.shape),
                  _resident(wr.shape), _resident(br.shape)],
        out_specs=[rows(D_MODEL), rows(D_MODEL), rows(LANES)],
        out_shape=[jax.ShapeDtypeStruct((n, D_MODEL), F32),
                   jax.ShapeDtypeStruct((n, D_MODEL), BF16),
                   jax.ShapeDtypeStruct((n, LANES), F32)],
        compiler_params=_params("parallel"),
        name="merge",
    )(h, g1, oa, ob, oc, od, wg, bg, wba, wbb, wbc, wbd, wo, g2, wr, br)


def _moe_kernel(x_ref, gates_ref, h1_ref, wg_ref, wu_ref, wd_ref, gf_ref, o_ref,
                xs_ref, ys_ref, gs_ref, pos_ref, win_ref, *, final_norm, tm):
    e = pl.program_id(1)
    grp = e // MOE_EXPERTS_PER_GROUP
    ts = tm + MOE_GROUPS * MOE_ALIGN

    @pl.when(e == 0)
    def _():
        gates = gates_ref[...]
        lane = lax.broadcasted_iota(jnp.int32, (tm, LANES), 1)
        member = jnp.zeros((tm, LANES), F32)
        for gg in range(MOE_GROUPS):
            lo = MOE_GROUPS + gg * MOE_EXPERTS_PER_GROUP
            in_g = jnp.where(lane >= lo, jnp.where(lane < lo + MOE_EXPERTS_PER_GROUP, gates, 0.0), 0.0)
            total = jnp.sum(in_g, axis=1, keepdims=True)
            member = jnp.where(lane == gg, jnp.where(total > 0.0, 1.0, 0.0), member)
        row_sq = lax.broadcasted_iota(jnp.int32, (tm, tm), 0)
        col_sq = lax.broadcasted_iota(jnp.int32, (tm, tm), 1)
        before = jnp.dot(jnp.where(col_sq < row_sq, 1.0, 0.0).astype(BF16), member.astype(BF16),
                         preferred_element_type=F32)
        count = before[tm - 1:tm, :] + member[tm - 1:tm, :]
        padded = jnp.ceil(count * (1.0 / MOE_ALIGN)) * MOE_ALIGN
        r128 = lax.broadcasted_iota(jnp.int32, (LANES, LANES), 0)
        c128 = lax.broadcasted_iota(jnp.int32, (LANES, LANES), 1)
        first = jnp.dot(jnp.broadcast_to(padded, (8, LANES)), jnp.where(r128 < c128, 1.0, 0.0),
                        precision=HIGHEST, preferred_element_type=F32)[0:1, :]
        slot = member * (first + before)
        pos_ref[...] = jnp.broadcast_to(jnp.sum(slot, axis=1, keepdims=True), (tm, LANES))
        pos_t = lax.dot_general(jnp.ones((8, LANES), F32), slot, NT_DIMS, precision=HIGHEST,
                                preferred_element_type=F32)[0:1, :]
        lane_row = lax.broadcasted_iota(jnp.int32, (1, LANES), 1)
        for gg in range(MOE_GROUPS):
            seg_start = jnp.sum(jnp.where(lane_row == gg, first, 0.0)).astype(jnp.int32)
            seg_rows = jnp.sum(jnp.where(lane_row == gg, count, 0.0)).astype(jnp.int32)
            win_ref[gg] = seg_start
            win_ref[MOE_GROUPS + gg] = (seg_rows + MOE_WINDOW - 1) // MOE_WINDOW
        row_s = lax.broadcasted_iota(jnp.int32, (ts, tm), 0)
        perm = jnp.where(pos_t == row_s.astype(F32), 1.0, 0.0).astype(BF16)
        xs_ref[0:ts, :] = jnp.dot(perm, x_ref[...], preferred_element_type=F32).astype(BF16)
        xs_ref[ts:, :] = jnp.zeros((xs_ref.shape[0] - ts, D_MODEL), BF16)
        g_hi = gates.astype(BF16)
        g_lo = (gates - g_hi.astype(F32)).astype(BF16)
        gs_ref[0:ts, :] = (jnp.dot(perm, g_hi, preferred_element_type=F32)
                           + jnp.dot(perm, g_lo, preferred_element_type=F32))
        gs_ref[ts:, :] = jnp.zeros((gs_ref.shape[0] - ts, LANES), F32)
        ys_ref[...] = jnp.zeros(ys_ref.shape, F32)

    lane_w = lax.broadcasted_iota(jnp.int32, (MOE_WINDOW, LANES), 1)

    def window(k, _):
        start = pl.multiple_of(win_ref[grp] + k * MOE_WINDOW, MOE_ALIGN)
        xw = xs_ref[pl.ds(start, MOE_WINDOW), :]
        hg = jnp.dot(xw, wg_ref[...], preferred_element_type=F32)
        hu = jnp.dot(xw, wu_ref[...], preferred_element_type=F32)
        hid = (hg * _sigmoid(hg) * hu).astype(BF16)
        y = jnp.dot(hid, wd_ref[...], preferred_element_type=F32)
        gate = jnp.sum(jnp.where(lane_w == e + MOE_GROUPS, gs_ref[pl.ds(start, MOE_WINDOW), :], 0.0),
                       axis=1, keepdims=True)
        ys_ref[pl.ds(start, MOE_WINDOW), :] += gate * y
        return 0

    lax.fori_loop(0, win_ref[MOE_GROUPS + grp], window, 0)

    @pl.when(e == N_EXPERTS - 1)
    def _():
        pos_s = jnp.concatenate([pos_ref[...]] * (ts // LANES), axis=1)
        col_s = lax.broadcasted_iota(jnp.int32, (tm, ts), 1)
        unperm = jnp.where(pos_s == col_s.astype(F32), 1.0, 0.0).astype(BF16)
        ys = ys_ref[0:ts, :]
        y_hi = ys.astype(BF16)
        y_lo = (ys - y_hi.astype(F32)).astype(BF16)
        out = h1_ref[...] + (jnp.dot(unperm, y_hi, preferred_element_type=F32)
                             + jnp.dot(unperm, y_lo, preferred_element_type=F32))
        o_ref[...] = _rms(out, gf_ref[...]) if final_norm else out


def _moe(xn2, gates, h1, wg, wu, wd, gf, final_norm):
    n = xn2.shape[0]
    tm = min(MOE_ROWS, n)
    sorted_rows = tm + MOE_GROUPS * MOE_ALIGN + MOE_WINDOW
    rows = lambda w: pl.BlockSpec((tm, w), lambda i, e: (i, 0))
    return pl.pallas_call(
        functools.partial(_moe_kernel, final_norm=final_norm, tm=tm),
        grid=(n // tm, N_EXPERTS),
        in_specs=[rows(D_MODEL), rows(LANES), rows(D_MODEL),
                  pl.BlockSpec((None, D_MODEL, MOE_HIDDEN), lambda i, e: (e, 0, 0)),
                  pl.BlockSpec((None, D_MODEL, MOE_HIDDEN), lambda i, e: (e, 0, 0)),
                  pl.BlockSpec((None, MOE_HIDDEN, D_MODEL), lambda i, e: (e, 0, 0)),
                  pl.BlockSpec((1, D_MODEL), lambda i, e: (0, 0))],
        out_specs=rows(D_MODEL),
        out_shape=jax.ShapeDtypeStruct((n, D_MODEL), F32),
        scratch_shapes=[pltpu.VMEM((sorted_rows, D_MODEL), BF16),
                        pltpu.VMEM((sorted_rows, D_MODEL), F32),
                        pltpu.VMEM((sorted_rows, LANES), F32),
                        pltpu.VMEM((tm, LANES), F32),
                        pltpu.SMEM((2 * MOE_GROUPS,), jnp.int32)],
        compiler_params=_params("parallel", "arbitrary"),
        name="moe",
    )(xn2, gates, h1, wg, wu, wd, gf)


def _inproj_columns():
    offs = np.concatenate([[0], np.cumsum(IN_SPLITS)])
    (aq, ak, av, ag, alr, bq, bk, bv, iq, ik, iw, cq, ck, cv, dy, dx) = offs[:-1]
    cols = []

    def per_head(base, width):
        for h in range(A_HEADS):
            cols.extend(list(range(base + h * width, base + (h + 1) * width)) + [-1] * (LANES - width))

    per_head(aq, A_DK)
    per_head(ak, A_DK)
    per_head(av, A_DV)
    per_head(ag, A_DV)
    cols.extend(list(range(alr, alr + A_LOWRANK)) + [-1] * (LANES - A_LOWRANK))
    cols.extend(range(bq, bq + ATT_W))
    cols.extend(range(iq, iq + IDX_HEADS * IDX_DIM))
    cols.extend(list(range(ik, ik + IDX_DIM)) * IDX_HEADS)
    cols.extend(list(range(iw, iw + IDX_HEADS)) + [-1] * (LANES - IDX_HEADS))
    cols.extend(range(cq, cq + ATT_W))
    cols.extend(range(dy, dy + LRU_W))
    cols = np.asarray(cols, np.int32)
    assert cols.shape[0] == sum(GROUP_WIDTHS)
    return cols


def _pad_heads(a, width, axis):
    a = jnp.moveaxis(a, axis, -1)
    a = a.reshape(a.shape[:-1] + (A_HEADS, width))
    a = jnp.pad(a, [(0, 0)] * (a.ndim - 1) + [(0, LANES - width)])
    return jnp.moveaxis(a.reshape(a.shape[:-2] + (A_HEADS * LANES,)), -1, axis)


def _block_diag(w):
    nb, bw, _ = w.shape
    eye = jnp.eye(nb, dtype=w.dtype)
    return jnp.einsum('ncd,nm->ncmd', w, eye).reshape(nb * bw, nb * bw)


def kernel(x, w_in, a_w2, a_b2, a_gain, conv_w, conv_b, lru_wr, lru_br, lru_wi, lru_bi, lru_lambda, w_gate, b_gate, w_branch, w_out, rel_bias, norm1, norm2, norm_f, moe_wrg, moe_brg, moe_wre, moe_bre, moe_wg, moe_wu, moe_wd):
    bsz, t, _ = x.shape
    n = bsz * t
    depth = w_in.shape[0]
    cols = _inproj_columns()
    tbias = _tbias(rel_bias, t)
    h = x.reshape(n, D_MODEL)
    for l in range(depth):
        w_all = jnp.where(cols[None, :] >= 0, w_in[l][:, np.maximum(cols, 0)], 0.0).astype(BF16)
        gla_in, dsa_in, idx_in, dil_in, lru_in = _inproj(h, norm1[l][None, :], w_all)

        wa2 = jnp.pad(_pad_heads(a_w2[l], A_DK, 1), ((0, LANES - A_LOWRANK), (0, 0)))
        o_a = _gla(gla_in.reshape(bsz, t, GLA_W), wa2, _pad_heads(a_b2[l], A_DK, 0)[None, :],
                   _pad_heads(a_gain[l], A_DV, 0)[None, :])
        o_b, o_c = _softmax_mixers(dsa_in.reshape(bsz, t, ATT_W), idx_in.reshape(bsz, t, IDX_W),
                                   dil_in.reshape(bsz, t, ATT_W), tbias)
        w_ri = jnp.concatenate([_block_diag(lru_wr[l]), _block_diag(lru_wi[l])], axis=1).astype(BF16)
        b_ri = jnp.concatenate([lru_br[l], lru_bi[l]])[None, :]
        o_d = _rglru(lru_in.reshape(bsz, t, LRU_W), conv_w[l], conv_b[l][None, :], w_ri, b_ri,
                     lru_lambda[l][None, :])

        w_router = jnp.pad(jnp.concatenate([moe_wrg[l], moe_wre[l]], axis=1),
                           ((0, 0), (0, LANES - MOE_GROUPS - N_EXPERTS)))
        b_router = jnp.pad(jnp.concatenate([moe_brg[l], moe_bre[l]]),
                           (0, LANES - MOE_GROUPS - N_EXPERTS))[None, :]
        h1, xn2, gates = _merge(
            h, norm1[l][None, :], o_a.reshape(n, -1), o_b.reshape(n, -1), o_c.reshape(n, -1),
            o_d.reshape(n, -1), w_gate[l].astype(BF16), b_gate[l],
            _pad_heads(w_branch[l, 0], A_DV, 0).astype(BF16), w_branch[l, 1].astype(BF16),
            w_branch[l, 2].astype(BF16), w_branch[l, 3].astype(BF16), w_out[l].astype(BF16),
            norm2[l][None, :], w_router.astype(BF16), b_router)
        h = _moe(xn2, gates, h1, moe_wg[l].astype(BF16), moe_wu[l].astype(BF16),
                 moe_wd[l].astype(BF16), norm_f[None, :], final_norm=(l == depth - 1))
    return h.reshape(bsz, t, D_MODEL)
```

```python
import functools
import math

import numpy as np
import jax
import jax.numpy as jnp
from jax import lax
from jax.experimental import pallas as pl
from jax.experimental.pallas import tpu as pltpu

F32 = jnp.float32
BF16 = jnp.bfloat16
HIGHEST = lax.Precision.HIGHEST

D_MODEL = 1024
EPS = 1e-6
A_HEADS, A_DK, A_DV, A_LOWRANK, A_GATE_TAU, A_CHUNK = 4, 48, 96, 16, 16.0, 64
B_HEADS, B_HEAD_DIM, IDX_HEADS, IDX_DIM, B_TOPK_MAX = 6, 64, 4, 32, 256
C_HEADS, C_HEAD_DIM = 6, 64
C_PATTERNS = ((128, 1), (512, 4), (2048, 16))
D_WIDTH, D_BLOCKS, D_CONV, D_C = 384, 8, 4, 8.0
REL_BUCKETS, REL_MAX_DIST = 32, 2048
N_SOFTMAX_HEADS = B_HEADS + C_HEADS
MOE_GROUPS, MOE_EXPERTS_PER_GROUP, MOE_HIDDEN = 4, 4, 512
N_EXPERTS = MOE_GROUPS * MOE_EXPERTS_PER_GROUP
IN_SPLITS = (A_HEADS * A_DK, A_HEADS * A_DK, A_HEADS * A_DV, A_HEADS * A_DV, A_LOWRANK,
             B_HEADS * B_HEAD_DIM, B_HEADS * B_HEAD_DIM, B_HEADS * B_HEAD_DIM,
             IDX_HEADS * IDX_DIM, IDX_DIM, IDX_HEADS,
             C_HEADS * C_HEAD_DIM, C_HEADS * C_HEAD_DIM, C_HEADS * C_HEAD_DIM,
             D_WIDTH, D_WIDTH)

LANES = 128
VMEM_LIMIT = 56 * 1024 * 1024
NEG = -1e30
INT_MIN = -2 ** 31
LOG2E = math.log2(math.e)

ATT_BLOCK = 256
SEQ_BLOCK = 256
PROJ_ROWS = 512
MOE_ROWS = 1024
MOE_STEP_EXPERTS = 2
MOE_WINDOW = 288
MOE_ALIGN = 32

GLA_W = 4 * A_HEADS * LANES + LANES
ATT_W = 3 * B_HEADS * B_HEAD_DIM
IDX_W = 3 * LANES
LRU_W = 2 * D_WIDTH
GROUP_WIDTHS = (GLA_W, ATT_W, IDX_W, ATT_W, LRU_W)
NT_DIMS = (((1,), (1,)), ((), ()))
TN_DIMS = (((0,), (0,)), ((), ()))


def _params(*sem):
    return pltpu.CompilerParams(dimension_semantics=sem, vmem_limit_bytes=VMEM_LIMIT)


def _resident(shape):
    zeros = (0,) * len(shape)
    return pl.BlockSpec(shape, lambda *_: zeros, pipeline_mode=pl.Buffered(1))


def _rms(x, gain):
    return x * lax.rsqrt(jnp.mean(x * x, axis=-1, keepdims=True) + EPS) * gain


def _sigmoid(x):
    return 1.0 / (1.0 + jnp.exp(-x))


def _softplus(x):
    return jnp.maximum(x, 0.0) + jnp.log1p(jnp.exp(-jnp.abs(x)))


def _inproj_kernel(x_ref, g_ref, w_ref, *out_refs):
    xn = _rms(x_ref[...], g_ref[...]).astype(BF16)
    off = 0
    for o_ref in out_refs:
        width = o_ref.shape[-1]
        o_ref[...] = jnp.dot(xn, w_ref[:, off:off + width],
                             preferred_element_type=F32).astype(o_ref.dtype)
        off += width


def _inproj(h, gain, w_all):
    n = h.shape[0]
    tm = min(PROJ_ROWS, n)
    total = sum(GROUP_WIDTHS)
    return pl.pallas_call(
        _inproj_kernel,
        grid=(n // tm,),
        in_specs=[pl.BlockSpec((tm, D_MODEL), lambda i: (i, 0)),
                  _resident((1, D_MODEL)),
                  _resident((D_MODEL, total))],
        out_specs=[pl.BlockSpec((tm, w), lambda i: (i, 0)) for w in GROUP_WIDTHS],
        out_shape=[jax.ShapeDtypeStruct((n, w), BF16) for w in GROUP_WIDTHS],
        compiler_params=_params("parallel"),
        name="inproj",
    )(h, gain, w_all)


def _gla_kernel(x_ref, wa2_ref, ba2_ref, gain_ref, o_ref, st_ref, *, tb):
    @pl.when(pl.program_id(1) == 0)
    def _():
        st_ref[...] = jnp.zeros_like(st_ref)

    hw = A_HEADS * LANES
    x = x_ref[...]
    z = jnp.dot(x[:, 4 * hw:].astype(F32), wa2_ref[...], precision=HIGHEST,
                preferred_element_type=F32) + ba2_ref[...]
    log_a = -_softplus(-z) * (1.0 / A_GATE_TAU)

    r64 = lax.broadcasted_iota(jnp.int32, (A_CHUNK, A_CHUNK), 0)
    c64 = lax.broadcasted_iota(jnp.int32, (A_CHUNK, A_CHUNK), 1)
    causal = c64 <= r64
    tri = jnp.where(causal, 1.0, 0.0).astype(BF16)
    la_hi = log_a.astype(BF16)
    la_lo = (log_a - la_hi.astype(F32)).astype(BF16)
    b_parts, b_last_parts = [], []
    for c in range(tb // A_CHUNK):
        rs = slice(c * A_CHUNK, (c + 1) * A_CHUNK)
        bc = (jnp.dot(tri, la_hi[rs], preferred_element_type=F32)
              + jnp.dot(tri, la_lo[rs], preferred_element_type=F32))
        b_parts.append(bc)
        b_last_parts.append(jnp.broadcast_to(bc[A_CHUNK - 1:A_CHUNK, :], bc.shape))
    b = jnp.concatenate(b_parts, axis=0)
    b_last = jnp.concatenate(b_last_parts, axis=0)

    q = x[:, 0:hw].astype(F32) * (A_DK ** -0.5)
    k = x[:, hw:2 * hw].astype(F32)
    v = x[:, 2 * hw:3 * hw]
    q_dec = (q * jnp.exp(b)).astype(BF16)
    k_dec = (k * jnp.exp(-b)).astype(BF16)
    k_end = (k * jnp.exp(b_last - b)).astype(BF16)
    decay = jnp.exp(b_last)

    heads = []
    for h in range(A_HEADS):
        cs = slice(h * LANES, (h + 1) * LANES)
        state_t = st_ref[h]
        chunks = []
        for c in range(tb // A_CHUNK):
            rs = slice(c * A_CHUNK, (c + 1) * A_CHUNK)
            qc, kc, kec, vc = q_dec[rs, cs], k_dec[rs, cs], k_end[rs, cs], v[rs, cs]
            att = lax.dot_general(qc, kc, NT_DIMS, preferred_element_type=F32)
            att = jnp.where(causal, att, 0.0).astype(BF16)
            oc = jnp.dot(att, vc, preferred_element_type=F32)
            oc += lax.dot_general(qc, state_t.astype(BF16), NT_DIMS, preferred_element_type=F32)
            inc_t = lax.dot_general(vc, kec, TN_DIMS, preferred_element_type=F32)
            state_t = state_t * decay[c * A_CHUNK:c * A_CHUNK + 1, cs] + inc_t
            chunks.append(oc)
        st_ref[h] = state_t
        o_h = jnp.concatenate(chunks, axis=0)
        ms = jnp.sum(o_h * o_h, axis=-1, keepdims=True) * (1.0 / A_DV)
        heads.append(o_h * lax.rsqrt(ms + EPS))
    o = jnp.concatenate(heads, axis=1) * gain_ref[...]
    g = x[:, 3 * hw:4 * hw].astype(F32)
    o_ref[...] = (o * (g * _sigmoid(g))).astype(o_ref.dtype)


def _gla(x3, wa2, ba2, gain):
    bsz, t, _ = x3.shape
    tb = min(SEQ_BLOCK, t)
    hw = A_HEADS * LANES
    return pl.pallas_call(
        functools.partial(_gla_kernel, tb=tb),
        grid=(bsz, t // tb),
        in_specs=[pl.BlockSpec((None, tb, GLA_W), lambda b, i: (b, i, 0)),
                  _resident((LANES, hw)), _resident((1, hw)), _resident((1, hw))],
        out_specs=pl.BlockSpec((None, tb, hw), lambda b, i: (b, i, 0)),
        out_shape=jax.ShapeDtypeStruct((bsz, t, hw), BF16),
        scratch_shapes=[pltpu.VMEM((A_HEADS, LANES, LANES), F32)],
        compiler_params=_params("parallel", "arbitrary"),
        name="gla",
    )(x3, wa2, ba2, gain)


def _rglru_kernel(d_ref, cw_ref, cb_ref, wri_ref, bri_ref, lam_ref, o_ref, xbuf, hc_ref, *, tb):
    halo = 8

    @pl.when(pl.program_id(1) == 0)
    def _():
        xbuf[0:halo, :] = jnp.zeros((halo, D_WIDTH), F32)
        hc_ref[...] = jnp.zeros_like(hc_ref)

    d = d_ref[...]
    y = d[:, 0:D_WIDTH].astype(F32)
    x = d[:, D_WIDTH:].astype(F32)
    xbuf[halo:halo + tb, :] = x
    xc = cb_ref[...]
    for j in range(D_CONV):
        xc = xc + cw_ref[j:j + 1, :] * xbuf[pl.ds(halo - (D_CONV - 1) + j, tb), :]
    xbuf[0:halo, :] = x[tb - halo:tb, :]

    ri = jnp.dot(xc.astype(BF16), wri_ref[...], preferred_element_type=F32) + bri_ref[...]
    r = _sigmoid(ri[:, 0:D_WIDTH])
    ig = _sigmoid(ri[:, D_WIDTH:])
    log_a = (-D_C) * r * _softplus(-lam_ref[...])
    a = jnp.exp(log_a)
    u = jnp.sqrt(-jnp.tanh(log_a) * (a * a + 1.0)) * (ig * xc)

    row = lax.broadcasted_iota(jnp.int32, (tb, D_WIDTH), 0)
    s = 1
    while s < tb:
        valid = row >= s
        u = jnp.where(valid, a * pltpu.roll(u, s, axis=0), 0.0) + u
        a = jnp.where(valid, a * pltpu.roll(a, s, axis=0), a)
        s *= 2
    hs = u + a * hc_ref[0:1, :]
    hc_ref[...] = jnp.broadcast_to(hs[tb - 1:tb, :], hc_ref.shape)
    gelu = 0.5 * y * (1.0 + jnp.tanh(math.sqrt(2.0 / math.pi) * (y + 0.044715 * (y * y * y))))
    o_ref[...] = (gelu * hs).astype(o_ref.dtype)


def _rglru(d3, conv_w, conv_b, w_ri, b_ri, lam):
    bsz, t, _ = d3.shape
    tb = min(SEQ_BLOCK, t)
    return pl.pallas_call(
        functools.partial(_rglru_kernel, tb=tb),
        grid=(bsz, t // tb),
        in_specs=[pl.BlockSpec((None, tb, LRU_W), lambda b, i: (b, i, 0)),
                  _resident((D_CONV, D_WIDTH)), _resident((1, D_WIDTH)),
                  _resident((D_WIDTH, 2 * D_WIDTH)), _resident((1, 2 * D_WIDTH)),
                  _resident((1, D_WIDTH))],
        out_specs=pl.BlockSpec((None, tb, D_WIDTH), lambda b, i: (b, i, 0)),
        out_shape=jax.ShapeDtypeStruct((bsz, t, D_WIDTH), BF16),
        scratch_shapes=[pltpu.VMEM((tb + 8, D_WIDTH), F32), pltpu.VMEM((8, D_WIDTH), F32)],
        compiler_params=_params("parallel", "arbitrary"),
        name="rglru",
    )(d3, conv_w, conv_b, w_ri, b_ri, lam)


def _bucket_edges():
    n = np.arange(REL_MAX_DIST + 1)
    max_exact = REL_BUCKETS // 2
    nf = np.maximum(n, 1).astype(np.float64)
    large = max_exact + (np.log(nf / max_exact) / math.log(REL_MAX_DIST / max_exact)
                         * (REL_BUCKETS - max_exact)).astype(np.int64)
    bucket = np.where(n < max_exact, n, np.minimum(large, REL_BUCKETS - 1))
    assert np.all(np.diff(bucket) >= 0)
    return [int(np.argmax(bucket >= b)) for b in range(REL_BUCKETS)]


def _tbias_kernel(tab_ref, o_ref, *, bs, edges):
    h = pl.program_id(0)
    d = pl.program_id(1)
    row = lax.broadcasted_iota(jnp.int32, (bs, bs), 0)
    col = lax.broadcasted_iota(jnp.int32, (bs, bs), 1)
    dist = d * bs + row - col
    val = jnp.full((bs, bs), tab_ref[0, h], F32)
    for b in range(1, REL_BUCKETS):
        val = jnp.where(dist >= edges[b], tab_ref[b, h], val)
    mult = jnp.zeros((bs, bs), jnp.int32)
    for window, dil in C_PATTERNS:
        hit = jnp.where((dist & (dil - 1)) == 0, 1, 0)
        mult = mult + jnp.where(dist <= window, hit, 0)
    logm = jnp.full((bs, bs), NEG, F32)
    for m in range(1, len(C_PATTERNS) + 1):
        logm = jnp.where(mult == m, math.log(m), logm)
    val = jnp.where(h >= B_HEADS, val + logm, val)
    o_ref[...] = jnp.where(dist >= 0, val * LOG2E, NEG)


def _tbias(rel_bias, t):
    bs = min(ATT_BLOCK, t)
    nd = t // bs
    for _, dil in C_PATTERNS:
        assert dil & (dil - 1) == 0
    return pl.pallas_call(
        functools.partial(_tbias_kernel, bs=bs, edges=_bucket_edges()),
        grid=(N_SOFTMAX_HEADS, nd),
        in_specs=[pl.BlockSpec(memory_space=pltpu.SMEM)],
        out_specs=pl.BlockSpec((None, None, bs, bs), lambda h, d: (h, d, 0, 0)),
        out_shape=jax.ShapeDtypeStruct((N_SOFTMAX_HEADS, nd, bs, bs), F32),
        compiler_params=_params("parallel", "parallel"),
        name="tbias",
    )(rel_bias)


def _attn_kernel(qb_ref, kb_ref, vb_ref, qc_ref, kc_ref, vc_ref, tb_ref, madd_ref,
                 ob_ref, oc_ref, vp_ref, m_ref, acc_ref, *, bs):
    i = pl.program_id(1)
    streams = ((qb_ref, kb_ref, vb_ref, ob_ref), (qc_ref, kc_ref, vc_ref, oc_ref))
    per_stream = B_HEADS
    n_heads = vp_ref.shape[0]
    half = LANES // 2
    assert B_HEAD_DIM == half and C_HEAD_DIM == half and B_HEADS == C_HEADS

    @pl.when(i == 0)
    def _():
        low_t = lax.broadcasted_iota(jnp.int32, (vb_ref.shape[0], LANES), 1) < half
        for h in range(0, n_heads, 2):
            v_ref = streams[h // per_stream][2]
            hp = (h % per_stream) // 2
            pair = v_ref[:, hp * LANES:(hp + 1) * LANES]
            vp_ref[h] = jnp.where(low_t, pair, 1.0).astype(BF16)
            vp_ref[h + 1] = jnp.where(low_t, 1.0, pair).astype(BF16)

    m_ref[...] = jnp.full(m_ref.shape, NEG, F32)
    acc_ref[...] = jnp.zeros(acc_ref.shape, F32)
    low = lax.broadcasted_iota(jnp.int32, (bs, LANES), 1) < half
    q_heads = []
    for h in range(n_heads):
        q_ref = streams[h // per_stream][0]
        hp = (h % per_stream) // 2
        qp = q_ref[:, hp * LANES:(hp + 1) * LANES].astype(F32) * (B_HEAD_DIM ** -0.5 * LOG2E)
        q_heads.append(jnp.where(low if h % 2 == 0 else jnp.logical_not(low), qp, 0.0).astype(BF16))

    def body(c, _):
        start = pl.multiple_of(c * bs, bs)
        extra = madd_ref[c].astype(F32)
        for h in range(n_heads):
            k_ref = streams[h // per_stream][1]
            hp = (h % per_stream) // 2
            ks = k_ref[pl.ds(start, bs), hp * LANES:(hp + 1) * LANES]
            s = lax.dot_general(q_heads[h], ks, NT_DIMS, preferred_element_type=F32)
            s = s + tb_ref[h, i - c]
            if h < per_stream:
                s = s + extra
            m_prev = m_ref[h]
            m_new = jnp.maximum(m_prev, jnp.max(s, axis=1, keepdims=True))
            alpha = jnp.exp2(m_prev - m_new)
            p = jnp.exp2((s - jnp.concatenate([m_new] * (bs // LANES), axis=1)).astype(BF16))
            acc_ref[h] = alpha * acc_ref[h] + jnp.dot(p, vp_ref[h, pl.ds(start, bs), :],
                                                      preferred_element_type=F32)
            m_ref[h] = m_new
        return 0

    lax.fori_loop(0, i + 1, body, 0)

    for h in range(0, n_heads, 2):
        o_ref = streams[h // per_stream][3]
        hp = (h % per_stream) // 2
        a0, a1 = acc_ref[h], acc_ref[h + 1]
        o0 = a0 / pltpu.roll(a0, half, axis=1)
        o1 = a1 / pltpu.roll(a1, half, axis=1)
        o_ref[:, hp * LANES:(hp + 1) * LANES] = jnp.where(low, o0, o1).astype(o_ref.dtype)


def _attention(dsa3, dil3, tbias, madd):
    bsz, t, width3 = dsa3.shape
    width = width3 // 3
    bs = min(ATT_BLOCK, t)
    nd = t // bs
    q_spec = pl.BlockSpec((None, bs, width), lambda b, i: (b, i, 0))
    k_spec = pl.BlockSpec((None, t, width), lambda b, i: (b, 0, 1), pipeline_mode=pl.Buffered(1))
    v_spec = pl.BlockSpec((None, t, width), lambda b, i: (b, 0, 2), pipeline_mode=pl.Buffered(1))
    out_spec = pl.BlockSpec((None, bs, width), lambda b, i: (b, i, 0))
    out_shape = jax.ShapeDtypeStruct((bsz, t, width), BF16)
    return pl.pallas_call(
        functools.partial(_attn_kernel, bs=bs),
        grid=(bsz, nd),
        in_specs=[q_spec, k_spec, v_spec, q_spec, k_spec, v_spec,
                  _resident(tbias.shape),
                  pl.BlockSpec((None, None, nd, bs, bs), lambda b, i: (b, i, 0, 0, 0))],
        out_specs=[out_spec, out_spec],
        out_shape=[out_shape, out_shape],
        scratch_shapes=[pltpu.VMEM((N_SOFTMAX_HEADS, t, LANES), BF16),
                        pltpu.VMEM((N_SOFTMAX_HEADS, bs, LANES), F32),
                        pltpu.VMEM((N_SOFTMAX_HEADS, bs, LANES), F32)],
        compiler_params=_params("arbitrary", "arbitrary"),
        name="attn",
    )(dsa3, dsa3, dsa3, dil3, dil3, dil3, tbias, madd)


def _dsa_select_kernel(iq_ref, ik_ref, iw_ref, madd_ref, key_ref, thr_ref, need_ref,
                       *, bs, nd, k_sel):
    pair = lambda i, c: i * (i + 1) // 2 + c
    lane = lax.broadcasted_iota(jnp.int32, (bs, LANES), 1)
    row = lax.broadcasted_iota(jnp.int32, (bs, bs), 0)
    col = lax.broadcasted_iota(jnp.int32, (bs, bs), 1)
    tiles = bs // LANES

    for i in range(nd):
        iq = iq_ref[i * bs:(i + 1) * bs, :]
        iw = iw_ref[i * bs:(i + 1) * bs, :].astype(F32) * (IDX_HEADS ** -0.5 * IDX_DIM ** -0.5)
        iq_heads = [jnp.where((lane // IDX_DIM) == hh, iq, 0.0).astype(BF16)
                    for hh in range(IDX_HEADS)]
        for c in range(i + 1):
            ikc = ik_ref[c * bs:(c + 1) * bs, :]
            sc = jnp.zeros((bs, bs), F32)
            for hh in range(IDX_HEADS):
                raw = lax.dot_general(iq_heads[hh], ikc, NT_DIMS, preferred_element_type=F32)
                sc = sc + iw[:, hh:hh + 1] * jnp.maximum(raw, 0.0)
            sc = jnp.where(sc == 0.0, 0.0, sc)
            bits = pltpu.bitcast(sc, jnp.int32)
            key = jnp.where(bits < 0, bits ^ 0x7FFFFFFF, bits)
            if c == i:
                key = jnp.where(col <= row, key, INT_MIN)
            key_ref[pair(i, c)] = key

    thr_ref[...] = jnp.full(thr_ref.shape, INT_MIN, jnp.int32)
    kf = float(k_sel)
    n_bits = 32

    def bisect(it, _):
        searching = it < n_bits
        step = jnp.where(searching, lax.shift_left(jnp.int32(1), jnp.maximum(n_bits - 1 - it, 0)), 1)
        for i in range(nd):
            thr = thr_ref[i]
            cand = thr + step
            acc = jnp.zeros((bs, LANES), F32)
            for c in range(i + 1):
                kk = key_ref[pair(i, c)]
                for j in range(tiles):
                    acc = acc + jnp.where(kk[:, j * LANES:(j + 1) * LANES] >= cand, 1.0, 0.0)
            cnt = jnp.sum(acc, axis=1, keepdims=True)
            thr_ref[i] = jnp.where(jnp.logical_and(searching, cnt >= kf), cand, thr)
            need_ref[i] = jnp.broadcast_to(kf - cnt, (bs, LANES))
        return 0

    lax.fori_loop(0, n_bits + 1, bisect, 0)

    prefix = jnp.where(row <= col, 1.0, 0.0).astype(BF16)
    for i in range(nd):
        thr2 = jnp.concatenate([thr_ref[i]] * tiles, axis=1)
        need = need_ref[i][:, 0:1]
        seen = jnp.zeros((bs, 1), F32)
        for c in range(i + 1):
            kk = key_ref[pair(i, c)]
            tie = kk == thr2
            rank = seen + jnp.dot(jnp.where(tie, 1.0, 0.0).astype(BF16), prefix,
                                  preferred_element_type=F32)
            tie_add = jnp.where(tie, jnp.where(rank <= need, 0.0, NEG), NEG)
            madd_ref[i, c] = jnp.where(kk > thr2, 0.0, tie_add).astype(madd_ref.dtype)
            seen = rank[:, bs - 1:bs]
        for c in range(i + 1, nd):
            madd_ref[i, c] = jnp.full((bs, bs), NEG, madd_ref.dtype)


def _dsa_select(idx3):
    bsz, t, _ = idx3.shape
    bs = min(ATT_BLOCK, t)
    nd = t // bs
    k_sel = min(B_TOPK_MAX, t // 4)
    col_block = lambda j: pl.BlockSpec((None, t, LANES), lambda b: (b, 0, j))
    return pl.pallas_call(
        functools.partial(_dsa_select_kernel, bs=bs, nd=nd, k_sel=k_sel),
        grid=(bsz,),
        in_specs=[col_block(0), col_block(1), col_block(2)],
        out_specs=pl.BlockSpec((None, nd, nd, bs, bs), lambda b: (b, 0, 0, 0, 0)),
        out_shape=jax.ShapeDtypeStruct((bsz, nd, nd, bs, bs), BF16),
        scratch_shapes=[pltpu.VMEM((nd * (nd + 1) // 2, bs, bs), jnp.int32),
                        pltpu.VMEM((nd, bs, LANES), jnp.int32),
                        pltpu.VMEM((nd, bs, LANES), F32)],
        compiler_params=_params("parallel"),
        name="dsa_select",
    )(idx3, idx3, idx3)


def _softmax_mixers(dsa3, idx3, dil3, tbias):
    return _attention(dsa3, dil3, tbias, _dsa_select(idx3))


def _route(logits):
    lane = lax.broadcasted_iota(jnp.int32, logits.shape, 1).astype(F32)
    far = float(LANES)

    def first_argmax(vals):
        top = jnp.max(vals, axis=1, keepdims=True)
        return top, jnp.min(jnp.where(vals == top, lane, far), axis=1, keepdims=True)

    is_group = lane < MOE_GROUPS
    g_top, g_sel = first_argmax(jnp.where(is_group, logits, NEG))
    g_w = 1.0 / jnp.sum(jnp.where(is_group, jnp.exp(logits - g_top), 0.0), axis=1, keepdims=True)
    lo = MOE_GROUPS + MOE_EXPERTS_PER_GROUP * g_sel
    e_logits = jnp.where(lane >= lo, jnp.where(lane < lo + MOE_EXPERTS_PER_GROUP, logits, NEG), NEG)
    v1, i1 = first_argmax(e_logits)
    v2, i2 = first_argmax(jnp.where(lane == i1, NEG, e_logits))
    e2 = jnp.exp(v2 - v1)
    w1 = 1.0 / (1.0 + e2)
    return jnp.where(lane == i1, w1 * g_w, jnp.where(lane == i2, e2 * w1 * g_w, 0.0))


def _merge_kernel(h_ref, g1_ref, oa_ref, ob_ref, oc_ref, od_ref, wg_ref, bg_ref,
                  wba_ref, wbb_ref, wbc_ref, wbd_ref, wo_ref, g2_ref, wr_ref, br_ref,
                  h1_ref, xn2_ref, gates_ref):
    h = h_ref[...]
    xn = _rms(h, g1_ref[...]).astype(BF16)
    merged = None
    for gi, (o_ref, wb_ref) in enumerate(((oa_ref, wba_ref), (ob_ref, wbb_ref),
                                          (oc_ref, wbc_ref), (od_ref, wbd_ref))):
        gate = _sigmoid(jnp.dot(xn, wg_ref[gi], preferred_element_type=F32) + bg_ref[gi:gi + 1, :])
        term = gate * jnp.dot(o_ref[...], wb_ref[...], preferred_element_type=F32)
        merged = term if merged is None else merged + term
    h1 = h + jnp.dot(merged.astype(BF16), wo_ref[...], preferred_element_type=F32)
    h1_ref[...] = h1
    xn2 = _rms(h1, g2_ref[...]).astype(BF16)
    xn2_ref[...] = xn2
    logits = jnp.dot(xn2, wr_ref[...], preferred_element_type=F32) + br_ref[...]
    gates_ref[...] = _route(logits)


def _merge(h, g1, oa, ob, oc, od, wg, bg, wba, wbb, wbc, wbd, wo, g2, wr, br):
    n = h.shape[0]
    tm = min(PROJ_ROWS, n)
    rows = lambda w: pl.BlockSpec((tm, w), lambda i: (i, 0))
    return pl.pallas_call(
        _merge_kernel,
        grid=(n // tm,),
        in_specs=[rows(D_MODEL), _resident(g1.shape),
                  rows(oa.shape[1]), rows(ob.shape[1]), rows(oc.shape[1]), rows(od.shape[1]),
                  _resident(wg.shape), _resident(bg.shape),
                  _resident(wba.shape), _resident(wbb.shape), _resident(wbc.shape),
                  _resident(wbd.shape), _resident(wo.shape), _resident(g2.shape),
                  _resident(wr.shape), _resident(br.shape)],
        out_specs=[rows(D_MODEL), rows(D_MODEL), rows(LANES)],
        out_shape=[jax.ShapeDtypeStruct((n, D_MODEL), F32),
                   jax.ShapeDtypeStruct((n, D_MODEL), BF16),
                   jax.ShapeDtypeStruct((n, LANES), F32)],
        compiler_params=_params("parallel"),
        name="merge",
    )(h, g1, oa, ob, oc, od, wg, bg, wba, wbb, wbc, wbd, wo, g2, wr, br)


def _moe_kernel(x_ref, gates_ref, h1_ref, wg_ref, wu_ref, wd_ref, gf_ref, o_ref,
                xs_ref, ys_ref, gs_ref, pos_ref, win_ref, *, final_norm, tm):
    step = pl.program_id(1)
    n_steps = N_EXPERTS // MOE_STEP_EXPERTS
    grp = (step * MOE_STEP_EXPERTS) // MOE_EXPERTS_PER_GROUP
    ts = tm + MOE_GROUPS * MOE_ALIGN

    @pl.when(step == 0)
    def _():
        gates = gates_ref[...]
        lane = lax.broadcasted_iota(jnp.int32, (tm, LANES), 1)
        member = jnp.zeros((tm, LANES), F32)
        for gg in range(MOE_GROUPS):
            lo = MOE_GROUPS + gg * MOE_EXPERTS_PER_GROUP
            in_g = jnp.where(lane >= lo, jnp.where(lane < lo + MOE_EXPERTS_PER_GROUP, gates, 0.0), 0.0)
            total = jnp.sum(in_g, axis=1, keepdims=True)
            member = jnp.where(lane == gg, jnp.where(total > 0.0, 1.0, 0.0), member)
        row_sq = lax.broadcasted_iota(jnp.int32, (tm, tm), 0)
        col_sq = lax.broadcasted_iota(jnp.int32, (tm, tm), 1)
        before = jnp.dot(jnp.where(col_sq < row_sq, 1.0, 0.0).astype(BF16), member.astype(BF16),
                         preferred_element_type=F32)
        count = before[tm - 1:tm, :] + member[tm - 1:tm, :]
        padded = jnp.ceil(count * (1.0 / MOE_ALIGN)) * MOE_ALIGN
        r128 = lax.broadcasted_iota(jnp.int32, (LANES, LANES), 0)
        c128 = lax.broadcasted_iota(jnp.int32, (LANES, LANES), 1)
        first = jnp.dot(jnp.broadcast_to(padded, (8, LANES)), jnp.where(r128 < c128, 1.0, 0.0),
                        precision=HIGHEST, preferred_element_type=F32)[0:1, :]
        slot = member * (first + before)
        pos_ref[...] = jnp.broadcast_to(jnp.sum(slot, axis=1, keepdims=True), (tm, LANES))
        pos_t = lax.dot_general(jnp.ones((8, LANES), F32), slot, NT_DIMS, precision=HIGHEST,
                                preferred_element_type=F32)[0:1, :]
        lane_row = lax.broadcasted_iota(jnp.int32, (1, LANES), 1)
        for gg in range(MOE_GROUPS):
            seg_start = jnp.sum(jnp.where(lane_row == gg, first, 0.0)).astype(jnp.int32)
            seg_rows = jnp.sum(jnp.where(lane_row == gg, count, 0.0)).astype(jnp.int32)
            win_ref[gg] = seg_start
            win_ref[MOE_GROUPS + gg] = (seg_rows + MOE_WINDOW - 1) // MOE_WINDOW
        row_s = lax.broadcasted_iota(jnp.int32, (ts, tm), 0)
        perm = jnp.where(pos_t == row_s.astype(F32), 1.0, 0.0).astype(BF16)
        xs_ref[0:ts, :] = jnp.dot(perm, x_ref[...], preferred_element_type=F32).astype(BF16)
        xs_ref[ts:, :] = jnp.zeros((xs_ref.shape[0] - ts, D_MODEL), BF16)
        g_hi = gates.astype(BF16)
        g_lo = (gates - g_hi.astype(F32)).astype(BF16)
        gs_ref[0:ts, :] = (jnp.dot(perm, g_hi, preferred_element_type=F32)
                           + jnp.dot(perm, g_lo, preferred_element_type=F32))
        gs_ref[ts:, :] = jnp.zeros((gs_ref.shape[0] - ts, LANES), F32)
        ys_ref[...] = jnp.zeros(ys_ref.shape, F32)

    lane_w = lax.broadcasted_iota(jnp.int32, (MOE_WINDOW, LANES), 1)

    def window(k, _):
        start = pl.multiple_of(win_ref[grp] + k * MOE_WINDOW, MOE_ALIGN)
        xw = xs_ref[pl.ds(start, MOE_WINDOW), :]
        gw = gs_ref[pl.ds(start, MOE_WINDOW), :]
        total = None
        for j in range(MOE_STEP_EXPERTS):
            hg = jnp.dot(xw, wg_ref[j], preferred_element_type=F32)
            hu = jnp.dot(xw, wu_ref[j], preferred_element_type=F32)
            hid = (hg * _sigmoid(hg) * hu).astype(BF16)
            y = jnp.dot(hid, wd_ref[j], preferred_element_type=F32)
            lane_e = step * MOE_STEP_EXPERTS + j + MOE_GROUPS
            gate = jnp.sum(jnp.where(lane_w == lane_e, gw, 0.0), axis=1, keepdims=True)
            total = gate * y if total is None else total + gate * y
        ys_ref[pl.ds(start, MOE_WINDOW), :] += total
        return 0

    lax.fori_loop(0, win_ref[MOE_GROUPS + grp], window, 0)

    @pl.when(step == n_steps - 1)
    def _():
        pos_s = jnp.concatenate([pos_ref[...]] * (ts // LANES), axis=1)
        col_s = lax.broadcasted_iota(jnp.int32, (tm, ts), 1)
        unperm = jnp.where(pos_s == col_s.astype(F32), 1.0, 0.0).astype(BF16)
        ys = ys_ref[0:ts, :]
        y_hi = ys.astype(BF16)
        y_lo = (ys - y_hi.astype(F32)).astype(BF16)
        out = h1_ref[...] + (jnp.dot(unperm, y_hi, preferred_element_type=F32)
                             + jnp.dot(unperm, y_lo, preferred_element_type=F32))
        o_ref[...] = _rms(out, gf_ref[...]) if final_norm else out


def _moe(xn2, gates, h1, wg, wu, wd, gf, final_norm):
    n = xn2.shape[0]
    tm = min(MOE_ROWS, n)
    sorted_rows = tm + MOE_GROUPS * MOE_ALIGN + MOE_WINDOW
    rows = lambda w: pl.BlockSpec((tm, w), lambda i, e: (i, 0))
    return pl.pallas_call(
        functools.partial(_moe_kernel, final_norm=final_norm, tm=tm),
        grid=(n // tm, N_EXPERTS // MOE_STEP_EXPERTS),
        in_specs=[rows(D_MODEL), rows(LANES), rows(D_MODEL),
                  pl.BlockSpec((MOE_STEP_EXPERTS, D_MODEL, MOE_HIDDEN), lambda i, e: (e, 0, 0)),
                  pl.BlockSpec((MOE_STEP_EXPERTS, D_MODEL, MOE_HIDDEN), lambda i, e: (e, 0, 0)),
                  pl.BlockSpec((MOE_STEP_EXPERTS, MOE_HIDDEN, D_MODEL), lambda i, e: (e, 0, 0)),
                  pl.BlockSpec((1, D_MODEL), lambda i, e: (0, 0))],
        out_specs=rows(D_MODEL),
        out_shape=jax.ShapeDtypeStruct((n, D_MODEL), F32),
        scratch_shapes=[pltpu.VMEM((sorted_rows, D_MODEL), BF16),
                        pltpu.VMEM((sorted_rows, D_MODEL), F32),
                        pltpu.VMEM((sorted_rows, LANES), F32),
                        pltpu.VMEM((tm, LANES), F32),
                        pltpu.SMEM((2 * MOE_GROUPS,), jnp.int32)],
        compiler_params=_params("parallel", "arbitrary"),
        name="moe",
    )(xn2, gates, h1, wg, wu, wd, gf)


def _inproj_columns():
    offs = np.concatenate([[0], np.cumsum(IN_SPLITS)])
    (aq, ak, av, ag, alr, bq, bk, bv, iq, ik, iw, cq, ck, cv, dy, dx) = offs[:-1]
    cols = []

    def per_head(base, width):
        for h in range(A_HEADS):
            cols.extend(list(range(base + h * width, base + (h + 1) * width)) + [-1] * (LANES - width))

    per_head(aq, A_DK)
    per_head(ak, A_DK)
    per_head(av, A_DV)
    per_head(ag, A_DV)
    cols.extend(list(range(alr, alr + A_LOWRANK)) + [-1] * (LANES - A_LOWRANK))
    cols.extend(range(bq, bq + ATT_W))
    cols.extend(range(iq, iq + IDX_HEADS * IDX_DIM))
    cols.extend(list(range(ik, ik + IDX_DIM)) * IDX_HEADS)
    cols.extend(list(range(iw, iw + IDX_HEADS)) + [-1] * (LANES - IDX_HEADS))
    cols.extend(range(cq, cq + ATT_W))
    cols.extend(range(dy, dy + LRU_W))
    cols = np.asarray(cols, np.int32)
    assert cols.shape[0] == sum(GROUP_WIDTHS)
    return cols


def _pad_heads(a, width, axis):
    a = jnp.moveaxis(a, axis, -1)
    a = a.reshape(a.shape[:-1] + (A_HEADS, width))
    a = jnp.pad(a, [(0, 0)] * (a.ndim - 1) + [(0, LANES - width)])
    return jnp.moveaxis(a.reshape(a.shape[:-2] + (A_HEADS * LANES,)), -1, axis)


def _block_diag(w):
    nb, bw, _ = w.shape
    eye = jnp.eye(nb, dtype=w.dtype)
    return jnp.einsum('ncd,nm->ncmd', w, eye).reshape(nb * bw, nb * bw)


def kernel(x, w_in, a_w2, a_b2, a_gain, conv_w, conv_b, lru_wr, lru_br, lru_wi, lru_bi, lru_lambda, w_gate, b_gate, w_branch, w_out, rel_bias, norm1, norm2, norm_f, moe_wrg, moe_brg, moe_wre, moe_bre, moe_wg, moe_wu, moe_wd):
    bsz, t, _ = x.shape
    n = bsz * t
    depth = w_in.shape[0]
    cols = _inproj_columns()
    tbias = _tbias(rel_bias, t)
    h = x.reshape(n, D_MODEL)
    for l in range(depth):
        w_all = jnp.where(cols[None, :] >= 0, w_in[l][:, np.maximum(cols, 0)], 0.0).astype(BF16)
        gla_in, dsa_in, idx_in, dil_in, lru_in = _inproj(h, norm1[l][None, :], w_all)

        wa2 = jnp.pad(_pad_heads(a_w2[l], A_DK, 1), ((0, LANES - A_LOWRANK), (0, 0)))
        o_a = _gla(gla_in.reshape(bsz, t, GLA_W), wa2, _pad_heads(a_b2[l], A_DK, 0)[None, :],
                   _pad_heads(a_gain[l], A_DV, 0)[None, :])
        o_b, o_c = _softmax_mixers(dsa_in.reshape(bsz, t, ATT_W), idx_in.reshape(bsz, t, IDX_W),
                                   dil_in.reshape(bsz, t, ATT_W), tbias)
        w_ri = jnp.concatenate([_block_diag(lru_wr[l]), _block_diag(lru_wi[l])], axis=1).astype(BF16)
        b_ri = jnp.concatenate([lru_br[l], lru_bi[l]])[None, :]
        o_d = _rglru(lru_in.reshape(bsz, t, LRU_W), conv_w[l], conv_b[l][None, :], w_ri, b_ri,
                     lru_lambda[l][None, :])

        w_router = jnp.pad(jnp.concatenate([moe_wrg[l], moe_wre[l]], axis=1),
                           ((0, 0), (0, LANES - MOE_GROUPS - N_EXPERTS)))
        b_router = jnp.pad(jnp.concatenate([moe_brg[l], moe_bre[l]]),
                           (0, LANES - MOE_GROUPS - N_EXPERTS))[None, :]
        h1, xn2, gates = _merge(
            h, norm1[l][None, :], o_a.reshape(n, -1), o_b.reshape(n, -1), o_c.reshape(n, -1),
            o_d.reshape(n, -1), w_gate[l].astype(BF16), b_gate[l],
            _pad_heads(w_branch[l, 0], A_DV, 0).astype(BF16), w_branch[l, 1].astype(BF16),
            w_branch[l, 2].astype(BF16), w_branch[l, 3].astype(BF16), w_out[l].astype(BF16),
            norm2[l][None, :], w_router.astype(BF16), b_router)
        h = _moe(xn2, gates, h1, moe_wg[l].astype(BF16), moe_wu[l].astype(BF16),
                 moe_wd[l].astype(BF16), norm_f[None, :], final_norm=(l == depth - 1))
    return h.reshape(bsz, t, D_MODEL)
```
